```python
import jax, jax.numpy as jnp
from jax import lax
import numpy as np

D_MODEL = 1024
BATCH = 4
SEQ = 8192
DEPTH = 1

N_MEM = 256
EPS = 1e-6
ROPE_THETA = 10000.0
NEG_INF = -1e30
FORCE_SCORE = 1e9

NSA_HEADS = 8
NSA_KV_GROUPS = 2
NSA_Q_PER_GROUP = NSA_HEADS // NSA_KV_GROUPS
NSA_HEAD_DIM = 64
CMP_BLOCK = 32
CMP_STRIDE = 16
CMP_HIDDEN = 256
SLC_BLOCK = 64
SLC_TOPK = 16
N_LOCAL_BLOCKS = 2
WINDOW = 512
Q_BLOCK = 128
NSA_Q_W = NSA_HEADS * NSA_HEAD_DIM
NSA_KV_W = NSA_KV_GROUPS * NSA_HEAD_DIM

HGRN_HEADS = 4
HGRN_DK = 128
HGRN_DV = 128
HGRN_CHUNK = 64
HGRN_KW = HGRN_HEADS * HGRN_DK
HGRN_VW = HGRN_HEADS * HGRN_DV

XA_HEADS = 4
XA_HEAD_DIM = 128
XA_W = XA_HEADS * XA_HEAD_DIM

N_EXPERTS = 32
TOP_K = 4
D_EXPERT = 1024
SWIGLU_ALPHA = 1.702
SWIGLU_LIMIT = 7.0

IN_WIDTHS = (D_MODEL, D_MODEL, NSA_Q_W, NSA_KV_W, NSA_KV_W, NSA_KV_W, NSA_KV_W, NSA_KV_W, NSA_KV_W, 3 * NSA_HEADS, HGRN_KW, HGRN_KW, HGRN_VW, HGRN_VW)
IN_WIDTH = sum(IN_WIDTHS)

kernel_name = 'hybrid_nsa_hgrn2_moe_block'


def _split_points():
    return [int(v) for v in np.cumsum(IN_WIDTHS)[:-1]]


def _rmsnorm(x, g):
    xf = x.astype(jnp.float32)
    y = xf * lax.rsqrt(jnp.mean(xf * xf, axis=-1, keepdims=True) + EPS)
    return (y * g.astype(jnp.float32)).astype(x.dtype)


def _rope(x, pos):
    half = x.shape[-1] // 2
    inv_freq = ROPE_THETA ** (-jnp.arange(half, dtype=jnp.float32) / half)
    ang = pos.astype(jnp.float32)[..., None] * inv_freq
    cos, sin = jnp.cos(ang), jnp.sin(ang)
    xf = x.astype(jnp.float32)
    x1, x2 = xf[..., :half], xf[..., half:]
    return jnp.concatenate([x1 * cos - x2 * sin, x2 * cos + x1 * sin], axis=-1).astype(x.dtype)


def _compress(kv, pe, w1, b1, w2, b2):
    S = kv.shape[2]
    n_cmp = (S - CMP_BLOCK) // CMP_STRIDE + 1
    idx = jnp.arange(n_cmp)[:, None] * CMP_STRIDE + jnp.arange(CMP_BLOCK)[None, :]
    blocks = kv[:, :, idx] + pe
    flat = blocks.reshape(blocks.shape[0], blocks.shape[1], n_cmp, CMP_BLOCK * NSA_HEAD_DIM)
    return jax.nn.gelu(flat @ w1 + b1) @ w2 + b2


def _nsa(q, k_cmp, v_cmp, k_slc, v_slc, k_win, v_win, gates):
    B, G, Hg, S, dh = q.shape
    scale = dh ** -0.5
    n_cmp = k_cmp.shape[2]
    n_slc = S // SLC_BLOCK
    top_k = min(SLC_TOPK, n_slc)
    cmp_start = jnp.arange(n_cmp) * CMP_STRIDE
    cmp_end = cmp_start + CMP_BLOCK - 1
    slc_ids = jnp.arange(n_slc)
    slc_start = slc_ids * SLC_BLOCK
    overlap = ((cmp_start[:, None] < slc_start[None, :] + SLC_BLOCK) & (cmp_start[:, None] + CMP_BLOCK > slc_start[None, :])).astype(jnp.float32)
    ks_blocks = k_slc.reshape(B, G, n_slc, SLC_BLOCK, dh)
    vs_blocks = v_slc.reshape(B, G, n_slc, SLC_BLOCK, dh)
    kw_pad = jnp.pad(k_win, ((0, 0), (0, 0), (WINDOW, 0), (0, 0)))
    vw_pad = jnp.pad(v_win, ((0, 0), (0, 0), (WINDOW, 0), (0, 0)))
    b_ix = jnp.arange(B)[:, None, None, None]
    g_ix = jnp.arange(G)[None, :, None, None]

    def block(qb):
        s0 = qb * Q_BLOCK
        t = s0 + jnp.arange(Q_BLOCK)
        qblk = lax.dynamic_slice_in_dim(q, s0, Q_BLOCK, axis=3)
        gblk = lax.dynamic_slice_in_dim(gates, s0, Q_BLOCK, axis=3)
        sc = jnp.einsum('bghtd,bgnd->bghtn', qblk, k_cmp).astype(jnp.float32) * scale
        valid_c = cmp_end[None, :] <= t[:, None]
        p_c = jnp.where(valid_c, jax.nn.softmax(jnp.where(valid_c, sc, NEG_INF), axis=-1), 0.0)
        o_c = jnp.einsum('bghtn,bgnd->bghtd', p_c.astype(v_cmp.dtype), v_cmp)
        imp = jnp.einsum('bghtn,nj->bgtj', p_c, overlap)
        cur = t // SLC_BLOCK
        causal_b = slc_ids[None, :] <= cur[:, None]
        forced = (slc_ids[None, :] == 0) | (causal_b & (slc_ids[None, :] > cur[:, None] - N_LOCAL_BLOCKS))
        score = jnp.where(forced, FORCE_SCORE, jnp.where(causal_b, imp, -1.0))
        _, sel = lax.top_k(score, top_k)
        k_sel = ks_blocks[b_ix, g_ix, sel].reshape(B, G, Q_BLOCK, top_k * SLC_BLOCK, dh)
        v_sel = vs_blocks[b_ix, g_ix, sel].reshape(B, G, Q_BLOCK, top_k * SLC_BLOCK, dh)
        kpos = (sel[..., None] * SLC_BLOCK + jnp.arange(SLC_BLOCK)).reshape(B, G, Q_BLOCK, top_k * SLC_BLOCK)
        valid_s = (kpos <= t[:, None])[:, :, None]
        ss = jnp.einsum('bghtd,bgtnd->bghtn', qblk, k_sel).astype(jnp.float32) * scale
        p_s = jax.nn.softmax(jnp.where(valid_s, ss, NEG_INF), axis=-1)
        o_s = jnp.einsum('bghtn,bgtnd->bghtd', p_s.astype(v_sel.dtype), v_sel)
        kwb = lax.dynamic_slice_in_dim(kw_pad, s0, WINDOW + Q_BLOCK, axis=2)
        vwb = lax.dynamic_slice_in_dim(vw_pad, s0, WINDOW + Q_BLOCK, axis=2)
        wpos = s0 - WINDOW + jnp.arange(WINDOW + Q_BLOCK)
        rel = t[:, None] - wpos[None, :]
        valid_w = (wpos[None, :] >= 0) & (rel >= 0) & (rel < WINDOW)
        sw = jnp.einsum('bghtd,bgnd->bghtn', qblk, kwb).astype(jnp.float32) * scale
        p_w = jax.nn.softmax(jnp.where(valid_w, sw, NEG_INF), axis=-1)
        o_w = jnp.einsum('bghtn,bgnd->bghtd', p_w.astype(vwb.dtype), vwb)
        out = gblk[..., 0:1] * o_c + gblk[..., 1:2] * o_s + gblk[..., 2:3] * o_w
        return out.astype(q.dtype)

    outs = lax.map(block, jnp.arange(S // Q_BLOCK))
    return outs.transpose(1, 0, 4, 2, 3, 5).reshape(B, S, G * Hg * dh)


def _hgrn2(q, log_f, k, v):
    B, S, H, dk = q.shape
    dv = v.shape[-1]
    n_chunk = S // HGRN_CHUNK

    def chunks(a):
        return a.reshape(B, n_chunk, HGRN_CHUNK, H, a.shape[-1]).transpose(1, 0, 3, 2, 4)

    causal = jnp.tril(jnp.ones((HGRN_CHUNK, HGRN_CHUNK), dtype=bool))[:, :, None]

    def step(state, inp):
        qc, lfc, kc, vc = inp
        b = jnp.cumsum(lfc, axis=2)
        o_inter = jnp.einsum('bhtk,bhkv->bhtv', qc * jnp.exp(b), state)
        decay = jnp.exp(jnp.where(causal, b[:, :, :, None, :] - b[:, :, None, :, :], -jnp.inf))
        attn = jnp.einsum('bhtk,bhsk,bhtsk->bhts', qc, kc, decay)
        o_intra = jnp.einsum('bhts,bhsv->bhtv', attn, vc)
        b_end = b[:, :, -1:, :]
        new_state = jnp.exp(b_end[:, :, 0, :])[..., None] * state + jnp.einsum('bhsk,bhsv->bhkv', kc * jnp.exp(b_end - b), vc)
        return new_state, o_inter + o_intra

    state0 = jnp.zeros((B, H, dk, dv), jnp.float32)
    _, o = lax.scan(step, state0, (chunks(q), chunks(log_f), chunks(k), chunks(v)))
    return o.transpose(1, 0, 3, 2, 4).reshape(B, S, H, dv)


def _clamped_swiglu(u):
    glu, lin = u[..., ::2], u[..., 1::2]
    glu = jnp.minimum(glu, SWIGLU_LIMIT)
    lin = jnp.clip(lin, -SWIGLU_LIMIT, SWIGLU_LIMIT)
    return glu * jax.nn.sigmoid(SWIGLU_ALPHA * glu) * (lin + 1.0)


def setup_inputs(seed: int = 0) -> dict:
    key = jax.random.key(seed)
    ks = iter(jax.random.split(key, 40))
    f32 = jnp.float32
    L, D = DEPTH, D_MODEL

    def nrm(shape, scale):
        return jax.random.normal(next(ks), shape, f32) * scale

    def gain(shape):
        return 1.0 + 0.05 * jax.random.normal(next(ks), shape, f32)

    cmp_in = CMP_BLOCK * NSA_HEAD_DIM
    offsets = jax.random.randint(next(ks), (BATCH, 1), 0, 1024, dtype=jnp.int32)
    positions = offsets + jnp.arange(SEQ, dtype=jnp.int32)[None, :]
    return {
        'x': nrm((BATCH, SEQ, D), 1.0),
        'mem': nrm((BATCH, N_MEM, D), 1.0),
        'positions': positions,
        'mix_norm_g': gain((L, D)),
        'w_in': nrm((L, D, IN_WIDTH), D ** -0.5),
        'cmp_pe': nrm((L, 2, CMP_BLOCK, NSA_HEAD_DIM), 0.1),
        'cmp_w1': nrm((L, 2, cmp_in, CMP_HIDDEN), cmp_in ** -0.5),
        'cmp_b1': nrm((L, 2, CMP_HIDDEN), 0.02),
        'cmp_w2': nrm((L, 2, CMP_HIDDEN, NSA_HEAD_DIM), CMP_HIDDEN ** -0.5),
        'cmp_b2': nrm((L, 2, NSA_HEAD_DIM), 0.02),
        'hgrn_lb_logits': nrm((L + 1, HGRN_KW), 0.5),
        'hgrn_norm_g': gain((L, HGRN_DV)),
        'w_up_nsa': nrm((L, NSA_Q_W, D), NSA_Q_W ** -0.5),
        'w_up_hgrn': nrm((L, HGRN_VW, D), HGRN_VW ** -0.5),
        'w_out': nrm((L, D, D), D ** -0.5),
        'xa_norm_g': gain((L, D)),
        'xa_mem_norm_g': gain((L, D)),
        'w_xq': nrm((L, D, XA_W), D ** -0.5),
        'w_xkv': nrm((L, D, 2 * XA_W), D ** -0.5),
        'w_xo': nrm((L, XA_W, D), XA_W ** -0.5),
        'moe_norm_g': gain((L, D)),
        'router_w': nrm((L, D, N_EXPERTS), D ** -0.5),
        'router_b': nrm((L, N_EXPERTS), 0.01),
        'moe_w1': nrm((L, N_EXPERTS, D, 2 * D_EXPERT), D ** -0.5),
        'moe_b1': nrm((L, N_EXPERTS, 2 * D_EXPERT), 0.02),
        'moe_w2': nrm((L, N_EXPERTS, D_EXPERT, D), D_EXPERT ** -0.5),
        'moe_b2': nrm((L, N_EXPERTS, D), 0.02),
        'final_norm_g': gain((D,)),
    }


def reference(x, mem, positions, mix_norm_g, w_in, cmp_pe, cmp_w1, cmp_b1, cmp_w2, cmp_b2, hgrn_lb_logits, hgrn_norm_g, w_up_nsa, w_up_hgrn, w_out, xa_norm_g, xa_mem_norm_g, w_xq, w_xkv, w_xo, moe_norm_g, router_w, router_b, moe_w1, moe_b1, moe_w2, moe_b2, final_norm_g):
    B, S, D = x.shape
    G, Hg, dh = NSA_KV_GROUPS, NSA_Q_PER_GROUP, NSA_HEAD_DIM
    lb_all = jnp.cumsum(jax.nn.softmax(hgrn_lb_logits.astype(jnp.float32), axis=0), axis=0)
    n_cmp = (S - CMP_BLOCK) // CMP_STRIDE + 1
    cmp_end_idx = jnp.arange(n_cmp) * CMP_STRIDE + CMP_BLOCK - 1

    def heads_q(a):
        return a.reshape(B, S, G, Hg, dh).transpose(0, 2, 3, 1, 4)

    def heads_kv(a):
        return a.reshape(B, S, G, dh).transpose(0, 2, 1, 3)

    for l in range(DEPTH):
        h = _rmsnorm(x, mix_norm_g[l])
        (g_a, g_b, nq, kc, vc, ks, vs, kw, vw, ng, hq, hf, hi, hg) = jnp.split(h @ w_in[l], _split_points(), axis=-1)

        q = _rope(heads_q(nq), positions[:, None, None, :])
        k_cmp = _compress(heads_kv(kc), cmp_pe[l, 0], cmp_w1[l, 0], cmp_b1[l, 0], cmp_w2[l, 0], cmp_b2[l, 0])
        v_cmp = _compress(heads_kv(vc), cmp_pe[l, 1], cmp_w1[l, 1], cmp_b1[l, 1], cmp_w2[l, 1], cmp_b2[l, 1])
        k_cmp = _rope(k_cmp, positions[:, cmp_end_idx][:, None, :])
        k_slc = _rope(heads_kv(ks), positions[:, None, :])
        k_win = _rope(heads_kv(kw), positions[:, None, :])
        nsa_gates = jax.nn.sigmoid(ng.astype(jnp.float32)).reshape(B, S, G, Hg, 3).transpose(0, 2, 3, 1, 4)
        y_nsa = _nsa(q, k_cmp, v_cmp, k_slc, heads_kv(vs), k_win, heads_kv(vw), nsa_gates)

        lb = lb_all[l].reshape(HGRN_HEADS, HGRN_DK)
        f = lb + (1.0 - lb) * jax.nn.sigmoid(hf.astype(jnp.float32).reshape(B, S, HGRN_HEADS, HGRN_DK))
        o = _hgrn2(hq.astype(jnp.float32).reshape(B, S, HGRN_HEADS, HGRN_DK), jnp.log(f), 1.0 - f, hi.astype(jnp.float32).reshape(B, S, HGRN_HEADS, HGRN_DV))
        o = _rmsnorm(o, hgrn_norm_g[l]) * jax.nn.silu(hg.astype(jnp.float32).reshape(B, S, HGRN_HEADS, HGRN_DV))
        y_hgrn = o.reshape(B, S, HGRN_VW).astype(x.dtype)

        mixed = jax.nn.sigmoid(g_a) * (y_nsa @ w_up_nsa[l]) + jax.nn.sigmoid(g_b) * (y_hgrn @ w_up_hgrn[l])
        x = x + mixed @ w_out[l]

        hx = _rmsnorm(x, xa_norm_g[l])
        m = _rmsnorm(mem, xa_mem_norm_g[l])
        xq = (hx @ w_xq[l]).reshape(B, S, XA_HEADS, XA_HEAD_DIM)
        xk, xv = jnp.split(m @ w_xkv[l], 2, axis=-1)
        xk = xk.reshape(B, -1, XA_HEADS, XA_HEAD_DIM)
        xv = xv.reshape(B, -1, XA_HEADS, XA_HEAD_DIM)
        s_x = jnp.einsum('bqhd,bmhd->bhqm', xq, xk).astype(jnp.float32) * (XA_HEAD_DIM ** -0.5)
        p_x = jax.nn.softmax(s_x, axis=-1)
        o_x = jnp.einsum('bhqm,bmhd->bqhd', p_x.astype(xv.dtype), xv).reshape(B, S, XA_W)
        x = x + o_x @ w_xo[l]

        hm = _rmsnorm(x, moe_norm_g[l]).reshape(B * S, D)
        logits = (hm @ router_w[l] + router_b[l]).astype(jnp.float32)
        top_vals, top_idx = lax.top_k(logits, TOP_K)
        top_w = jax.nn.softmax(top_vals, axis=-1)
        combine = jnp.sum(jax.nn.one_hot(top_idx, N_EXPERTS, dtype=jnp.float32) * top_w[..., None], axis=-2)
        y = jnp.zeros_like(hm)
        for e in range(N_EXPERTS):
            u = hm @ moe_w1[l, e] + moe_b1[l, e]
            y = y + combine[:, e:e + 1].astype(hm.dtype) * (_clamped_swiglu(u) @ moe_w2[l, e] + moe_b2[l, e])
        x = x + y.reshape(B, S, D)

    return _rmsnorm(x, final_norm_g)
```

```python
import functools

import jax
import jax.numpy as jnp
from jax import lax
from jax.experimental import pallas as pl
from jax.experimental.pallas import tpu as pltpu

EPS = 1e-6
ROPE_THETA = 10000.0
NEG_INF = -1e30
FORCE_SCORE = 1e9

NSA_HEADS = 8
NSA_KV_GROUPS = 2
NSA_Q_PER_GROUP = NSA_HEADS // NSA_KV_GROUPS
NSA_HEAD_DIM = 64
CMP_BLOCK = 32
CMP_STRIDE = 16
CMP_HIDDEN = 256
SLC_BLOCK = 64
SLC_TOPK = 16
N_LOCAL_BLOCKS = 2
WINDOW = 512
NSA_Q_W = NSA_HEADS * NSA_HEAD_DIM
NSA_KV_W = NSA_KV_GROUPS * NSA_HEAD_DIM

HGRN_HEADS = 4
HGRN_DK = 128
HGRN_DV = 128
HGRN_CHUNK = 64
HGRN_SUB = 16
HGRN_W = HGRN_HEADS * HGRN_DK

XA_HEADS = 4
XA_HEAD_DIM = 128
XA_W = XA_HEADS * XA_HEAD_DIM

N_EXPERTS = 32
TOP_K = 4
SWIGLU_ALPHA = 1.702
SWIGLU_LIMIT = 7.0

LANES = 128
VMEM_LIMIT = 48 * 1024 * 1024

COL_GA = 0
COL_GB = 1024
COL_HQ = 2048
COL_HF = 2560
COL_HI = 3072
COL_HG = 3584
COL_NQ = 4096
COL_KSKW = 4608
COL_KCVC = 4864
COL_VSVW = 5120
COL_NG = 5376
P_WIDTH = 5632

F32 = jnp.float32
BF16 = jnp.bfloat16


def _params(sem):
    return pltpu.CompilerParams(dimension_semantics=sem, vmem_limit_bytes=VMEM_LIMIT)


def _dot(a, b):
    return jnp.dot(a, b, preferred_element_type=F32)


def _dot_nt(a, b):
    return lax.dot_general(a, b, (((1,), (1,)), ((), ())), preferred_element_type=F32)


def _rms(xf, g):
    return xf * lax.rsqrt(jnp.mean(xf * xf, axis=-1, keepdims=True) + EPS) * g


def _sigmoid(x):
    return 1.0 / (1.0 + jnp.exp(-x))


def _norm_matmul_kernel(x_ref, g_ref, w_ref, o_ref, hn_ref):
    @pl.when(pl.program_id(1) == 0)
    def _():
        hn_ref[...] = _rms(x_ref[...], g_ref[...]).astype(BF16)

    o_ref[...] = _dot(hn_ref[...], w_ref[...]).astype(o_ref.dtype)


def _norm_matmul(x, g, w, out_dtype, tm, tn, name):
    t, d = x.shape
    n = w.shape[1]
    return pl.pallas_call(
        _norm_matmul_kernel,
        out_shape=jax.ShapeDtypeStruct((t, n), out_dtype),
        grid=(t // tm, n // tn),
        in_specs=[
            pl.BlockSpec((tm, d), lambda i, j: (i, 0)),
            pl.BlockSpec((1, d), lambda i, j: (0, 0)),
            pl.BlockSpec((d, tn), lambda i, j: (0, j)),
        ],
        out_specs=pl.BlockSpec((tm, tn), lambda i, j: (i, j)),
        scratch_shapes=[pltpu.VMEM((tm, d), BF16)],
        compiler_params=_params(("arbitrary", "arbitrary")),
        name=name,
    )(x, g, w)


def _rope_coeffs(pos_col, invf):
    ang = pos_col.astype(F32) * invf
    lane = lax.broadcasted_iota(jnp.int32, ang.shape, 1)
    first = (lane & (NSA_HEAD_DIM - 1)) < (NSA_HEAD_DIM // 2)
    c = jnp.cos(ang)
    s = jnp.sin(ang)
    return c, jnp.where(first, -s, s), first


def _rope_tile(x, c, s_signed, first):
    half = NSA_HEAD_DIM // 2
    partner = jnp.where(first, pltpu.roll(x, LANES - half, 1), pltpu.roll(x, half, 1))
    return x * c + partner * s_signed


def _rope_kernel(q_ref, k_ref, pos_ref, invf_ref, qo_ref, ko_ref, *, q_scale):
    c, s_signed, first = _rope_coeffs(pos_ref[...], invf_ref[...])
    for i in range(q_ref.shape[1] // LANES):
        sl = slice(i * LANES, (i + 1) * LANES)
        qo_ref[:, sl] = (_rope_tile(q_ref[:, sl], c, s_signed, first) * q_scale).astype(BF16)
    for i in range(k_ref.shape[1] // LANES):
        sl = slice(i * LANES, (i + 1) * LANES)
        ko_ref[:, sl] = _rope_tile(k_ref[:, sl], c, s_signed, first).astype(BF16)


def _rope(p, pos_col, invf, tm):
    t = p.shape[0]
    kw = 2 * NSA_KV_W
    return pl.pallas_call(
        functools.partial(_rope_kernel, q_scale=NSA_HEAD_DIM ** -0.5),
        out_shape=(jax.ShapeDtypeStruct((t, NSA_Q_W), BF16), jax.ShapeDtypeStruct((t, kw), BF16)),
        grid=(t // tm,),
        in_specs=[
            pl.BlockSpec((tm, NSA_Q_W), lambda i: (i, COL_NQ // NSA_Q_W)),
            pl.BlockSpec((tm, kw), lambda i: (i, COL_KSKW // kw)),
            pl.BlockSpec((tm, 1), lambda i: (i, 0)),
            pl.BlockSpec((1, LANES), lambda i: (0, 0)),
        ],
        out_specs=(
            pl.BlockSpec((tm, NSA_Q_W), lambda i: (i, 0)),
            pl.BlockSpec((tm, kw), lambda i: (i, 0)),
        ),
        compiler_params=_params(("arbitrary",)),
        name="rope",
    )(p, p, pos_col, invf)


def _gelu_tanh(x):
    return 0.5 * x * (1.0 + jnp.tanh(0.7978845608028654 * (x + 0.044715 * (x * x * x))))


def _compress_kernel(r_ref, pe_ref, w1_ref, b1_ref, w2_ref, b2_ref, pos_ref, invf_ref, o_ref):
    nr = r_ref.shape[3]
    half = r_ref.shape[4]
    acc = None
    for g in range(NSA_KV_GROUPS):
        r = r_ref[0, 0, g]
        top = _dot((r + pe_ref[0, 0]).astype(BF16), w1_ref[0, :half, :])
        bot = _dot((r + pe_ref[0, 1]).astype(BF16), w1_ref[0, half:, :])
        pre = top + pltpu.roll(bot, nr - 1, 0) + b1_ref[0]
        part = _dot(_gelu_tanh(pre).astype(BF16), w2_ref[0, g])
        acc = part if acc is None else acc + part
    out = acc + b2_ref[0]
    c, s_signed, first = _rope_coeffs(pos_ref[0], invf_ref[...])
    roped = _rope_tile(out, c, s_signed, first)
    is_key = pl.program_id(0) == 0
    o_ref[0, 0] = jnp.where(is_key, roped, out).astype(BF16)


def _compress(r, pe, w1, b1, w2p, b2t, pos_cmp, invf):
    _, b, g, nr, half = r.shape
    return pl.pallas_call(
        _compress_kernel,
        out_shape=jax.ShapeDtypeStruct((2, b, nr, LANES), BF16),
        grid=(2, b),
        in_specs=[
            pl.BlockSpec((1, 1, g, nr, half), lambda k, i: (k, i, 0, 0, 0)),
            pl.BlockSpec((1, 2, 1, half), lambda k, i: (k, 0, 0, 0)),
            pl.BlockSpec((1, 2 * half, CMP_HIDDEN), lambda k, i: (k, 0, 0)),
            pl.BlockSpec((1, 1, CMP_HIDDEN), lambda k, i: (k, 0, 0)),
            pl.BlockSpec((1, g, CMP_HIDDEN, LANES), lambda k, i: (k, 0, 0, 0)),
            pl.BlockSpec((1, 1, LANES), lambda k, i: (k, 0, 0)),
            pl.BlockSpec((1, nr, 1), lambda k, i: (i, 0, 0)),
            pl.BlockSpec((1, LANES), lambda k, i: (0, 0)),
        ],
        out_specs=pl.BlockSpec((1, 1, nr, LANES), lambda k, i: (k, i, 0, 0)),
        compiler_params=_params(("arbitrary", "arbitrary")),
        name="compress",
    )(r, pe, w1, b1, w2p, b2t, pos_cmp, invf)


def _softmax_step(s, valid, v, m, l, acc):
    sm = jnp.where(valid, s, NEG_INF)
    m_new = jnp.maximum(m, jnp.max(sm, axis=-1, keepdims=True))
    alpha = jnp.exp(m - m_new)
    p = jnp.where(valid, jnp.exp(sm - m_new), 0.0)
    l_new = alpha * l + jnp.sum(p, axis=-1, keepdims=True)
    acc_new = alpha * acc + _dot(p.astype(BF16), v)
    return m_new, l_new, acc_new


def _nsa_kernel(q_ref, kc_ref, vc_ref, ks_ref, vs_ref, kw_ref, vw_ref, g_ref, o_ref, *, tq, tk, seq):
    hg = NSA_Q_PER_GROUP
    rows = hg * tq
    n_cmp_rows = kc_ref.shape[2]
    nb = seq // SLC_BLOCK
    top_k = min(SLC_TOPK, nb)
    win_keys = WINDOW + tq
    s0 = pl.program_id(2) * tq

    q = q_ref[0, 0].reshape(rows, NSA_HEAD_DIM)
    row = lax.broadcasted_iota(jnp.int32, (rows, 1), 0)
    t_row = s0 + (row & (tq - 1))
    t_tok = s0 + lax.broadcasted_iota(jnp.int32, (tq, 1), 0)

    sc = _dot_nt(q, kc_ref[0, 0])
    n_idx = lax.broadcasted_iota(jnp.int32, (1, n_cmp_rows), 1)
    valid_c = (n_idx * CMP_STRIDE + (CMP_BLOCK - 1) <= t_row) & (n_idx < n_cmp_rows - 1)
    scm = jnp.where(valid_c, sc, NEG_INF)
    e = jnp.where(valid_c, jnp.exp(scm - jnp.max(scm, axis=-1, keepdims=True)), 0.0)
    den = jnp.sum(e, axis=-1, keepdims=True)
    p_c = e / jnp.where(den > 0.0, den, 1.0)
    o_c = _dot(p_c.astype(BF16), vc_ref[0, 0])

    p_sum = p_c[0:tq]
    for h in range(1, hg):
        p_sum = p_sum + p_c[h * tq:(h + 1) * tq]
    n_col = lax.broadcasted_iota(jnp.int32, (n_cmp_rows, 1), 0) * CMP_STRIDE
    j_row = lax.broadcasted_iota(jnp.int32, (1, nb), 1)
    overlap = ((n_col < j_row * SLC_BLOCK + SLC_BLOCK) & (n_col + CMP_BLOCK > j_row * SLC_BLOCK))
    overlap = jnp.where(overlap, 1.0, 0.0).astype(BF16)
    p_hi = p_sum.astype(BF16)
    p_lo = (p_sum - p_hi.astype(F32)).astype(BF16)
    imp = _dot(p_hi, overlap) + _dot(p_lo, overlap)

    cur = t_tok >> 6
    causal_b = j_row <= cur
    forced = (j_row == 0) | (causal_b & (j_row > cur - N_LOCAL_BLOCKS))
    score = jnp.where(forced, FORCE_SCORE, jnp.where(causal_b, imp, -1.0))
    j_f = j_row.astype(F32)
    sel = jnp.zeros((tq, nb), F32)
    for _ in range(top_k):
        mx = jnp.max(score, axis=-1, keepdims=True)
        first_idx = jnp.min(jnp.where(score == mx, j_f, float(nb)), axis=-1, keepdims=True)
        hit = j_f == first_idx
        sel = jnp.where(hit, 1.0, sel)
        score = jnp.where(hit, -jnp.inf, score)
    sel = jnp.where(causal_b, sel, 0.0).astype(BF16)

    jb_col = lax.broadcasted_iota(jnp.int32, (nb, 1), 0)
    c_row = lax.broadcasted_iota(jnp.int32, (1, tk), 1)

    def slc_body(kt, carry):
        k0 = pl.multiple_of(kt * tk, tk)
        kpos = k0 + c_row
        expand = jnp.where(jb_col == (kpos >> 6), 1.0, 0.0).astype(BF16)
        blk = _dot(sel, expand)
        ok = jnp.where((blk > 0.5) & (kpos <= t_tok), 1.0, 0.0)
        valid = jnp.concatenate([ok] * hg, axis=0) > 0.5
        s = _dot_nt(q, ks_ref[0, 0, pl.ds(k0, tk), :])
        return _softmax_step(s, valid, vs_ref[0, 0, pl.ds(k0, tk), :], *carry)

    init = (jnp.full((rows, 1), NEG_INF, F32), jnp.zeros((rows, 1), F32), jnp.zeros((rows, NSA_HEAD_DIM), F32))
    n_kt = (s0 + tq + tk - 1) // tk
    _, l_s, acc_s = lax.fori_loop(0, n_kt, slc_body, init)

    w0 = pl.multiple_of(jnp.maximum(s0 - WINDOW, 0), tq)
    wpos = w0 + lax.broadcasted_iota(jnp.int32, (1, win_keys), 1)
    valid_w = (wpos <= t_row) & (wpos > t_row - WINDOW)
    sw = _dot_nt(q, kw_ref[0, 0, pl.ds(w0, win_keys), :])
    _, l_w, acc_w = _softmax_step(sw, valid_w, vw_ref[0, 0, pl.ds(w0, win_keys), :], *init)

    gate = _sigmoid(g_ref[0, 0].reshape(rows, 3))
    out = gate[:, 0:1] * o_c + gate[:, 1:2] * (acc_s / l_s) + gate[:, 2:3] * (acc_w / l_w)
    o_ref[0, 0] = out.reshape(hg, tq, NSA_HEAD_DIM).astype(o_ref.dtype)


def _nsa(q5, kc, vc, ks, vs, kw, vw, gates, tq, tk):
    b, g, hg, s, dh = q5.shape
    nr = kc.shape[2]
    kv_spec = pl.BlockSpec((1, 1, s, dh), lambda i, j, k: (i, j, 0, 0))
    cmp_spec = pl.BlockSpec((1, 1, nr, dh), lambda i, j, k: (i, j, 0, 0))
    return pl.pallas_call(
        functools.partial(_nsa_kernel, tq=tq, tk=tk, seq=s),
        out_shape=jax.ShapeDtypeStruct((b, g, hg, s, dh), BF16),
        grid=(b, g, s // tq),
        in_specs=[
            pl.BlockSpec((1, 1, hg, tq, dh), lambda i, j, k: (i, j, 0, k, 0)),
            cmp_spec, cmp_spec, kv_spec, kv_spec, kv_spec, kv_spec,
            pl.BlockSpec((1, 1, hg, tq, 3), lambda i, j, k: (i, j, 0, k, 0)),
        ],
        out_specs=pl.BlockSpec((1, 1, hg, tq, dh), lambda i, j, k: (i, j, 0, k, 0)),
        compiler_params=_params(("arbitrary", "arbitrary", "arbitrary")),
        name="nsa",
    )(q5, kc, vc, ks, vs, kw, vw, gates)


def _cumsum_rows(x):
    n = x.shape[0]
    row = lax.broadcasted_iota(jnp.int32, x.shape, 0)
    d = 1
    while d < n:
        x = x + jnp.where(row >= d, pltpu.roll(x, d, 0), 0.0)
        d *= 2
    return x


def _hgrn_kernel(q_ref, f_ref, i_ref, g_ref, lb_ref, gn_ref, o_ref, st_ref, *, n_chunks):
    @pl.when(pl.program_id(1) == 0)
    def _():
        st_ref[...] = jnp.zeros_like(st_ref)

    c_len = HGRN_CHUNK
    sub = HGRN_SUB
    lbl = lb_ref[...]
    lb_e = jnp.exp(lbl - jnp.max(lbl, axis=0, keepdims=True))
    lb_all = lb_e[0:1] / jnp.sum(lb_e, axis=0, keepdims=True)

    for c in range(n_chunks):
        rs = slice(c * c_len, (c + 1) * c_len)
        for h in range(HGRN_HEADS):
            ls = slice(h * HGRN_DK, (h + 1) * HGRN_DK)
            lb = lb_all[:, ls]
            f = lb + (1.0 - lb) * _sigmoid(f_ref[rs, ls])
            k = 1.0 - f
            b = _cumsum_rows(jnp.log(f))
            q = q_ref[rs, ls]
            v32 = i_ref[rs, ls]
            v = v32.astype(BF16)
            st = st_ref[h]
            o_inter = _dot_nt((q * jnp.exp(b)).astype(BF16), st.astype(BF16))
            b_end = b[c_len - 1:c_len]
            pieces = []
            for blk in range(c_len // sub):
                lo, hi = blk * sub, (blk + 1) * sub
                beta = b[lo - 1:lo] if blk > 0 else jnp.zeros_like(b_end)
                qd = (q[lo:hi] * jnp.exp(b[lo:hi] - beta)).astype(BF16)
                kd = (k[:hi] * jnp.exp(beta - b[:hi])).astype(BF16)
                a = _dot_nt(qd, kd)
                ti = lax.broadcasted_iota(jnp.int32, (sub, hi), 0)
                si = lax.broadcasted_iota(jnp.int32, (sub, hi), 1)
                a = jnp.where(si <= ti + lo, a, 0.0)
                pieces.append(_dot(a.astype(BF16), v[:hi]))
            o = o_inter + jnp.concatenate(pieces, axis=0)
            kd_end = (k * jnp.exp(b_end - b)).astype(BF16)
            st_ref[h] = st * jnp.exp(b_end) + _dot(v32.T.astype(BF16), kd_end)
            y = _rms(o, gn_ref[...])
            gate = g_ref[rs, ls]
            o_ref[rs, ls] = (y * (gate * _sigmoid(gate))).astype(o_ref.dtype)


def _hgrn(p, lb_logits, gn, batch, seq, n_chunks):
    t = p.shape[0]
    rows = n_chunks * HGRN_CHUNK
    steps = seq // rows

    def col(cb):
        return pl.BlockSpec((rows, HGRN_W), lambda i, j: (i * steps + j, cb))

    return pl.pallas_call(
        functools.partial(_hgrn_kernel, n_chunks=n_chunks),
        out_shape=jax.ShapeDtypeStruct((t, HGRN_W), BF16),
        grid=(batch, steps),
        in_specs=[
            col(COL_HQ // HGRN_W), col(COL_HF // HGRN_W), col(COL_HI // HGRN_W), col(COL_HG // HGRN_W),
            pl.BlockSpec(lb_logits.shape, lambda i, j: (0, 0)),
            pl.BlockSpec((1, HGRN_DV), lambda i, j: (0, 0)),
        ],
        out_specs=pl.BlockSpec((rows, HGRN_W), lambda i, j: (i * steps + j, 0)),
        scratch_shapes=[pltpu.VMEM((HGRN_HEADS, HGRN_DV, HGRN_DK), F32)],
        compiler_params=_params(("arbitrary", "arbitrary")),
        name="hgrn",
    )(p, p, p, p, lb_logits, gn)


def _merge_kernel(x_ref, ga_ref, gb_ref, yn_ref, yh_ref, wn_ref, wh_ref, wo_ref, o_ref):
    mixed = _sigmoid(ga_ref[...]) * _dot(yn_ref[...], wn_ref[...]) + _sigmoid(gb_ref[...]) * _dot(yh_ref[...], wh_ref[...])
    o_ref[...] = x_ref[...] + _dot(mixed.astype(BF16), wo_ref[...])


def _merge(x, p, y_nsa, y_hgrn, wn, wh, wo, tm):
    t, d = x.shape
    full = lambda a: pl.BlockSpec(a.shape, lambda i: (0, 0))
    return pl.pallas_call(
        _merge_kernel,
        out_shape=jax.ShapeDtypeStruct((t, d), F32),
        grid=(t // tm,),
        in_specs=[
            pl.BlockSpec((tm, d), lambda i: (i, 0)),
            pl.BlockSpec((tm, d), lambda i: (i, COL_GA // d)),
            pl.BlockSpec((tm, d), lambda i: (i, COL_GB // d)),
            pl.BlockSpec((tm, NSA_Q_W), lambda i: (i, 0)),
            pl.BlockSpec((tm, HGRN_W), lambda i: (i, 0)),
            full(wn), full(wh), full(wo),
        ],
        out_specs=pl.BlockSpec((tm, d), lambda i: (i, 0)),
        compiler_params=_params(("arbitrary",)),
        name="merge",
    )(x, p, p, y_nsa, y_hgrn, wn, wh, wo)


def _xattn_kernel(x_ref, g_ref, wq_ref, kv_ref, wo_ref, o_ref):
    x = x_ref[...]
    xq = _dot(_rms(x, g_ref[...]).astype(BF16), wq_ref[...]).astype(BF16)
    outs = []
    for h in range(XA_HEADS):
        ls = slice(h * XA_HEAD_DIM, (h + 1) * XA_HEAD_DIM)
        s = _dot_nt(xq[:, ls], kv_ref[0, :, ls]) * (XA_HEAD_DIM ** -0.5)
        e = jnp.exp(s - jnp.max(s, axis=-1, keepdims=True))
        p = e / jnp.sum(e, axis=-1, keepdims=True)
        outs.append(_dot(p.astype(BF16), kv_ref[0, :, XA_W + h * XA_HEAD_DIM:XA_W + (h + 1) * XA_HEAD_DIM]))
    o_x = jnp.concatenate(outs, axis=-1)
    o_ref[...] = x + _dot(o_x.astype(BF16), wo_ref[...])


def _xattn(x, g, wq, kv, wo, seq, tm):
    t, d = x.shape
    steps = seq // tm
    full = lambda a: pl.BlockSpec(a.shape, lambda i: (0, 0))
    return pl.pallas_call(
        _xattn_kernel,
        out_shape=jax.ShapeDtypeStruct((t, d), F32),
        grid=(t // tm,),
        in_specs=[
            pl.BlockSpec((tm, d), lambda i: (i, 0)),
            full(g), full(wq),
            pl.BlockSpec((1,) + kv.shape[1:], lambda i: (i // steps, 0, 0)),
            full(wo),
        ],
        out_specs=pl.BlockSpec((tm, d), lambda i: (i, 0)),
        compiler_params=_params(("arbitrary",)),
        name="xattn",
    )(x, g, wq, kv, wo)


def _router_kernel(x_ref, g_ref, w_ref, b_ref, hm_ref, idx_ref, rank_ref, wt_ref, cnt_ref):
    @pl.when(pl.program_id(0) == 0)
    def _():
        cnt_ref[...] = jnp.zeros_like(cnt_ref)

    tm = x_ref.shape[0]
    hm = _rms(x_ref[...], g_ref[...])
    hm_ref[...] = hm
    lane = lax.broadcasted_iota(jnp.int32, (tm, LANES), 1)
    lane_f = lane.astype(F32)
    logits = _dot(hm.astype(BF16), w_ref[...]) + b_ref[...]
    logits = jnp.where(lane < N_EXPERTS, logits, -jnp.inf)
    picks, vals = [], []
    onehot_all = jnp.zeros((tm, LANES), F32)
    for _ in range(TOP_K):
        mx = jnp.max(logits, axis=-1, keepdims=True)
        first_idx = jnp.min(jnp.where(logits == mx, lane_f, float(LANES)), axis=-1, keepdims=True)
        hit = lane_f == first_idx
        onehot = jnp.where(hit, 1.0, 0.0)
        logits = jnp.where(hit, -jnp.inf, logits)
        picks.append((first_idx, onehot))
        vals.append(mx)
        onehot_all = onehot_all + onehot
    exps = [jnp.exp(v - vals[0]) for v in vals]
    den = exps[0]
    for e in exps[1:]:
        den = den + e
    r_i = lax.broadcasted_iota(jnp.int32, (tm, tm), 0)
    c_i = lax.broadcasted_iota(jnp.int32, (tm, tm), 1)
    lower = jnp.where(c_i < r_i, 1.0, 0.0).astype(BF16)
    before = _dot(lower, onehot_all.astype(BF16)) + cnt_ref[...]
    idx_out = jnp.zeros((tm, LANES), F32)
    rank_out = jnp.zeros((tm, LANES), F32)
    wt_out = jnp.zeros((tm, LANES), F32)
    for k in range(TOP_K):
        first_idx, onehot = picks[k]
        rank = jnp.sum(onehot * before, axis=-1, keepdims=True)
        idx_out = jnp.where(lane == k, first_idx, idx_out)
        rank_out = jnp.where(lane == k, rank, rank_out)
        wt_out = jnp.where(lane == k, exps[k] / den, wt_out)
    idx_ref[...] = idx_out.astype(jnp.int32)
    rank_ref[...] = rank_out.astype(jnp.int32)
    wt_ref[...] = wt_out
    cnt_ref[...] = cnt_ref[...] + jnp.sum(onehot_all, axis=0, keepdims=True)


def _router(x, g, w, b, tm):
    t, d = x.shape
    full = lambda a: pl.BlockSpec(a.shape, lambda i: (0, 0))
    lane_out = pl.BlockSpec((tm, LANES), lambda i: (i, 0))
    return pl.pallas_call(
        _router_kernel,
        out_shape=(
            jax.ShapeDtypeStruct((t, d), F32),
            jax.ShapeDtypeStruct((t, LANES), jnp.int32),
            jax.ShapeDtypeStruct((t, LANES), jnp.int32),
            jax.ShapeDtypeStruct((t, LANES), F32),
            jax.ShapeDtypeStruct((1, LANES), F32),
        ),
        grid=(t // tm,),
        in_specs=[pl.BlockSpec((tm, d), lambda i: (i, 0)), full(g), full(w), full(b)],
        out_specs=(pl.BlockSpec((tm, d), lambda i: (i, 0)), lane_out, lane_out, lane_out,
                   pl.BlockSpec((1, LANES), lambda i: (0, 0))),
        compiler_params=_params(("arbitrary",)),
        name="router",
    )(x, g, w, b)


def _dispatch_kernel(dest_ref, hm_ref, xs_in_ref, xs_ref, sem):
    del xs_in_ref
    n_slots = dest_ref.shape[0]
    tok0 = pl.program_id(0) * (n_slots // TOP_K)

    def row_copy(n):
        return pltpu.make_async_copy(
            hm_ref.at[pl.ds(tok0 + n // TOP_K, 1)], xs_ref.at[pl.ds(dest_ref[n], 1)], sem)

    def start(n, c):
        row_copy(n).start()
        return c

    def wait(n, c):
        row_copy(n).wait()
        return c

    lax.fori_loop(0, n_slots, start, 0)
    lax.fori_loop(0, n_slots, wait, 0)


def _dispatch(dest_flat, hm, xs_zero, slots_per_step):
    n = dest_flat.shape[0]
    return pl.pallas_call(
        _dispatch_kernel,
        out_shape=jax.ShapeDtypeStruct(xs_zero.shape, xs_zero.dtype),
        grid=(n // slots_per_step,),
        in_specs=[
            pl.BlockSpec((slots_per_step,), lambda i: (i,), memory_space=pltpu.SMEM),
            pl.BlockSpec(memory_space=pl.ANY),
            pl.BlockSpec(memory_space=pl.ANY),
        ],
        out_specs=pl.BlockSpec(memory_space=pl.ANY),
        scratch_shapes=[pltpu.SemaphoreType.DMA],
        input_output_aliases={2: 0},
        compiler_params=_params(("arbitrary",)),
        name="dispatch",
    )(dest_flat, hm, xs_zero)


def _ffn_kernel(te_ref, nu_ref, x_ref, w1_ref, b1_ref, w2_ref, b2_ref, o_ref):
    del te_ref
    used = pl.program_id(0) < nu_ref[0]

    @pl.when(used)
    def _():
        f = w2_ref.shape[1]
        u = _dot(x_ref[...].astype(BF16), w1_ref[0]) + b1_ref[0]
        glu = jnp.minimum(u[:, :f], SWIGLU_LIMIT)
        lin = jnp.clip(u[:, f:], -SWIGLU_LIMIT, SWIGLU_LIMIT)
        act = glu * _sigmoid(SWIGLU_ALPHA * glu) * (lin + 1.0)
        o_ref[...] = _dot(act.astype(BF16), w2_ref[0]) + b2_ref[0]

    @pl.when(jnp.logical_not(used))
    def _():
        o_ref[...] = jnp.zeros_like(o_ref)


def _ffn(tile_expert, n_used, xs, w1, b1, w2, b2, tm):
    n_pad, d = xs.shape
    f2 = w1.shape[2]
    f = w2.shape[1]
    grid_spec = pltpu.PrefetchScalarGridSpec(
        num_scalar_prefetch=2,
        grid=(n_pad // tm,),
        in_specs=[
            pl.BlockSpec((tm, d), lambda i, te, nu: (i, 0)),
            pl.BlockSpec((1, d, f2), lambda i, te, nu: (te[i], 0, 0)),
            pl.BlockSpec((1, 1, f2), lambda i, te, nu: (te[i], 0, 0)),
            pl.BlockSpec((1, f, d), lambda i, te, nu: (te[i], 0, 0)),
            pl.BlockSpec((1, 1, d), lambda i, te, nu: (te[i], 0, 0)),
        ],
        out_specs=pl.BlockSpec((tm, d), lambda i, te, nu: (i, 0)),
    )
    return pl.pallas_call(
        _ffn_kernel,
        out_shape=jax.ShapeDtypeStruct((n_pad, d), F32),
        grid_spec=grid_spec,
        compiler_params=_params(("arbitrary",)),
        name="expert_ffn",
    )(tile_expert, n_used, xs, w1, b1, w2, b2)


def _combine_kernel(dest_ref, ys_ref, x_ref, wt_ref, g_ref, o_ref, rows_ref, sem):
    n_slots = dest_ref.shape[0]

    def row_copy(n):
        return pltpu.make_async_copy(
            ys_ref.at[pl.ds(dest_ref[n], 1)], rows_ref.at[n % TOP_K, pl.ds(n // TOP_K, 1)], sem)

    def start(n, c):
        row_copy(n).start()
        return c

    def wait(n, c):
        row_copy(n).wait()
        return c

    lax.fori_loop(0, n_slots, start, 0)
    lax.fori_loop(0, n_slots, wait, 0)
    wt = wt_ref[...]
    y = wt[:, 0:1] * rows_ref[0]
    for k in range(1, TOP_K):
        y = y + wt[:, k:k + 1] * rows_ref[k]
    o_ref[...] = _rms(x_ref[...] + y, g_ref[...])


def _combine(dest_flat, ys, x, wt, g, tm):
    t, d = x.shape
    return pl.pallas_call(
        _combine_kernel,
        out_shape=jax.ShapeDtypeStruct((t, d), F32),
        grid=(t // tm,),
        in_specs=[
            pl.BlockSpec((tm * TOP_K,), lambda i: (i,), memory_space=pltpu.SMEM),
            pl.BlockSpec(memory_space=pl.ANY),
            pl.BlockSpec((tm, d), lambda i: (i, 0)),
            pl.BlockSpec((tm, LANES), lambda i: (i, 0)),
            pl.BlockSpec((1, d), lambda i: (0, 0)),
        ],
        out_specs=pl.BlockSpec((tm, d), lambda i: (i, 0)),
        scratch_shapes=[pltpu.VMEM((TOP_K, tm, d), F32), pltpu.SemaphoreType.DMA],
        compiler_params=_params(("arbitrary",)),
        name="combine",
    )(dest_flat, ys, x, wt, g)


def _tile_sizes(seq):
    return dict(
        tm_proj=512, tn_proj=512,
        tm_rope=512,
        tq=128, tk=min(512, seq),
        hgrn_chunks=4,
        tm_merge=512, tm_xattn=512, tm_router=512,
        dispatch_tokens=256,
        tm_ffn=512,
        tm_combine=256,
    )


def _layer(x, mem, positions, ts, mix_norm_g, w_in, cmp_pe, cmp_w1, cmp_b1, cmp_w2, cmp_b2, lb_logits, hgrn_norm_g,
           w_up_nsa, w_up_hgrn, w_out, xa_norm_g, xa_mem_norm_g, w_xq, w_xkv, w_xo, moe_norm_g, router_w, router_b,
           moe_w1, moe_b1, moe_w2, moe_b2, out_norm_g):
    b, s, d = x.shape
    t = b * s
    g, hg, dh = NSA_KV_GROUPS, NSA_Q_PER_GROUP, NSA_HEAD_DIM
    x2 = x.reshape(t, d)
    row = lambda v: v.reshape(1, -1).astype(F32)

    splits = [0]
    for w in (d, d, NSA_Q_W) + (NSA_KV_W,) * 6 + (3 * NSA_HEADS,) + (HGRN_W,) * 4:
        splits.append(splits[-1] + w)
    seg = lambda i: w_in[:, splits[i]:splits[i + 1]]
    (ga, gb, nq, kc, vc, ks, vs, kw, vw, ng, hq, hf, hi, hgate) = [seg(i) for i in range(14)]
    pad = jnp.zeros((d, P_WIDTH - COL_NG - 3 * NSA_HEADS), w_in.dtype)
    w_p = jnp.concatenate([ga, gb, hq, hf, hi, hgate, nq, ks, kw, kc, vc, vs, vw, ng, pad], axis=1).astype(BF16)

    p = _norm_matmul(x2, row(mix_norm_g), w_p, F32, ts["tm_proj"], ts["tn_proj"], "in_proj")

    half = dh // 2
    inv_freq = ROPE_THETA ** (-jnp.arange(half, dtype=F32) / half)
    invf = jnp.tile(inv_freq, LANES // half).reshape(1, LANES)
    q_r, kk_r = _rope(p, positions.reshape(t, 1), invf, ts["tm_rope"])
    q5 = q_r.reshape(b, s, g, hg, dh).transpose(0, 2, 3, 1, 4)
    kk = kk_r.reshape(b, s, 2 * g, dh).transpose(0, 2, 1, 3)
    k_slc, k_win = kk[:, :g], kk[:, g:]
    vv = p[:, COL_VSVW:COL_VSVW + 2 * NSA_KV_W].astype(BF16).reshape(b, s, 2 * g, dh).transpose(0, 2, 1, 3)
    v_slc, v_win = vv[:, :g], vv[:, g:]
    gates = p[:, COL_NG:COL_NG + 3 * NSA_HEADS].reshape(b, s, g, hg, 3).transpose(0, 2, 3, 1, 4)

    nr = s // CMP_STRIDE
    kcvc = p[:, COL_KCVC:COL_KCVC + 2 * NSA_KV_W].reshape(b, s, 2, g, dh).transpose(2, 0, 3, 1, 4)
    r = kcvc.reshape(2, b, g, nr, CMP_STRIDE * dh)
    pe = cmp_pe.reshape(2, 2, 1, CMP_STRIDE * dh)
    zeros_w2 = jnp.zeros_like(cmp_w2)
    w2p = jnp.stack([jnp.concatenate([cmp_w2, zeros_w2], axis=-1),
                     jnp.concatenate([zeros_w2, cmp_w2], axis=-1)], axis=1).astype(BF16)
    b2t = jnp.tile(cmp_b2, (1, g)).reshape(2, 1, LANES)
    pos_cmp = positions[:, CMP_BLOCK - 1::CMP_STRIDE]
    pos_cmp = jnp.pad(pos_cmp, ((0, 0), (0, nr - pos_cmp.shape[1]))).reshape(b, nr, 1)
    cmp = _compress(r, pe, cmp_w1.astype(BF16), cmp_b1.reshape(2, 1, CMP_HIDDEN), w2p, b2t, pos_cmp, invf)
    cmp = cmp.reshape(2, b, nr, g, dh).transpose(0, 1, 3, 2, 4)

    y_nsa = _nsa(q5, cmp[0], cmp[1], k_slc, v_slc, k_win, v_win, gates, ts["tq"], ts["tk"])
    y_nsa = y_nsa.transpose(0, 3, 1, 2, 4).reshape(t, NSA_Q_W)

    y_hgrn = _hgrn(p, lb_logits.astype(F32), row(hgrn_norm_g), b, s, ts["hgrn_chunks"])

    x2 = _merge(x2, p, y_nsa, y_hgrn, w_up_nsa.astype(BF16), w_up_hgrn.astype(BF16), w_out.astype(BF16),
                ts["tm_merge"])

    n_mem = mem.shape[1]
    kv = _norm_matmul(mem.reshape(b * n_mem, d), row(xa_mem_norm_g), w_xkv.astype(BF16), BF16,
                      n_mem, 2 * XA_W, "mem_kv").reshape(b, n_mem, 2 * XA_W)
    x2 = _xattn(x2, row(xa_norm_g), w_xq.astype(BF16), kv, w_xo.astype(BF16), s, ts["tm_xattn"])

    n_exp = router_w.shape[1]
    rw = jnp.pad(router_w, ((0, 0), (0, LANES - n_exp))).astype(BF16)
    rb = jnp.pad(router_b, (0, LANES - n_exp)).reshape(1, LANES).astype(F32)
    hm, idx, rank, wt, cnt = _router(x2, row(moe_norm_g), rw, rb, ts["tm_router"])
    tm = ts["tm_ffn"]
    counts = cnt[0, :n_exp].astype(jnp.int32)
    padded = ((counts + tm - 1) // tm) * tm
    ends = jnp.cumsum(padded)
    offs = ends - padded
    dest = (offs[idx[:, :TOP_K]] + rank[:, :TOP_K]).reshape(t * TOP_K)
    n_pad = t * TOP_K + n_exp * tm
    n_tiles = n_pad // tm
    tile_ids = jnp.arange(n_tiles, dtype=jnp.int32)
    tile_expert = jnp.sum(((ends // tm)[None, :] <= tile_ids[:, None]).astype(jnp.int32), axis=1)
    tile_expert = jnp.minimum(tile_expert, n_exp - 1)
    n_used = (ends[-1] // tm).reshape(1).astype(jnp.int32)

    xs = _dispatch(dest, hm, jnp.zeros((n_pad, d), F32), ts["dispatch_tokens"] * TOP_K)
    f = moe_w2.shape[1]
    w1p = jnp.concatenate([moe_w1[:, :, 0::2], moe_w1[:, :, 1::2]], axis=-1).astype(BF16)
    b1p = jnp.concatenate([moe_b1[:, 0::2], moe_b1[:, 1::2]], axis=-1).reshape(n_exp, 1, 2 * f)
    ys = _ffn(tile_expert, n_used, xs, w1p, b1p, moe_w2.astype(BF16), moe_b2.reshape(n_exp, 1, d), tm)
    out = _combine(dest, ys, x2, wt, row(out_norm_g), ts["tm_combine"])
    return out.reshape(b, s, d)


def kernel(x, mem, positions, mix_norm_g, w_in, cmp_pe, cmp_w1, cmp_b1, cmp_w2, cmp_b2, hgrn_lb_logits, hgrn_norm_g, w_up_nsa, w_up_hgrn, w_out, xa_norm_g, xa_mem_norm_g, w_xq, w_xkv, w_xo, moe_norm_g, router_w, router_b, moe_w1, moe_b1, moe_w2, moe_b2, final_norm_g):
    depth = w_in.shape[0]
    assert depth == 1, "single-layer block: the final norm is fused into the last layer's combine"
    ts = _tile_sizes(x.shape[1])
    l = 0
    return _layer(x, mem, positions, ts, mix_norm_g[l], w_in[l], cmp_pe[l], cmp_w1[l], cmp_b1[l], cmp_w2[l], cmp_b2[l],
                  hgrn_lb_logits, hgrn_norm_g[l], w_up_nsa[l], w_up_hgrn[l], w_out[l], xa_norm_g[l], xa_mem_norm_g[l],
                  w_xq[l], w_xkv[l], w_xo[l], moe_norm_g[l], router_w[l], router_b[l], moe_w1[l], moe_b1[l], moe_w2[l],
                  moe_b2[l], final_norm_g)
```

```python
import functools

import jax
import jax.numpy as jnp
from jax import lax
from jax.experimental import pallas as pl
from jax.experimental.pallas import tpu as pltpu

EPS = 1e-6
ROPE_THETA = 10000.0
NEG_INF = -1e30
FORCE_SCORE = 1e9

NSA_HEADS = 8
NSA_KV_GROUPS = 2
NSA_Q_PER_GROUP = NSA_HEADS // NSA_KV_GROUPS
NSA_HEAD_DIM = 64
CMP_BLOCK = 32
CMP_STRIDE = 16
CMP_HIDDEN = 256
SLC_BLOCK = 64
SLC_TOPK = 16
N_LOCAL_BLOCKS = 2
WINDOW = 512
NSA_Q_W = NSA_HEADS * NSA_HEAD_DIM
NSA_KV_W = NSA_KV_GROUPS * NSA_HEAD_DIM

HGRN_HEADS = 4
HGRN_DK = 128
HGRN_DV = 128
HGRN_CHUNK = 64
HGRN_SUB = 16
HGRN_W = HGRN_HEADS * HGRN_DK

XA_HEADS = 4
XA_HEAD_DIM = 128
XA_W = XA_HEADS * XA_HEAD_DIM

N_EXPERTS = 32
TOP_K = 4
SWIGLU_ALPHA = 1.702
SWIGLU_LIMIT = 7.0

LANES = 128
SEG_ALIGN = 8
VMEM_LIMIT = 48 * 1024 * 1024

COL_GA = 0
COL_GB = 1024
COL_HQ = 2048
COL_HF = 2560
COL_HI = 3072
COL_HG = 3584
COL_NQ = 4096
COL_KSKW = 4608
COL_KCVC = 4864
COL_VSVW = 5120
COL_NG = 5376
P_WIDTH = 5632

F32 = jnp.float32
BF16 = jnp.bfloat16


def _params(sem):
    return pltpu.CompilerParams(dimension_semantics=sem, vmem_limit_bytes=VMEM_LIMIT)


def _dot(a, b):
    return jnp.dot(a, b, preferred_element_type=F32)


def _dot_nt(a, b):
    return lax.dot_general(a, b, (((1,), (1,)), ((), ())), preferred_element_type=F32)


def _rms(xf, g):
    return xf * lax.rsqrt(jnp.mean(xf * xf, axis=-1, keepdims=True) + EPS) * g


def _sigmoid(x):
    return 1.0 / (1.0 + jnp.exp(-x))


def _norm_matmul_kernel(x_ref, g_ref, w_ref, o_ref, hn_ref):
    @pl.when(pl.program_id(1) == 0)
    def _():
        hn_ref[...] = _rms(x_ref[...], g_ref[...]).astype(BF16)

    o_ref[...] = _dot(hn_ref[...], w_ref[...]).astype(o_ref.dtype)


def _norm_matmul(x, g, w, out_dtype, tm, tn, name):
    t, d = x.shape
    n = w.shape[1]
    return pl.pallas_call(
        _norm_matmul_kernel,
        out_shape=jax.ShapeDtypeStruct((t, n), out_dtype),
        grid=(t // tm, n // tn),
        in_specs=[
            pl.BlockSpec((tm, d), lambda i, j: (i, 0)),
            pl.BlockSpec((1, d), lambda i, j: (0, 0)),
            pl.BlockSpec((d, tn), lambda i, j: (0, j)),
        ],
        out_specs=pl.BlockSpec((tm, tn), lambda i, j: (i, j)),
        scratch_shapes=[pltpu.VMEM((tm, d), BF16)],
        compiler_params=_params(("arbitrary", "arbitrary")),
        name=name,
    )(x, g, w)


def _rope_coeffs(pos_col, invf):
    ang = pos_col.astype(F32) * invf
    lane = lax.broadcasted_iota(jnp.int32, ang.shape, 1)
    first = (lane & (NSA_HEAD_DIM - 1)) < (NSA_HEAD_DIM // 2)
    c = jnp.cos(ang)
    s = jnp.sin(ang)
    return c, jnp.where(first, -s, s), first


def _rope_tile(x, c, s_signed, first):
    half = NSA_HEAD_DIM // 2
    partner = jnp.where(first, pltpu.roll(x, LANES - half, 1), pltpu.roll(x, half, 1))
    return x * c + partner * s_signed


def _rope_kernel(q_ref, k_ref, pos_ref, invf_ref, qo_ref, ko_ref, *, q_scale):
    c, s_signed, first = _rope_coeffs(pos_ref[...], invf_ref[...])
    for i in range(q_ref.shape[1] // LANES):
        sl = slice(i * LANES, (i + 1) * LANES)
        qo_ref[:, sl] = (_rope_tile(q_ref[:, sl], c, s_signed, first) * q_scale).astype(BF16)
    for i in range(k_ref.shape[1] // LANES):
        sl = slice(i * LANES, (i + 1) * LANES)
        ko_ref[:, sl] = _rope_tile(k_ref[:, sl], c, s_signed, first).astype(BF16)


def _rope(p, pos_col, invf, tm):
    t = p.shape[0]
    kw = 2 * NSA_KV_W
    return pl.pallas_call(
        functools.partial(_rope_kernel, q_scale=NSA_HEAD_DIM ** -0.5),
        out_shape=(jax.ShapeDtypeStruct((t, NSA_Q_W), BF16), jax.ShapeDtypeStruct((t, kw), BF16)),
        grid=(t // tm,),
        in_specs=[
            pl.BlockSpec((tm, NSA_Q_W), lambda i: (i, COL_NQ // NSA_Q_W)),
            pl.BlockSpec((tm, kw), lambda i: (i, COL_KSKW // kw)),
            pl.BlockSpec((tm, 1), lambda i: (i, 0)),
            pl.BlockSpec((1, LANES), lambda i: (0, 0)),
        ],
        out_specs=(
            pl.BlockSpec((tm, NSA_Q_W), lambda i: (i, 0)),
            pl.BlockSpec((tm, kw), lambda i: (i, 0)),
        ),
        compiler_params=_params(("arbitrary",)),
        name="rope",
    )(p, p, pos_col, invf)


def _gelu_tanh(x):
    return 0.5 * x * (1.0 + jnp.tanh(0.7978845608028654 * (x + 0.044715 * (x * x * x))))


def _compress_kernel(r_ref, pe_ref, w1_ref, b1_ref, w2_ref, b2_ref, pos_ref, invf_ref, o_ref):
    nr = r_ref.shape[3]
    half = r_ref.shape[4]
    acc = None
    for g in range(NSA_KV_GROUPS):
        r = r_ref[0, 0, g]
        top = _dot((r + pe_ref[0, 0]).astype(BF16), w1_ref[0, :half, :])
        bot = _dot((r + pe_ref[0, 1]).astype(BF16), w1_ref[0, half:, :])
        pre = top + pltpu.roll(bot, nr - 1, 0) + b1_ref[0]
        part = _dot(_gelu_tanh(pre).astype(BF16), w2_ref[0, g])
        acc = part if acc is None else acc + part
    out = acc + b2_ref[0]
    c, s_signed, first = _rope_coeffs(pos_ref[0], invf_ref[...])
    roped = _rope_tile(out, c, s_signed, first)
    is_key = pl.program_id(0) == 0
    o_ref[0, 0] = jnp.where(is_key, roped, out).astype(BF16)


def _compress(r, pe, w1, b1, w2p, b2t, pos_cmp, invf):
    _, b, g, nr, half = r.shape
    return pl.pallas_call(
        _compress_kernel,
        out_shape=jax.ShapeDtypeStruct((2, b, nr, LANES), BF16),
        grid=(2, b),
        in_specs=[
            pl.BlockSpec((1, 1, g, nr, half), lambda k, i: (k, i, 0, 0, 0)),
            pl.BlockSpec((1, 2, 1, half), lambda k, i: (k, 0, 0, 0)),
            pl.BlockSpec((1, 2 * half, CMP_HIDDEN), lambda k, i: (k, 0, 0)),
            pl.BlockSpec((1, 1, CMP_HIDDEN), lambda k, i: (k, 0, 0)),
            pl.BlockSpec((1, g, CMP_HIDDEN, LANES), lambda k, i: (k, 0, 0, 0)),
            pl.BlockSpec((1, 1, LANES), lambda k, i: (k, 0, 0)),
            pl.BlockSpec((1, nr, 1), lambda k, i: (i, 0, 0)),
            pl.BlockSpec((1, LANES), lambda k, i: (0, 0)),
        ],
        out_specs=pl.BlockSpec((1, 1, nr, LANES), lambda k, i: (k, i, 0, 0)),
        compiler_params=_params(("arbitrary", "arbitrary")),
        name="compress",
    )(r, pe, w1, b1, w2p, b2t, pos_cmp, invf)


def _softmax_step(s, valid, v, m, l, acc):
    sm = jnp.where(valid, s, NEG_INF)
    m_new = jnp.maximum(m, jnp.max(sm, axis=-1, keepdims=True))
    alpha = jnp.exp(m - m_new)
    p = jnp.where(valid, jnp.exp(sm - m_new), 0.0)
    l_new = alpha * l + jnp.sum(p, axis=-1, keepdims=True)
    acc_new = alpha * acc + _dot(p.astype(BF16), v)
    return m_new, l_new, acc_new


def _nsa_kernel(q_ref, kc_ref, vc_ref, ks_ref, vs_ref, kw_ref, vw_ref, g_ref, o_ref, *, tq, tk, seq):
    hg = NSA_Q_PER_GROUP
    rows = hg * tq
    n_cmp_rows = kc_ref.shape[2]
    nb = seq // SLC_BLOCK
    top_k = min(SLC_TOPK, nb)
    win_keys = WINDOW + tq
    s0 = pl.program_id(2) * tq

    q = q_ref[0, 0].reshape(rows, NSA_HEAD_DIM)
    row = lax.broadcasted_iota(jnp.int32, (rows, 1), 0)
    t_row = s0 + (row & (tq - 1))
    t_tok = s0 + lax.broadcasted_iota(jnp.int32, (tq, 1), 0)

    sc = _dot_nt(q, kc_ref[0, 0])
    n_idx = lax.broadcasted_iota(jnp.int32, (1, n_cmp_rows), 1)
    valid_c = (n_idx * CMP_STRIDE + (CMP_BLOCK - 1) <= t_row) & (n_idx < n_cmp_rows - 1)
    scm = jnp.where(valid_c, sc, NEG_INF)
    e = jnp.where(valid_c, jnp.exp(scm - jnp.max(scm, axis=-1, keepdims=True)), 0.0)
    den = jnp.sum(e, axis=-1, keepdims=True)
    p_c = e / jnp.where(den > 0.0, den, 1.0)
    o_c = _dot(p_c.astype(BF16), vc_ref[0, 0])

    p_sum = p_c[0:tq]
    for h in range(1, hg):
        p_sum = p_sum + p_c[h * tq:(h + 1) * tq]
    n_col = lax.broadcasted_iota(jnp.int32, (n_cmp_rows, 1), 0) * CMP_STRIDE
    j_row = lax.broadcasted_iota(jnp.int32, (1, nb), 1)
    overlap = ((n_col < j_row * SLC_BLOCK + SLC_BLOCK) & (n_col + CMP_BLOCK > j_row * SLC_BLOCK))
    overlap = jnp.where(overlap, 1.0, 0.0).astype(BF16)
    p_hi = p_sum.astype(BF16)
    p_lo = (p_sum - p_hi.astype(F32)).astype(BF16)
    imp = _dot(p_hi, overlap) + _dot(p_lo, overlap)

    cur = t_tok >> 6
    causal_b = j_row <= cur
    forced = (j_row == 0) | (causal_b & (j_row > cur - N_LOCAL_BLOCKS))
    score = jnp.where(forced, FORCE_SCORE, jnp.where(causal_b, imp, -1.0))
    j_f = j_row.astype(F32)
    sel = jnp.zeros((tq, nb), F32)
    for _ in range(top_k):
        mx = jnp.max(score, axis=-1, keepdims=True)
        first_idx = jnp.min(jnp.where(score == mx, j_f, float(nb)), axis=-1, keepdims=True)
        hit = j_f == first_idx
        sel = jnp.where(hit, 1.0, sel)
        score = jnp.where(hit, -jnp.inf, score)
    sel = jnp.where(causal_b, sel, 0.0).astype(BF16)

    jb_col = lax.broadcasted_iota(jnp.int32, (nb, 1), 0)
    c_row = lax.broadcasted_iota(jnp.int32, (1, tk), 1)

    def slc_body(kt, carry):
        k0 = pl.multiple_of(kt * tk, tk)
        kpos = k0 + c_row
        expand = jnp.where(jb_col == (kpos >> 6), 1.0, 0.0).astype(BF16)
        blk = _dot(sel, expand)
        ok = jnp.where((blk > 0.5) & (kpos <= t_tok), 1.0, 0.0)
        valid = jnp.concatenate([ok] * hg, axis=0) > 0.5
        s = _dot_nt(q, ks_ref[0, 0, pl.ds(k0, tk), :])
        return _softmax_step(s, valid, vs_ref[0, 0, pl.ds(k0, tk), :], *carry)

    init = (jnp.full((rows, 1), NEG_INF, F32), jnp.zeros((rows, 1), F32), jnp.zeros((rows, NSA_HEAD_DIM), F32))
    n_kt = (s0 + tq + tk - 1) // tk
    _, l_s, acc_s = lax.fori_loop(0, n_kt, slc_body, init)

    w0 = pl.multiple_of(jnp.maximum(s0 - WINDOW, 0), tq)
    wpos = w0 + lax.broadcasted_iota(jnp.int32, (1, win_keys), 1)
    valid_w = (wpos <= t_row) & (wpos > t_row - WINDOW)
    sw = _dot_nt(q, kw_ref[0, 0, pl.ds(w0, win_keys), :])
    _, l_w, acc_w = _softmax_step(sw, valid_w, vw_ref[0, 0, pl.ds(w0, win_keys), :], *init)

    gate = _sigmoid(g_ref[0, 0].reshape(rows, 3))
    out = gate[:, 0:1] * o_c + gate[:, 1:2] * (acc_s / l_s) + gate[:, 2:3] * (acc_w / l_w)
    o_ref[0, 0] = out.reshape(hg, tq, NSA_HEAD_DIM).astype(o_ref.dtype)


def _nsa(q5, kc, vc, ks, vs, kw, vw, gates, tq, tk):
    b, g, hg, s, dh = q5.shape
    nr = kc.shape[2]
    kv_spec = pl.BlockSpec((1, 1, s, dh), lambda i, j, k: (i, j, 0, 0))
    cmp_spec = pl.BlockSpec((1, 1, nr, dh), lambda i, j, k: (i, j, 0, 0))
    return pl.pallas_call(
        functools.partial(_nsa_kernel, tq=tq, tk=tk, seq=s),
        out_shape=jax.ShapeDtypeStruct((b, g, hg, s, dh), BF16),
        grid=(b, g, s // tq),
        in_specs=[
            pl.BlockSpec((1, 1, hg, tq, dh), lambda i, j, k: (i, j, 0, k, 0)),
            cmp_spec, cmp_spec, kv_spec, kv_spec, kv_spec, kv_spec,
            pl.BlockSpec((1, 1, hg, tq, 3), lambda i, j, k: (i, j, 0, k, 0)),
        ],
        out_specs=pl.BlockSpec((1, 1, hg, tq, dh), lambda i, j, k: (i, j, 0, k, 0)),
        compiler_params=_params(("arbitrary", "arbitrary", "arbitrary")),
        name="nsa",
    )(q5, kc, vc, ks, vs, kw, vw, gates)


def _cumsum_rows(x):
    n = x.shape[0]
    row = lax.broadcasted_iota(jnp.int32, x.shape, 0)
    d = 1
    while d < n:
        x = x + jnp.where(row >= d, pltpu.roll(x, d, 0), 0.0)
        d *= 2
    return x


def _hgrn_kernel(q_ref, f_ref, i_ref, g_ref, lb_ref, gn_ref, o_ref, st_ref, *, n_chunks):
    @pl.when(pl.program_id(1) == 0)
    def _():
        st_ref[...] = jnp.zeros_like(st_ref)

    c_len = HGRN_CHUNK
    sub = HGRN_SUB
    lbl = lb_ref[...]
    lb_e = jnp.exp(lbl - jnp.max(lbl, axis=0, keepdims=True))
    lb_all = lb_e[0:1] / jnp.sum(lb_e, axis=0, keepdims=True)

    for c in range(n_chunks):
        rs = slice(c * c_len, (c + 1) * c_len)
        for h in range(HGRN_HEADS):
            ls = slice(h * HGRN_DK, (h + 1) * HGRN_DK)
            lb = lb_all[:, ls]
            f = lb + (1.0 - lb) * _sigmoid(f_ref[rs, ls])
            k = 1.0 - f
            b = _cumsum_rows(jnp.log(f))
            q = q_ref[rs, ls]
            v32 = i_ref[rs, ls]
            v = v32.astype(BF16)
            st = st_ref[h]
            o_inter = _dot_nt((q * jnp.exp(b)).astype(BF16), st.astype(BF16))
            b_end = b[c_len - 1:c_len]
            pieces = []
            for blk in range(c_len // sub):
                lo, hi = blk * sub, (blk + 1) * sub
                beta = b[lo - 1:lo] if blk > 0 else jnp.zeros_like(b_end)
                qd = (q[lo:hi] * jnp.exp(b[lo:hi] - beta)).astype(BF16)
                kd = (k[:hi] * jnp.exp(beta - b[:hi])).astype(BF16)
                a = _dot_nt(qd, kd)
                ti = lax.broadcasted_iota(jnp.int32, (sub, hi), 0)
                si = lax.broadcasted_iota(jnp.int32, (sub, hi), 1)
                a = jnp.where(si <= ti + lo, a, 0.0)
                pieces.append(_dot(a.astype(BF16), v[:hi]))
            o = o_inter + jnp.concatenate(pieces, axis=0)
            kd_end = (k * jnp.exp(b_end - b)).astype(BF16)
            st_ref[h] = st * jnp.exp(b_end) + _dot(v32.T.astype(BF16), kd_end)
            y = _rms(o, gn_ref[...])
            gate = g_ref[rs, ls]
            o_ref[rs, ls] = (y * (gate * _sigmoid(gate))).astype(o_ref.dtype)


def _hgrn(p, lb_logits, gn, batch, seq, n_chunks):
    t = p.shape[0]
    rows = n_chunks * HGRN_CHUNK
    steps = seq // rows

    def col(cb):
        return pl.BlockSpec((rows, HGRN_W), lambda i, j: (i * steps + j, cb))

    return pl.pallas_call(
        functools.partial(_hgrn_kernel, n_chunks=n_chunks),
        out_shape=jax.ShapeDtypeStruct((t, HGRN_W), BF16),
        grid=(batch, steps),
        in_specs=[
            col(COL_HQ // HGRN_W), col(COL_HF // HGRN_W), col(COL_HI // HGRN_W), col(COL_HG // HGRN_W),
            pl.BlockSpec(lb_logits.shape, lambda i, j: (0, 0)),
            pl.BlockSpec((1, HGRN_DV), lambda i, j: (0, 0)),
        ],
        out_specs=pl.BlockSpec((rows, HGRN_W), lambda i, j: (i * steps + j, 0)),
        scratch_shapes=[pltpu.VMEM((HGRN_HEADS, HGRN_DV, HGRN_DK), F32)],
        compiler_params=_params(("arbitrary", "arbitrary")),
        name="hgrn",
    )(p, p, p, p, lb_logits, gn)


def _merge_kernel(x_ref, ga_ref, gb_ref, yn_ref, yh_ref, wn_ref, wh_ref, wo_ref, o_ref):
    mixed = _sigmoid(ga_ref[...]) * _dot(yn_ref[...], wn_ref[...]) + _sigmoid(gb_ref[...]) * _dot(yh_ref[...], wh_ref[...])
    o_ref[...] = x_ref[...] + _dot(mixed.astype(BF16), wo_ref[...])


def _merge(x, p, y_nsa, y_hgrn, wn, wh, wo, tm):
    t, d = x.shape
    full = lambda a: pl.BlockSpec(a.shape, lambda i: (0, 0))
    return pl.pallas_call(
        _merge_kernel,
        out_shape=jax.ShapeDtypeStruct((t, d), F32),
        grid=(t // tm,),
        in_specs=[
            pl.BlockSpec((tm, d), lambda i: (i, 0)),
            pl.BlockSpec((tm, d), lambda i: (i, COL_GA // d)),
            pl.BlockSpec((tm, d), lambda i: (i, COL_GB // d)),
            pl.BlockSpec((tm, NSA_Q_W), lambda i: (i, 0)),
            pl.BlockSpec((tm, HGRN_W), lambda i: (i, 0)),
            full(wn), full(wh), full(wo),
        ],
        out_specs=pl.BlockSpec((tm, d), lambda i: (i, 0)),
        compiler_params=_params(("arbitrary",)),
        name="merge",
    )(x, p, p, y_nsa, y_hgrn, wn, wh, wo)


def _xattn_kernel(x_ref, g_ref, wq_ref, kv_ref, wo_ref, o_ref):
    x = x_ref[...]
    xq = _dot(_rms(x, g_ref[...]).astype(BF16), wq_ref[...]).astype(BF16)
    outs = []
    for h in range(XA_HEADS):
        ls = slice(h * XA_HEAD_DIM, (h + 1) * XA_HEAD_DIM)
        s = _dot_nt(xq[:, ls], kv_ref[0, :, ls]) * (XA_HEAD_DIM ** -0.5)
        e = jnp.exp(s - jnp.max(s, axis=-1, keepdims=True))
        p = e / jnp.sum(e, axis=-1, keepdims=True)
        outs.append(_dot(p.astype(BF16), kv_ref[0, :, XA_W + h * XA_HEAD_DIM:XA_W + (h + 1) * XA_HEAD_DIM]))
    o_x = jnp.concatenate(outs, axis=-1)
    o_ref[...] = x + _dot(o_x.astype(BF16), wo_ref[...])


def _xattn(x, g, wq, kv, wo, seq, tm):
    t, d = x.shape
    steps = seq // tm
    full = lambda a: pl.BlockSpec(a.shape, lambda i: (0, 0))
    return pl.pallas_call(
        _xattn_kernel,
        out_shape=jax.ShapeDtypeStruct((t, d), F32),
        grid=(t // tm,),
        in_specs=[
            pl.BlockSpec((tm, d), lambda i: (i, 0)),
            full(g), full(wq),
            pl.BlockSpec((1,) + kv.shape[1:], lambda i: (i // steps, 0, 0)),
            full(wo),
        ],
        out_specs=pl.BlockSpec((tm, d), lambda i: (i, 0)),
        compiler_params=_params(("arbitrary",)),
        name="xattn",
    )(x, g, wq, kv, wo)


def _router_kernel(x_ref, g_ref, w_ref, b_ref, hm_ref, idx_ref, rank_ref, wt_ref, cnt_ref):
    tm = x_ref.shape[0]
    hm = _rms(x_ref[...], g_ref[...]).astype(BF16)
    hm_ref[...] = hm
    lane = lax.broadcasted_iota(jnp.int32, (tm, LANES), 1)
    lane_f = lane.astype(F32)
    logits = _dot(hm, w_ref[...]) + b_ref[...]
    logits = jnp.where(lane < N_EXPERTS, logits, -jnp.inf)
    picks, vals = [], []
    onehot_all = jnp.zeros((tm, LANES), F32)
    for _ in range(TOP_K):
        mx = jnp.max(logits, axis=-1, keepdims=True)
        first_idx = jnp.min(jnp.where(logits == mx, lane_f, float(LANES)), axis=-1, keepdims=True)
        hit = lane_f == first_idx
        onehot = jnp.where(hit, 1.0, 0.0)
        logits = jnp.where(hit, -jnp.inf, logits)
        picks.append((first_idx, onehot))
        vals.append(mx)
        onehot_all = onehot_all + onehot
    exps = [jnp.exp(v - vals[0]) for v in vals]
    den = exps[0]
    for e in exps[1:]:
        den = den + e
    r_i = lax.broadcasted_iota(jnp.int32, (tm, tm), 0)
    c_i = lax.broadcasted_iota(jnp.int32, (tm, tm), 1)
    lower = jnp.where(c_i < r_i, 1.0, 0.0).astype(BF16)
    before = _dot(lower, onehot_all.astype(BF16))
    idx_out = jnp.zeros((tm, LANES), F32)
    rank_out = jnp.zeros((tm, LANES), F32)
    wt_out = jnp.zeros((tm, LANES), F32)
    for k in range(TOP_K):
        first_idx, onehot = picks[k]
        rank = jnp.sum(onehot * before, axis=-1, keepdims=True)
        idx_out = jnp.where(lane == k, first_idx, idx_out)
        rank_out = jnp.where(lane == k, rank, rank_out)
        wt_out = jnp.where(lane == k, exps[k] / den, wt_out)
    idx_ref[...] = idx_out.astype(jnp.int32)
    rank_ref[...] = rank_out.astype(jnp.int32)
    wt_ref[...] = wt_out
    cnt_ref[0] = jnp.sum(onehot_all, axis=0, keepdims=True)


def _router(x, g, w, b, tm):
    t, d = x.shape
    full = lambda a: pl.BlockSpec(a.shape, lambda i: (0, 0))
    lane_out = pl.BlockSpec((tm, LANES), lambda i: (i, 0))
    return pl.pallas_call(
        _router_kernel,
        out_shape=(
            jax.ShapeDtypeStruct((t, d), BF16),
            jax.ShapeDtypeStruct((t, LANES), jnp.int32),
            jax.ShapeDtypeStruct((t, LANES), jnp.int32),
            jax.ShapeDtypeStruct((t, LANES), F32),
            jax.ShapeDtypeStruct((t // tm, 1, LANES), F32),
        ),
        grid=(t // tm,),
        in_specs=[pl.BlockSpec((tm, d), lambda i: (i, 0)), full(g), full(w), full(b)],
        out_specs=(pl.BlockSpec((tm, d), lambda i: (i, 0)), lane_out, lane_out, lane_out,
                   pl.BlockSpec((1, 1, LANES), lambda i: (i, 0, 0))),
        compiler_params=_params(("arbitrary",)),
        name="router",
    )(x, g, w, b)


def _slot_matrix(idx_ref, rank_ref, offv_ref, rows, values=None):
    tt = idx_ref.shape[0]
    lane = lax.broadcasted_iota(jnp.int32, (tt, LANES), 1)
    r = lax.broadcasted_iota(jnp.int32, (tt, rows), 1)
    offv = offv_ref[0]
    idx = idx_ref[...]
    rank = rank_ref[...]
    out = jnp.zeros((tt, rows), F32)
    for k in range(TOP_K):
        seg = jnp.sum(jnp.where(lane == idx[:, k:k + 1], offv, 0.0), axis=-1, keepdims=True)
        row_k = seg.astype(jnp.int32) + rank[:, k:k + 1]
        out = jnp.where(r == row_k, 1.0 if values is None else values[:, k:k + 1], out)
    return out


def _segment_copies(src_ref, n8_ref, dst_ref, make_copy, tile_tokens):
    base = pl.program_id(0) * N_EXPERTS
    sizes = []
    size = tile_tokens
    while size >= SEG_ALIGN:
        sizes.append(size)
        size //= 2

    def visit(e, start):
        n8 = n8_ref[base + e]
        src = src_ref[base + e]
        dst = dst_ref[base + e]
        for size in sizes:
            done = n8 & (-2 * size)

            @pl.when((n8 & size) != 0)
            def _():
                cp = make_copy(pl.multiple_of(src + done, SEG_ALIGN), pl.multiple_of(dst + done, SEG_ALIGN), size)
                if start:
                    cp.start()
                else:
                    cp.wait()

    def start_all(e, c):
        visit(e, True)
        return c

    def wait_all(e, c):
        visit(e, False)
        return c

    return start_all, wait_all


def _dispatch_kernel(src_ref, n8_ref, dst_ref, hm_ref, idx_ref, rank_ref, offv_ref, xs_in_ref, xs_ref, buf_ref, sem):
    del xs_in_ref
    rows = buf_ref.shape[0]
    onehot = _slot_matrix(idx_ref, rank_ref, offv_ref, rows).astype(BF16)
    buf_ref[...] = lax.dot_general(onehot, hm_ref[...], (((0,), (0,)), ((), ())), preferred_element_type=F32)

    def make_copy(src, dst, size):
        return pltpu.make_async_copy(buf_ref.at[pl.ds(src, size)], xs_ref.at[pl.ds(dst, size)], sem)

    start_all, wait_all = _segment_copies(src_ref, n8_ref, dst_ref, make_copy, hm_ref.shape[0])
    lax.fori_loop(0, N_EXPERTS, start_all, 0)
    lax.fori_loop(0, N_EXPERTS, wait_all, 0)


def _dispatch(tables, hm, idx, rank, offv, xs_zero, tt, rows):
    t, d = hm.shape
    tile = lambda w: pl.BlockSpec((tt, w), lambda i, *_: (i, 0))
    grid_spec = pltpu.PrefetchScalarGridSpec(
        num_scalar_prefetch=3,
        grid=(t // tt,),
        in_specs=[tile(d), tile(LANES), tile(LANES), pl.BlockSpec((1, 1, LANES), lambda i, *_: (i, 0, 0)),
                  pl.BlockSpec(memory_space=pl.ANY)],
        out_specs=pl.BlockSpec(memory_space=pl.ANY),
        scratch_shapes=[pltpu.VMEM((rows, d), F32), pltpu.SemaphoreType.DMA],
    )
    return pl.pallas_call(
        _dispatch_kernel,
        out_shape=jax.ShapeDtypeStruct(xs_zero.shape, xs_zero.dtype),
        grid_spec=grid_spec,
        input_output_aliases={7: 0},
        compiler_params=_params(("arbitrary",)),
        name="dispatch",
    )(*tables, hm, idx, rank, offv, xs_zero)


def _w1_prep_kernel(w_ref, o_ref):
    grp = 2 * LANES
    r_i = lax.broadcasted_iota(jnp.int32, (grp, grp), 0)
    c_i = lax.broadcasted_iota(jnp.int32, (grp, grp), 1)
    src_col = jnp.where(c_i < LANES, 2 * c_i, 2 * (c_i - LANES) + 1)
    perm = jnp.where(r_i == src_col, 1.0, 0.0).astype(BF16)
    for c in range(w_ref.shape[1] // grp):
        sl = slice(c * grp, (c + 1) * grp)
        o_ref[:, sl] = _dot(w_ref[:, sl].astype(BF16), perm).astype(BF16)


def _w1_prep(w, tm):
    r, n = w.shape
    return pl.pallas_call(
        _w1_prep_kernel,
        out_shape=jax.ShapeDtypeStruct((r, n), BF16),
        grid=(r // tm,),
        in_specs=[pl.BlockSpec((tm, n), lambda i: (i, 0))],
        out_specs=pl.BlockSpec((tm, n), lambda i: (i, 0)),
        compiler_params=_params(("arbitrary",)),
        name="w1_prep",
    )(w)


def _ffn_kernel(te_ref, nu_ref, x_ref, w1_ref, b1_ref, w2_ref, b2_ref, o_ref):
    del te_ref
    used = pl.program_id(0) < nu_ref[0]

    @pl.when(used)
    def _():
        u = _dot(x_ref[...].astype(BF16), w1_ref[0]) + b1_ref[0]
        acts = []
        for c in range(u.shape[1] // (2 * LANES)):
            glu = jnp.minimum(u[:, 2 * c * LANES:(2 * c + 1) * LANES], SWIGLU_LIMIT)
            lin = jnp.clip(u[:, (2 * c + 1) * LANES:(2 * c + 2) * LANES], -SWIGLU_LIMIT, SWIGLU_LIMIT)
            acts.append((glu * _sigmoid(SWIGLU_ALPHA * glu) * (lin + 1.0)).astype(BF16))
        o_ref[...] = _dot(jnp.concatenate(acts, axis=-1), w2_ref[0]) + b2_ref[0]

    @pl.when(jnp.logical_not(used))
    def _():
        o_ref[...] = jnp.zeros_like(o_ref)


def _ffn(tile_expert, n_used, xs, w1, b1, w2, b2, tm):
    n_pad, d = xs.shape
    f2 = w1.shape[2]
    f = w2.shape[1]
    grid_spec = pltpu.PrefetchScalarGridSpec(
        num_scalar_prefetch=2,
        grid=(n_pad // tm,),
        in_specs=[
            pl.BlockSpec((tm, d), lambda i, te, nu: (i, 0)),
            pl.BlockSpec((1, d, f2), lambda i, te, nu: (te[i], 0, 0)),
            pl.BlockSpec((1, 1, f2), lambda i, te, nu: (te[i], 0, 0)),
            pl.BlockSpec((1, f, d), lambda i, te, nu: (te[i], 0, 0)),
            pl.BlockSpec((1, 1, d), lambda i, te, nu: (te[i], 0, 0)),
        ],
        out_specs=pl.BlockSpec((tm, d), lambda i, te, nu: (i, 0)),
    )
    return pl.pallas_call(
        _ffn_kernel,
        out_shape=jax.ShapeDtypeStruct((n_pad, d), F32),
        grid_spec=grid_spec,
        compiler_params=_params(("arbitrary",)),
        name="expert_ffn",
    )(tile_expert, n_used, xs, w1, b1, w2, b2)


def _combine_kernel(src_ref, n8_ref, dst_ref, ys_ref, idx_ref, rank_ref, wt_ref, offv_ref, x_ref, g_ref, o_ref,
                    buf_ref, sem):
    rows = buf_ref.shape[0]

    @pl.when(pl.program_id(0) == 0)
    def _():
        buf_ref[...] = jnp.zeros_like(buf_ref)

    def make_copy(src, dst, size):
        return pltpu.make_async_copy(ys_ref.at[pl.ds(dst, size)], buf_ref.at[pl.ds(src, size)], sem)

    start_all, wait_all = _segment_copies(src_ref, n8_ref, dst_ref, make_copy, x_ref.shape[0])
    lax.fori_loop(0, N_EXPERTS, start_all, 0)
    pw = _slot_matrix(idx_ref, rank_ref, offv_ref, rows, values=wt_ref[...])
    p_hi = pw.astype(BF16)
    p_lo = (pw - p_hi.astype(F32)).astype(BF16)
    lax.fori_loop(0, N_EXPERTS, wait_all, 0)
    ys = buf_ref[...]
    y_hi = ys.astype(BF16)
    y_lo = (ys - y_hi.astype(F32)).astype(BF16)
    y = _dot(p_hi, y_hi) + _dot(p_hi, y_lo) + _dot(p_lo, y_hi)
    o_ref[...] = _rms(x_ref[...] + y, g_ref[...])


def _combine(tables, ys, idx, rank, wt, offv, x, g, tt, rows):
    t, d = x.shape
    tile = lambda w: pl.BlockSpec((tt, w), lambda i, *_: (i, 0))
    grid_spec = pltpu.PrefetchScalarGridSpec(
        num_scalar_prefetch=3,
        grid=(t // tt,),
        in_specs=[pl.BlockSpec(memory_space=pl.ANY), tile(LANES), tile(LANES), tile(LANES),
                  pl.BlockSpec((1, 1, LANES), lambda i, *_: (i, 0, 0)), tile(d),
                  pl.BlockSpec((1, d), lambda i, *_: (0, 0))],
        out_specs=tile(d),
        scratch_shapes=[pltpu.VMEM((rows, d), F32), pltpu.SemaphoreType.DMA],
    )
    return pl.pallas_call(
        _combine_kernel,
        out_shape=jax.ShapeDtypeStruct((t, d), F32),
        grid_spec=grid_spec,
        compiler_params=_params(("arbitrary",)),
        name="combine",
    )(*tables, ys, idx, rank, wt, offv, x, g)


def _tile_sizes(seq):
    return dict(
        tm_proj=512, tn_proj=512,
        tm_rope=512,
        tq=128, tk=min(512, seq),
        hgrn_chunks=4,
        tm_merge=512, tm_xattn=512,
        tm_router=512,
        tm_w1_prep=512,
        tm_ffn=512,
    )


def _layer(x, mem, positions, ts, mix_norm_g, w_in, cmp_pe, cmp_w1, cmp_b1, cmp_w2, cmp_b2, lb_logits, hgrn_norm_g,
           w_up_nsa, w_up_hgrn, w_out, xa_norm_g, xa_mem_norm_g, w_xq, w_xkv, w_xo, moe_norm_g, router_w, router_b,
           moe_w1, moe_b1, moe_w2, moe_b2, out_norm_g):
    b, s, d = x.shape
    t = b * s
    g, hg, dh = NSA_KV_GROUPS, NSA_Q_PER_GROUP, NSA_HEAD_DIM
    x2 = x.reshape(t, d)
    row = lambda v: v.reshape(1, -1).astype(F32)

    splits = [0]
    for w in (d, d, NSA_Q_W) + (NSA_KV_W,) * 6 + (3 * NSA_HEADS,) + (HGRN_W,) * 4:
        splits.append(splits[-1] + w)
    seg = lambda i: w_in[:, splits[i]:splits[i + 1]]
    (ga, gb, nq, kc, vc, ks, vs, kw, vw, ng, hq, hf, hi, hgate) = [seg(i) for i in range(14)]
    pad = jnp.zeros((d, P_WIDTH - COL_NG - 3 * NSA_HEADS), w_in.dtype)
    w_p = jnp.concatenate([ga, gb, hq, hf, hi, hgate, nq, ks, kw, kc, vc, vs, vw, ng, pad], axis=1).astype(BF16)

    p = _norm_matmul(x2, row(mix_norm_g), w_p, F32, ts["tm_proj"], ts["tn_proj"], "in_proj")

    half = dh // 2
    inv_freq = ROPE_THETA ** (-jnp.arange(half, dtype=F32) / half)
    invf = jnp.tile(inv_freq, LANES // half).reshape(1, LANES)
    q_r, kk_r = _rope(p, positions.reshape(t, 1), invf, ts["tm_rope"])
    q5 = q_r.reshape(b, s, g, hg, dh).transpose(0, 2, 3, 1, 4)
    kk = kk_r.reshape(b, s, 2 * g, dh).transpose(0, 2, 1, 3)
    k_slc, k_win = kk[:, :g], kk[:, g:]
    vv = p[:, COL_VSVW:COL_VSVW + 2 * NSA_KV_W].astype(BF16).reshape(b, s, 2 * g, dh).transpose(0, 2, 1, 3)
    v_slc, v_win = vv[:, :g], vv[:, g:]
    gates = p[:, COL_NG:COL_NG + 3 * NSA_HEADS].reshape(b, s, g, hg, 3).transpose(0, 2, 3, 1, 4)

    nr = s // CMP_STRIDE
    kcvc = p[:, COL_KCVC:COL_KCVC + 2 * NSA_KV_W].reshape(b, s, 2, g, dh).transpose(2, 0, 3, 1, 4)
    r = kcvc.reshape(2, b, g, nr, CMP_STRIDE * dh)
    pe = cmp_pe.reshape(2, 2, 1, CMP_STRIDE * dh)
    zeros_w2 = jnp.zeros_like(cmp_w2)
    w2p = jnp.stack([jnp.concatenate([cmp_w2, zeros_w2], axis=-1),
                     jnp.concatenate([zeros_w2, cmp_w2], axis=-1)], axis=1).astype(BF16)
    b2t = jnp.tile(cmp_b2, (1, g)).reshape(2, 1, LANES)
    pos_cmp = positions[:, CMP_BLOCK - 1::CMP_STRIDE]
    pos_cmp = jnp.pad(pos_cmp, ((0, 0), (0, nr - pos_cmp.shape[1]))).reshape(b, nr, 1)
    cmp = _compress(r, pe, cmp_w1.astype(BF16), cmp_b1.reshape(2, 1, CMP_HIDDEN), w2p, b2t, pos_cmp, invf)
    cmp = cmp.reshape(2, b, nr, g, dh).transpose(0, 1, 3, 2, 4)

    y_nsa = _nsa(q5, cmp[0], cmp[1], k_slc, v_slc, k_win, v_win, gates, ts["tq"], ts["tk"])
    y_nsa = y_nsa.transpose(0, 3, 1, 2, 4).reshape(t, NSA_Q_W)

    y_hgrn = _hgrn(p, lb_logits.astype(F32), row(hgrn_norm_g), b, s, ts["hgrn_chunks"])

    x2 = _merge(x2, p, y_nsa, y_hgrn, w_up_nsa.astype(BF16), w_up_hgrn.astype(BF16), w_out.astype(BF16),
                ts["tm_merge"])

    n_mem = mem.shape[1]
    kv = _norm_matmul(mem.reshape(b * n_mem, d), row(xa_mem_norm_g), w_xkv.astype(BF16), BF16,
                      n_mem, 2 * XA_W, "mem_kv").reshape(b, n_mem, 2 * XA_W)
    x2 = _xattn(x2, row(xa_norm_g), w_xq.astype(BF16), kv, w_xo.astype(BF16), s, ts["tm_xattn"])

    n_exp = router_w.shape[1]
    rw = jnp.pad(router_w, ((0, 0), (0, LANES - n_exp))).astype(BF16)
    rb = jnp.pad(router_b, (0, LANES - n_exp)).reshape(1, LANES).astype(F32)
    assert n_exp == N_EXPERTS
    tt = ts["tm_router"]
    nt = t // tt
    hm, idx, rank, wt, cnt = _router(x2, row(moe_norm_g), rw, rb, tt)
    tm = ts["tm_ffn"]
    n8 = (cnt[:, 0, :n_exp].astype(jnp.int32) + SEG_ALIGN - 1) // SEG_ALIGN * SEG_ALIGN
    src_off = jnp.cumsum(n8, axis=1) - n8
    padded = (jnp.sum(n8, axis=0) + tm - 1) // tm * tm
    ends = jnp.cumsum(padded)
    dst_off = (ends - padded)[None, :] + jnp.cumsum(n8, axis=0) - n8
    tables = (src_off.reshape(-1), n8.reshape(-1), dst_off.reshape(-1))
    offv = jnp.pad(src_off.astype(F32), ((0, 0), (0, LANES - n_exp))).reshape(nt, 1, LANES)
    rows = tt * TOP_K + n_exp * SEG_ALIGN
    n_pad = (t * TOP_K + nt * n_exp * SEG_ALIGN + n_exp * tm + tm - 1) // tm * tm
    n_tiles = n_pad // tm
    tile_ids = jnp.arange(n_tiles, dtype=jnp.int32)
    tile_expert = jnp.sum(((ends // tm)[None, :] <= tile_ids[:, None]).astype(jnp.int32), axis=1)
    tile_expert = jnp.minimum(tile_expert, n_exp - 1)
    n_used = (ends[-1] // tm).reshape(1).astype(jnp.int32)

    xs = _dispatch(tables, hm, idx, rank, offv, jnp.zeros((n_pad, d), F32), tt, rows)
    f = moe_w2.shape[1]
    w1p = _w1_prep(moe_w1.reshape(n_exp * d, 2 * f), ts["tm_w1_prep"]).reshape(n_exp, d, 2 * f)
    b1p = moe_b1.reshape(n_exp, f // LANES, LANES, 2).transpose(0, 1, 3, 2).reshape(n_exp, 1, 2 * f)
    ys = _ffn(tile_expert, n_used, xs, w1p, b1p, moe_w2.astype(BF16), moe_b2.reshape(n_exp, 1, d), tm)
    out = _combine(tables, ys, idx, rank, wt, offv, x2, row(out_norm_g), tt, rows)
    return out.reshape(b, s, d)


def kernel(x, mem, positions, mix_norm_g, w_in, cmp_pe, cmp_w1, cmp_b1, cmp_w2, cmp_b2, hgrn_lb_logits, hgrn_norm_g, w_up_nsa, w_up_hgrn, w_out, xa_norm_g, xa_mem_norm_g, w_xq, w_xkv, w_xo, moe_norm_g, router_w, router_b, moe_w1, moe_b1, moe_w2, moe_b2, final_norm_g):
    depth = w_in.shape[0]
    assert depth == 1, "single-layer block: the final norm is fused into the last layer's combine"
    ts = _tile_sizes(x.shape[1])
    l = 0
    return _layer(x, mem, positions, ts, mix_norm_g[l], w_in[l], cmp_pe[l], cmp_w1[l], cmp_b1[l], cmp_w2[l], cmp_b2[l],
                  hgrn_lb_logits, hgrn_norm_g[l], w_up_nsa[l], w_up_hgrn[l], w_out[l], xa_norm_g[l], xa_mem_norm_g[l],
                  w_xq[l], w_xkv[l], w_xo[l], moe_norm_g[l], router_w[l], router_b[l], moe_w1[l], moe_b1[l], moe_w2[l],
                  moe_b2[l], final_norm_g)
```

```python
import functools

import jax
import jax.numpy as jnp
from jax import lax
from jax.experimental import pallas as pl
from jax.experimental.pallas import tpu as pltpu

EPS = 1e-6
ROPE_THETA = 10000.0
NEG_INF = -1e30
FORCE_SCORE = 1e9

NSA_HEADS = 8
NSA_KV_GROUPS = 2
NSA_Q_PER_GROUP = NSA_HEADS // NSA_KV_GROUPS
NSA_HEAD_DIM = 64
CMP_BLOCK = 32
CMP_STRIDE = 16
CMP_HIDDEN = 256
SLC_BLOCK = 64
SLC_TOPK = 16
N_LOCAL_BLOCKS = 2
WINDOW = 512
NSA_Q_W = NSA_HEADS * NSA_HEAD_DIM
NSA_KV_W = NSA_KV_GROUPS * NSA_HEAD_DIM

HGRN_HEADS = 4
HGRN_DK = 128
HGRN_DV = 128
HGRN_CHUNK = 64
HGRN_SUB = 16
HGRN_W = HGRN_HEADS * HGRN_DK

XA_HEADS = 4
XA_HEAD_DIM = 128
XA_W = XA_HEADS * XA_HEAD_DIM

N_EXPERTS = 32
TOP_K = 4
SWIGLU_ALPHA = 1.702
SWIGLU_LIMIT = 7.0

LANES = 128
SEG_ALIGN = 8
LOG2E = 1.4426950408889634
VMEM_LIMIT = 48 * 1024 * 1024

COL_GA = 0
COL_GB = 1024
COL_HQ = 2048
COL_HF = 2560
COL_HI = 3072
COL_HG = 3584
COL_NQ = 4096
COL_KSKW = 4608
COL_KCVC = 4864
COL_VSVW = 5120
COL_NG = 5376
P_WIDTH = 5632

F32 = jnp.float32
BF16 = jnp.bfloat16


def _params(sem):
    return pltpu.CompilerParams(dimension_semantics=sem, vmem_limit_bytes=VMEM_LIMIT)


def _dot(a, b):
    return jnp.dot(a, b, preferred_element_type=F32)


def _dot_nt(a, b):
    return lax.dot_general(a, b, (((1,), (1,)), ((), ())), preferred_element_type=F32)


def _rms(xf, g):
    return xf * lax.rsqrt(jnp.mean(xf * xf, axis=-1, keepdims=True) + EPS) * g


def _sigmoid(x):
    return 1.0 / (1.0 + jnp.exp(-x))


def _norm_matmul_kernel(x_ref, g_ref, w_ref, o_ref, hn_ref):
    @pl.when(pl.program_id(1) == 0)
    def _():
        hn_ref[...] = _rms(x_ref[...], g_ref[...]).astype(BF16)

    o_ref[...] = _dot(hn_ref[...], w_ref[...]).astype(o_ref.dtype)


def _norm_matmul(x, g, w, out_dtype, tm, tn, name):
    t, d = x.shape
    n = w.shape[1]
    return pl.pallas_call(
        _norm_matmul_kernel,
        out_shape=jax.ShapeDtypeStruct((t, n), out_dtype),
        grid=(t // tm, n // tn),
        in_specs=[
            pl.BlockSpec((tm, d), lambda i, j: (i, 0)),
            pl.BlockSpec((1, d), lambda i, j: (0, 0)),
            pl.BlockSpec((d, tn), lambda i, j: (0, j)),
        ],
        out_specs=pl.BlockSpec((tm, tn), lambda i, j: (i, j)),
        scratch_shapes=[pltpu.VMEM((tm, d), BF16)],
        compiler_params=_params(("arbitrary", "arbitrary")),
        name=name,
    )(x, g, w)


def _rope_coeffs(pos_col, invf):
    ang = pos_col.astype(F32) * invf
    lane = lax.broadcasted_iota(jnp.int32, ang.shape, 1)
    first = (lane & (NSA_HEAD_DIM - 1)) < (NSA_HEAD_DIM // 2)
    c = jnp.cos(ang)
    s = jnp.sin(ang)
    return c, jnp.where(first, -s, s), first


def _rope_tile(x, c, s_signed, first):
    half = NSA_HEAD_DIM // 2
    partner = jnp.where(first, pltpu.roll(x, LANES - half, 1), pltpu.roll(x, half, 1))
    return x * c + partner * s_signed


def _rope_kernel(q_ref, k_ref, pos_ref, invf_ref, qo_ref, ko_ref, *, q_scale):
    c, s_signed, first = _rope_coeffs(pos_ref[...], invf_ref[...])
    for i in range(q_ref.shape[1] // LANES):
        sl = slice(i * LANES, (i + 1) * LANES)
        qo_ref[:, sl] = (_rope_tile(q_ref[:, sl], c, s_signed, first) * q_scale).astype(BF16)
    for i in range(k_ref.shape[1] // LANES):
        sl = slice(i * LANES, (i + 1) * LANES)
        ko_ref[:, sl] = _rope_tile(k_ref[:, sl], c, s_signed, first).astype(BF16)


def _rope(p, pos_col, invf, tm):
    t = p.shape[0]
    kw = 2 * NSA_KV_W
    return pl.pallas_call(
        functools.partial(_rope_kernel, q_scale=NSA_HEAD_DIM ** -0.5 * LOG2E),
        out_shape=(jax.ShapeDtypeStruct((t, NSA_Q_W), BF16), jax.ShapeDtypeStruct((t, kw), BF16)),
        grid=(t // tm,),
        in_specs=[
            pl.BlockSpec((tm, NSA_Q_W), lambda i: (i, COL_NQ // NSA_Q_W)),
            pl.BlockSpec((tm, kw), lambda i: (i, COL_KSKW // kw)),
            pl.BlockSpec((tm, 1), lambda i: (i, 0)),
            pl.BlockSpec((1, LANES), lambda i: (0, 0)),
        ],
        out_specs=(
            pl.BlockSpec((tm, NSA_Q_W), lambda i: (i, 0)),
            pl.BlockSpec((tm, kw), lambda i: (i, 0)),
        ),
        compiler_params=_params(("arbitrary",)),
        name="rope",
    )(p, p, pos_col, invf)


def _gelu_tanh(x):
    return 0.5 * x * (1.0 + jnp.tanh(0.7978845608028654 * (x + 0.044715 * (x * x * x))))


def _compress_kernel(r_ref, pe_ref, w1_ref, b1_ref, w2_ref, b2_ref, pos_ref, invf_ref, o_ref):
    nr = r_ref.shape[3]
    half = r_ref.shape[4]
    acc = None
    for g in range(NSA_KV_GROUPS):
        r = r_ref[0, 0, g]
        top = _dot((r + pe_ref[0, 0]).astype(BF16), w1_ref[0, :half, :])
        bot = _dot((r + pe_ref[0, 1]).astype(BF16), w1_ref[0, half:, :])
        pre = top + pltpu.roll(bot, nr - 1, 0) + b1_ref[0]
        part = _dot(_gelu_tanh(pre).astype(BF16), w2_ref[0, g])
        acc = part if acc is None else acc + part
    out = acc + b2_ref[0]
    c, s_signed, first = _rope_coeffs(pos_ref[0], invf_ref[...])
    roped = _rope_tile(out, c, s_signed, first)
    is_key = pl.program_id(0) == 0
    o_ref[0, 0] = jnp.where(is_key, roped, out).astype(BF16)


def _compress(r, pe, w1, b1, w2p, b2t, pos_cmp, invf):
    _, b, g, nr, half = r.shape
    return pl.pallas_call(
        _compress_kernel,
        out_shape=jax.ShapeDtypeStruct((2, b, nr, LANES), BF16),
        grid=(2, b),
        in_specs=[
            pl.BlockSpec((1, 1, g, nr, half), lambda k, i: (k, i, 0, 0, 0)),
            pl.BlockSpec((1, 2, 1, half), lambda k, i: (k, 0, 0, 0)),
            pl.BlockSpec((1, 2 * half, CMP_HIDDEN), lambda k, i: (k, 0, 0)),
            pl.BlockSpec((1, 1, CMP_HIDDEN), lambda k, i: (k, 0, 0)),
            pl.BlockSpec((1, g, CMP_HIDDEN, LANES), lambda k, i: (k, 0, 0, 0)),
            pl.BlockSpec((1, 1, LANES), lambda k, i: (k, 0, 0)),
            pl.BlockSpec((1, nr, 1), lambda k, i: (i, 0, 0)),
            pl.BlockSpec((1, LANES), lambda k, i: (0, 0)),
        ],
        out_specs=pl.BlockSpec((1, 1, nr, LANES), lambda k, i: (k, i, 0, 0)),
        compiler_params=_params(("arbitrary", "arbitrary")),
        name="compress",
    )(r, pe, w1, b1, w2p, b2t, pos_cmp, invf)


def _nsa_kernel(qt_ref, kc_ref, vct_ref, ks_ref, vst_ref, kw_ref, vwt_ref, g_ref, o_ref, acc_ref, out_ref,
                *, tq, tk, seq):
    hg = NSA_Q_PER_GROUP
    dh = NSA_HEAD_DIM
    nc = kc_ref.shape[2]
    nb = seq // SLC_BLOCK
    top_k = min(SLC_TOPK, nb)
    s0 = pl.program_id(2) * tq
    t_lane = s0 + lax.broadcasted_iota(jnp.int32, (1, tq), 1)
    gate = _sigmoid(g_ref[0, 0, 0])

    def scores(k_tile, bias):
        s = _dot(k_tile, qt_ref[0, 0, 0])
        return jnp.concatenate([s[:, h * tq:(h + 1) * tq] + bias for h in range(hg)], axis=1)

    def attend(k_tile, vt_tile, bias, m_old):
        s = scores(k_tile, bias)
        mx = jnp.max(s, axis=0, keepdims=True)
        m_new = mx if m_old is None else jnp.maximum(m_old, mx)
        pv = _dot(vt_tile, jnp.exp2(s - m_new).astype(BF16))
        if m_old is None:
            acc_ref[...] = pv
        else:
            acc_ref[...] = jnp.exp2(m_old - m_new) * acc_ref[...] + pv
        return m_new

    def branch_result():
        a = acc_ref[...]
        return a[:dh] / a[dh:dh + 1]

    n_col = lax.broadcasted_iota(jnp.int32, (nc, 1), 0)
    valid_c = (n_col * CMP_STRIDE + (CMP_BLOCK - 1) <= t_lane) & (n_col < nc - 1)
    s_c = scores(kc_ref[0, 0], jnp.where(valid_c, 0.0, NEG_INF))
    e_c = jnp.exp2(s_c - jnp.max(s_c, axis=0, keepdims=True))
    t_all = s0 + (lax.broadcasted_iota(jnp.int32, (1, hg * tq), 1) & (tq - 1))
    row_ok = t_all >= CMP_BLOCK - 1
    pn = e_c * jnp.where(row_ok, 1.0 / jnp.sum(e_c, axis=0, keepdims=True), 0.0)
    out_ref[...] = gate[0:1] * _dot(vct_ref[0, 0], pn.astype(BF16))[:dh]
    p_sum = pn[:, 0:tq]
    for h in range(1, hg):
        p_sum = p_sum + pn[:, h * tq:(h + 1) * tq]

    j_col = lax.broadcasted_iota(jnp.int32, (nb, 1), 0)
    n_row = lax.broadcasted_iota(jnp.int32, (1, nc), 1) * CMP_STRIDE
    overlap = (n_row < j_col * SLC_BLOCK + SLC_BLOCK) & (n_row + CMP_BLOCK > j_col * SLC_BLOCK)
    overlap = jnp.where(overlap, 1.0, 0.0).astype(BF16)
    p_hi = p_sum.astype(BF16)
    p_lo = (p_sum - p_hi.astype(F32)).astype(BF16)
    imp = _dot(overlap, p_hi) + _dot(overlap, p_lo)

    n_win = WINDOW // tq + 1
    r_col = lax.broadcasted_iota(jnp.int32, (tq, 1), 0)
    m_w = None
    for u in [n_win - 1] + list(range(n_win - 1)):
        start = s0 - WINDOW + u * tq
        wpos = start + r_col
        bias_w = jnp.where((wpos >= 0) & (wpos <= t_lane) & (wpos > t_lane - WINDOW), 0.0, NEG_INF)
        k0 = pl.multiple_of(jnp.maximum(start, 0), tq)
        m_w = attend(kw_ref[0, 0, pl.ds(k0, tq), :], vwt_ref[0, 0, :, pl.ds(k0, tq)], bias_w, m_w)
    out_ref[...] += gate[2:3] * branch_result()

    cur = t_lane >> 6
    causal_b = j_col <= cur
    forced = (j_col == 0) | (causal_b & (j_col > cur - N_LOCAL_BLOCKS))
    score = jnp.where(forced, FORCE_SCORE, jnp.where(causal_b, imp, -1.0))
    j_f = jnp.broadcast_to(j_col.astype(F32), (nb, tq))
    sel = jnp.zeros((nb, tq), F32)
    for _ in range(top_k):
        mx = jnp.max(score, axis=0, keepdims=True)
        first_idx = jnp.min(jnp.where(score == mx, j_f, float(nb)), axis=0, keepdims=True)
        hit = j_f == first_idx
        sel = jnp.where(hit, 1.0, sel)
        score = jnp.where(hit, -jnp.inf, score)
    sel = jnp.where(causal_b, sel, 0.0).astype(BF16)

    k_col = lax.broadcasted_iota(jnp.int32, (tk, 1), 0)
    jb_row = lax.broadcasted_iota(jnp.int32, (1, nb), 1)

    def slc_tile(k0, m_old):
        kpos = k0 + k_col
        expand = jnp.where(jb_row == (kpos >> 6), 1.0, 0.0).astype(BF16)
        blk = _dot(expand, sel)
        bias = jnp.where(kpos <= t_lane, (blk - 1.0) * (-NEG_INF), NEG_INF)
        return attend(ks_ref[0, 0, pl.ds(k0, tk), :], vst_ref[0, 0, :, pl.ds(k0, tk)], bias, m_old)

    m_s = slc_tile(0, None)
    lax.fori_loop(1, (s0 + tq + tk - 1) // tk, lambda kt, m: slc_tile(pl.multiple_of(kt * tk, tk), m), m_s)
    o_ref[0, 0, 0] = (out_ref[...] + gate[1:2] * branch_result()).astype(o_ref.dtype)


def _nsa(qt, kc, vct, ks, vst, kw, vwt, gates, tq, tk):
    b, g, nq, dh, lanes = qt.shape
    hg = lanes // tq
    s = nq * tq
    nr = kc.shape[2]
    k_spec = pl.BlockSpec((1, 1, s, dh), lambda i, j, k: (i, j, 0, 0))
    vt_spec = pl.BlockSpec((1, 1, 2 * dh, s), lambda i, j, k: (i, j, 0, 0))
    return pl.pallas_call(
        functools.partial(_nsa_kernel, tq=tq, tk=tk, seq=s),
        out_shape=jax.ShapeDtypeStruct((b, g, nq, dh, hg * tq), BF16),
        grid=(b, g, nq),
        in_specs=[
            pl.BlockSpec((1, 1, 1, dh, hg * tq), lambda i, j, k: (i, j, k, 0, 0)),
            pl.BlockSpec((1, 1, nr, dh), lambda i, j, k: (i, j, 0, 0)),
            pl.BlockSpec((1, 1, 2 * dh, nr), lambda i, j, k: (i, j, 0, 0)),
            k_spec, vt_spec, k_spec, vt_spec,
            pl.BlockSpec((1, 1, 1, 3, hg * tq), lambda i, j, k: (i, j, k, 0, 0)),
        ],
        out_specs=pl.BlockSpec((1, 1, 1, dh, hg * tq), lambda i, j, k: (i, j, k, 0, 0)),
        scratch_shapes=[pltpu.VMEM((2 * dh, hg * tq), F32), pltpu.VMEM((dh, hg * tq), F32)],
        compiler_params=_params(("arbitrary", "arbitrary", "arbitrary")),
        name="nsa",
    )(qt, kc, vct, ks, vst, kw, vwt, gates)


def _cumsum_rows(x):
    n = x.shape[0]
    row = lax.broadcasted_iota(jnp.int32, x.shape, 0)
    d = 1
    while d < n:
        x = x + jnp.where(row >= d, pltpu.roll(x, d, 0), 0.0)
        d *= 2
    return x


def _hgrn_kernel(q_ref, f_ref, i_ref, g_ref, lb_ref, gn_ref, o_ref, st_ref, *, n_chunks):
    @pl.when(pl.program_id(1) == 0)
    def _():
        st_ref[...] = jnp.zeros_like(st_ref)

    c_len = HGRN_CHUNK
    sub = HGRN_SUB
    lbl = lb_ref[...]
    lb_e = jnp.exp(lbl - jnp.max(lbl, axis=0, keepdims=True))
    lb_all = lb_e[0:1] / jnp.sum(lb_e, axis=0, keepdims=True)

    for c in range(n_chunks):
        rs = slice(c * c_len, (c + 1) * c_len)
        for h in range(HGRN_HEADS):
            ls = slice(h * HGRN_DK, (h + 1) * HGRN_DK)
            lb = lb_all[:, ls]
            f = lb + (1.0 - lb) * _sigmoid(f_ref[rs, ls])
            k = 1.0 - f
            b = _cumsum_rows(jnp.log(f))
            q = q_ref[rs, ls]
            v32 = i_ref[rs, ls]
            v = v32.astype(BF16)
            st = st_ref[h]
            o_inter = _dot_nt((q * jnp.exp(b)).astype(BF16), st.astype(BF16))
            b_end = b[c_len - 1:c_len]
            pieces = []
            for blk in range(c_len // sub):
                lo, hi = blk * sub, (blk + 1) * sub
                beta = b[lo - 1:lo] if blk > 0 else jnp.zeros_like(b_end)
                qd = (q[lo:hi] * jnp.exp(b[lo:hi] - beta)).astype(BF16)
                kd = (k[:hi] * jnp.exp(beta - b[:hi])).astype(BF16)
                a = _dot_nt(qd, kd)
                ti = lax.broadcasted_iota(jnp.int32, (sub, hi), 0)
                si = lax.broadcasted_iota(jnp.int32, (sub, hi), 1)
                a = jnp.where(si <= ti + lo, a, 0.0)
                pieces.append(_dot(a.astype(BF16), v[:hi]))
            o = o_inter + jnp.concatenate(pieces, axis=0)
            kd_end = (k * jnp.exp(b_end - b)).astype(BF16)
            st_ref[h] = st * jnp.exp(b_end) + _dot(v32.T.astype(BF16), kd_end)
            y = _rms(o, gn_ref[...])
            gate = g_ref[rs, ls]
            o_ref[rs, ls] = (y * (gate * _sigmoid(gate))).astype(o_ref.dtype)


def _hgrn(p, lb_logits, gn, batch, seq, n_chunks):
    t = p.shape[0]
    rows = n_chunks * HGRN_CHUNK
    steps = seq // rows

    def col(cb):
        return pl.BlockSpec((rows, HGRN_W), lambda i, j: (i * steps + j, cb))

    return pl.pallas_call(
        functools.partial(_hgrn_kernel, n_chunks=n_chunks),
        out_shape=jax.ShapeDtypeStruct((t, HGRN_W), BF16),
        grid=(batch, steps),
        in_specs=[
            col(COL_HQ // HGRN_W), col(COL_HF // HGRN_W), col(COL_HI // HGRN_W), col(COL_HG // HGRN_W),
            pl.BlockSpec(lb_logits.shape, lambda i, j: (0, 0)),
            pl.BlockSpec((1, HGRN_DV), lambda i, j: (0, 0)),
        ],
        out_specs=pl.BlockSpec((rows, HGRN_W), lambda i, j: (i * steps + j, 0)),
        scratch_shapes=[pltpu.VMEM((HGRN_HEADS, HGRN_DV, HGRN_DK), F32)],
        compiler_params=_params(("arbitrary", "arbitrary")),
        name="hgrn",
    )(p, p, p, p, lb_logits, gn)


def _merge_kernel(x_ref, ga_ref, gb_ref, yn_ref, yh_ref, wn_ref, wh_ref, wo_ref, o_ref):
    mixed = _sigmoid(ga_ref[...]) * _dot(yn_ref[...], wn_ref[...]) + _sigmoid(gb_ref[...]) * _dot(yh_ref[...], wh_ref[...])
    o_ref[...] = x_ref[...] + _dot(mixed.astype(BF16), wo_ref[...])


def _merge(x, p, y_nsa, y_hgrn, wn, wh, wo, tm):
    t, d = x.shape
    full = lambda a: pl.BlockSpec(a.shape, lambda i: (0, 0))
    return pl.pallas_call(
        _merge_kernel,
        out_shape=jax.ShapeDtypeStruct((t, d), F32),
        grid=(t // tm,),
        in_specs=[
            pl.BlockSpec((tm, d), lambda i: (i, 0)),
            pl.BlockSpec((tm, d), lambda i: (i, COL_GA // d)),
            pl.BlockSpec((tm, d), lambda i: (i, COL_GB // d)),
            pl.BlockSpec((tm, NSA_Q_W), lambda i: (i, 0)),
            pl.BlockSpec((tm, HGRN_W), lambda i: (i, 0)),
            full(wn), full(wh), full(wo),
        ],
        out_specs=pl.BlockSpec((tm, d), lambda i: (i, 0)),
        compiler_params=_params(("arbitrary",)),
        name="merge",
    )(x, p, p, y_nsa, y_hgrn, wn, wh, wo)


def _xattn_kernel(x_ref, g_ref, wq_ref, kv_ref, wo_ref, o_ref):
    x = x_ref[...]
    xq = _dot(_rms(x, g_ref[...]).astype(BF16), wq_ref[...]).astype(BF16)
    outs = []
    for h in range(XA_HEADS):
        ls = slice(h * XA_HEAD_DIM, (h + 1) * XA_HEAD_DIM)
        s = _dot_nt(xq[:, ls], kv_ref[0, :, ls]) * (XA_HEAD_DIM ** -0.5)
        e = jnp.exp(s - jnp.max(s, axis=-1, keepdims=True))
        p = e / jnp.sum(e, axis=-1, keepdims=True)
        outs.append(_dot(p.astype(BF16), kv_ref[0, :, XA_W + h * XA_HEAD_DIM:XA_W + (h + 1) * XA_HEAD_DIM]))
    o_x = jnp.concatenate(outs, axis=-1)
    o_ref[...] = x + _dot(o_x.astype(BF16), wo_ref[...])


def _xattn(x, g, wq, kv, wo, seq, tm):
    t, d = x.shape
    steps = seq // tm
    full = lambda a: pl.BlockSpec(a.shape, lambda i: (0, 0))
    return pl.pallas_call(
        _xattn_kernel,
        out_shape=jax.ShapeDtypeStruct((t, d), F32),
        grid=(t // tm,),
        in_specs=[
            pl.BlockSpec((tm, d), lambda i: (i, 0)),
            full(g), full(wq),
            pl.BlockSpec((1,) + kv.shape[1:], lambda i: (i // steps, 0, 0)),
            full(wo),
        ],
        out_specs=pl.BlockSpec((tm, d), lambda i: (i, 0)),
        compiler_params=_params(("arbitrary",)),
        name="xattn",
    )(x, g, wq, kv, wo)


def _router_kernel(x_ref, g_ref, w_ref, b_ref, hm_ref, idx_ref, rank_ref, wt_ref, cnt_ref):
    tm = x_ref.shape[0]
    hm = _rms(x_ref[...], g_ref[...]).astype(BF16)
    hm_ref[...] = hm
    lane = lax.broadcasted_iota(jnp.int32, (tm, LANES), 1)
    lane_f = lane.astype(F32)
    logits = _dot(hm, w_ref[...]) + b_ref[...]
    logits = jnp.where(lane < N_EXPERTS, logits, -jnp.inf)
    picks, vals = [], []
    onehot_all = jnp.zeros((tm, LANES), F32)
    for _ in range(TOP_K):
        mx = jnp.max(logits, axis=-1, keepdims=True)
        first_idx = jnp.min(jnp.where(logits == mx, lane_f, float(LANES)), axis=-1, keepdims=True)
        hit = lane_f == first_idx
        onehot = jnp.where(hit, 1.0, 0.0)
        logits = jnp.where(hit, -jnp.inf, logits)
        picks.append((first_idx, onehot))
        vals.append(mx)
        onehot_all = onehot_all + onehot
    exps = [jnp.exp(v - vals[0]) for v in vals]
    den = exps[0]
    for e in exps[1:]:
        den = den + e
    r_i = lax.broadcasted_iota(jnp.int32, (tm, tm), 0)
    c_i = lax.broadcasted_iota(jnp.int32, (tm, tm), 1)
    lower = jnp.where(c_i < r_i, 1.0, 0.0).astype(BF16)
    before = _dot(lower, onehot_all.astype(BF16))
    idx_out = jnp.zeros((tm, LANES), F32)
    rank_out = jnp.zeros((tm, LANES), F32)
    wt_out = jnp.zeros((tm, LANES), F32)
    for k in range(TOP_K):
        first_idx, onehot = picks[k]
        rank = jnp.sum(onehot * before, axis=-1, keepdims=True)
        idx_out = jnp.where(lane == k, first_idx, idx_out)
        rank_out = jnp.where(lane == k, rank, rank_out)
        wt_out = jnp.where(lane == k, exps[k] / den, wt_out)
    idx_ref[...] = idx_out.astype(jnp.int32)
    rank_ref[...] = rank_out.astype(jnp.int32)
    wt_ref[...] = wt_out
    cnt_ref[0] = jnp.sum(onehot_all, axis=0, keepdims=True)


def _router(x, g, w, b, tm):
    t, d = x.shape
    full = lambda a: pl.BlockSpec(a.shape, lambda i: (0, 0))
    lane_out = pl.BlockSpec((tm, LANES), lambda i: (i, 0))
    return pl.pallas_call(
        _router_kernel,
        out_shape=(
            jax.ShapeDtypeStruct((t, d), BF16),
            jax.ShapeDtypeStruct((t, LANES), jnp.int32),
            jax.ShapeDtypeStruct((t, LANES), jnp.int32),
            jax.ShapeDtypeStruct((t, LANES), F32),
            jax.ShapeDtypeStruct((t // tm, 1, LANES), F32),
        ),
        grid=(t // tm,),
        in_specs=[pl.BlockSpec((tm, d), lambda i: (i, 0)), full(g), full(w), full(b)],
        out_specs=(pl.BlockSpec((tm, d), lambda i: (i, 0)), lane_out, lane_out, lane_out,
                   pl.BlockSpec((1, 1, LANES), lambda i: (i, 0, 0))),
        compiler_params=_params(("arbitrary",)),
        name="router",
    )(x, g, w, b)


def _slot_matrix(idx_ref, rank_ref, offv_ref, rows, values=None):
    tt = idx_ref.shape[0]
    lane = lax.broadcasted_iota(jnp.int32, (tt, LANES), 1)
    r = lax.broadcasted_iota(jnp.int32, (tt, rows), 1)
    offv = offv_ref[0]
    idx = idx_ref[...]
    rank = rank_ref[...]
    out = jnp.zeros((tt, rows), F32)
    for k in range(TOP_K):
        seg = jnp.sum(jnp.where(lane == idx[:, k:k + 1], offv, 0.0), axis=-1, keepdims=True)
        row_k = seg.astype(jnp.int32) + rank[:, k:k + 1]
        out = jnp.where(r == row_k, 1.0 if values is None else values[:, k:k + 1], out)
    return out


def _segment_copies(src_ref, n8_ref, dst_ref, make_copy, tile_tokens):
    base = pl.program_id(0) * N_EXPERTS
    sizes = []
    size = tile_tokens
    while size >= SEG_ALIGN:
        sizes.append(size)
        size //= 2

    def visit(e, start):
        n8 = n8_ref[base + e]
        src = src_ref[base + e]
        dst = dst_ref[base + e]
        for size in sizes:
            done = n8 & (-2 * size)

            @pl.when((n8 & size) != 0)
            def _():
                cp = make_copy(pl.multiple_of(src + done, SEG_ALIGN), pl.multiple_of(dst + done, SEG_ALIGN), size)
                if start:
                    cp.start()
                else:
                    cp.wait()

    def start_all(e, c):
        visit(e, True)
        return c

    def wait_all(e, c):
        visit(e, False)
        return c

    return start_all, wait_all


def _dispatch_kernel(src_ref, n8_ref, dst_ref, hm_ref, idx_ref, rank_ref, offv_ref, xs_in_ref, xs_ref, buf_ref, sem):
    del xs_in_ref
    rows = buf_ref.shape[0]
    onehot = _slot_matrix(idx_ref, rank_ref, offv_ref, rows).astype(BF16)
    buf_ref[...] = lax.dot_general(onehot, hm_ref[...], (((0,), (0,)), ((), ())), preferred_element_type=F32)

    def make_copy(src, dst, size):
        return pltpu.make_async_copy(buf_ref.at[pl.ds(src, size)], xs_ref.at[pl.ds(dst, size)], sem)

    start_all, wait_all = _segment_copies(src_ref, n8_ref, dst_ref, make_copy, hm_ref.shape[0])
    lax.fori_loop(0, N_EXPERTS, start_all, 0)
    lax.fori_loop(0, N_EXPERTS, wait_all, 0)


def _dispatch(tables, hm, idx, rank, offv, xs_zero, tt, rows):
    t, d = hm.shape
    tile = lambda w: pl.BlockSpec((tt, w), lambda i, *_: (i, 0))
    grid_spec = pltpu.PrefetchScalarGridSpec(
        num_scalar_prefetch=3,
        grid=(t // tt,),
        in_specs=[tile(d), tile(LANES), tile(LANES), pl.BlockSpec((1, 1, LANES), lambda i, *_: (i, 0, 0)),
                  pl.BlockSpec(memory_space=pl.ANY)],
        out_specs=pl.BlockSpec(memory_space=pl.ANY),
        scratch_shapes=[pltpu.VMEM((rows, d), F32), pltpu.SemaphoreType.DMA],
    )
    return pl.pallas_call(
        _dispatch_kernel,
        out_shape=jax.ShapeDtypeStruct(xs_zero.shape, xs_zero.dtype),
        grid_spec=grid_spec,
        input_output_aliases={7: 0},
        compiler_params=_params(("arbitrary",)),
        name="dispatch",
    )(*tables, hm, idx, rank, offv, xs_zero)


def _w1_prep_kernel(w_ref, o_ref):
    grp = 2 * LANES
    r_i = lax.broadcasted_iota(jnp.int32, (grp, grp), 0)
    c_i = lax.broadcasted_iota(jnp.int32, (grp, grp), 1)
    src_col = jnp.where(c_i < LANES, 2 * c_i, 2 * (c_i - LANES) + 1)
    perm = jnp.where(r_i == src_col, 1.0, 0.0).astype(BF16)
    for c in range(w_ref.shape[1] // grp):
        sl = slice(c * grp, (c + 1) * grp)
        o_ref[:, sl] = _dot(w_ref[:, sl].astype(BF16), perm).astype(BF16)


def _w1_prep(w, tm):
    r, n = w.shape
    return pl.pallas_call(
        _w1_prep_kernel,
        out_shape=jax.ShapeDtypeStruct((r, n), BF16),
        grid=(r // tm,),
        in_specs=[pl.BlockSpec((tm, n), lambda i: (i, 0))],
        out_specs=pl.BlockSpec((tm, n), lambda i: (i, 0)),
        compiler_params=_params(("arbitrary",)),
        name="w1_prep",
    )(w)


def _ffn_kernel(te_ref, nu_ref, x_ref, w1_ref, b1_ref, w2_ref, b2_ref, o_ref):
    del te_ref
    used = pl.program_id(0) < nu_ref[0]

    @pl.when(used)
    def _():
        u = _dot(x_ref[...].astype(BF16), w1_ref[0]) + b1_ref[0]
        acts = []
        for c in range(u.shape[1] // (2 * LANES)):
            glu = jnp.minimum(u[:, 2 * c * LANES:(2 * c + 1) * LANES], SWIGLU_LIMIT)
            lin = jnp.clip(u[:, (2 * c + 1) * LANES:(2 * c + 2) * LANES], -SWIGLU_LIMIT, SWIGLU_LIMIT)
            acts.append((glu * _sigmoid(SWIGLU_ALPHA * glu) * (lin + 1.0)).astype(BF16))
        o_ref[...] = _dot(jnp.concatenate(acts, axis=-1), w2_ref[0]) + b2_ref[0]

    @pl.when(jnp.logical_not(used))
    def _():
        o_ref[...] = jnp.zeros_like(o_ref)


def _ffn(tile_expert, n_used, xs, w1, b1, w2, b2, tm):
    n_pad, d = xs.shape
    f2 = w1.shape[2]
    f = w2.shape[1]
    grid_spec = pltpu.PrefetchScalarGridSpec(
        num_scalar_prefetch=2,
        grid=(n_pad // tm,),
        in_specs=[
            pl.BlockSpec((tm, d), lambda i, te, nu: (i, 0)),
            pl.BlockSpec((1, d, f2), lambda i, te, nu: (te[i], 0, 0)),
            pl.BlockSpec((1, 1, f2), lambda i, te, nu: (te[i], 0, 0)),
            pl.BlockSpec((1, f, d), lambda i, te, nu: (te[i], 0, 0)),
            pl.BlockSpec((1, 1, d), lambda i, te, nu: (te[i], 0, 0)),
        ],
        out_specs=pl.BlockSpec((tm, d), lambda i, te, nu: (i, 0)),
    )
    return pl.pallas_call(
        _ffn_kernel,
        out_shape=jax.ShapeDtypeStruct((n_pad, d), F32),
        grid_spec=grid_spec,
        compiler_params=_params(("arbitrary",)),
        name="expert_ffn",
    )(tile_expert, n_used, xs, w1, b1, w2, b2)


def _combine_kernel(src_ref, n8_ref, dst_ref, ys_ref, idx_ref, rank_ref, wt_ref, offv_ref, x_ref, g_ref, o_ref,
                    buf_ref, sem):
    rows = buf_ref.shape[0]

    @pl.when(pl.program_id(0) == 0)
    def _():
        buf_ref[...] = jnp.zeros_like(buf_ref)

    def make_copy(src, dst, size):
        return pltpu.make_async_copy(ys_ref.at[pl.ds(dst, size)], buf_ref.at[pl.ds(src, size)], sem)

    start_all, wait_all = _segment_copies(src_ref, n8_ref, dst_ref, make_copy, x_ref.shape[0])
    lax.fori_loop(0, N_EXPERTS, start_all, 0)
    pw = _slot_matrix(idx_ref, rank_ref, offv_ref, rows, values=wt_ref[...])
    p_hi = pw.astype(BF16)
    p_lo = (pw - p_hi.astype(F32)).astype(BF16)
    lax.fori_loop(0, N_EXPERTS, wait_all, 0)
    ys = buf_ref[...]
    y_hi = ys.astype(BF16)
    y_lo = (ys - y_hi.astype(F32)).astype(BF16)
    y = _dot(p_hi, y_hi) + _dot(p_hi, y_lo) + _dot(p_lo, y_hi)
    o_ref[...] = _rms(x_ref[...] + y, g_ref[...])


def _combine(tables, ys, idx, rank, wt, offv, x, g, tt, rows):
    t, d = x.shape
    tile = lambda w: pl.BlockSpec((tt, w), lambda i, *_: (i, 0))
    grid_spec = pltpu.PrefetchScalarGridSpec(
        num_scalar_prefetch=3,
        grid=(t // tt,),
        in_specs=[pl.BlockSpec(memory_space=pl.ANY), tile(LANES), tile(LANES), tile(LANES),
                  pl.BlockSpec((1, 1, LANES), lambda i, *_: (i, 0, 0)), tile(d),
                  pl.BlockSpec((1, d), lambda i, *_: (0, 0))],
        out_specs=tile(d),
        scratch_shapes=[pltpu.VMEM((rows, d), F32), pltpu.SemaphoreType.DMA],
    )
    return pl.pallas_call(
        _combine_kernel,
        out_shape=jax.ShapeDtypeStruct((t, d), F32),
        grid_spec=grid_spec,
        compiler_params=_params(("arbitrary",)),
        name="combine",
    )(*tables, ys, idx, rank, wt, offv, x, g)


def _tile_sizes(seq):
    return dict(
        tm_proj=1024, tn_proj=P_WIDTH // 4,
        tm_rope=512,
        tq=128, tk=512,
        hgrn_chunks=4,
        tm_merge=512, tm_xattn=512,
        tm_router=512,
        tm_w1_prep=512,
        tm_ffn=512,
    )


def _layer(x, mem, positions, ts, mix_norm_g, w_in, cmp_pe, cmp_w1, cmp_b1, cmp_w2, cmp_b2, lb_logits, hgrn_norm_g,
           w_up_nsa, w_up_hgrn, w_out, xa_norm_g, xa_mem_norm_g, w_xq, w_xkv, w_xo, moe_norm_g, router_w, router_b,
           moe_w1, moe_b1, moe_w2, moe_b2, out_norm_g):
    b, s, d = x.shape
    t = b * s
    g, hg, dh = NSA_KV_GROUPS, NSA_Q_PER_GROUP, NSA_HEAD_DIM
    x2 = x.reshape(t, d)
    row = lambda v: v.reshape(1, -1).astype(F32)

    splits = [0]
    for w in (d, d, NSA_Q_W) + (NSA_KV_W,) * 6 + (3 * NSA_HEADS,) + (HGRN_W,) * 4:
        splits.append(splits[-1] + w)
    seg = lambda i: w_in[:, splits[i]:splits[i + 1]]
    (ga, gb, nq, kc, vc, ks, vs, kw, vw, ng, hq, hf, hi, hgate) = [seg(i) for i in range(14)]
    pad = jnp.zeros((d, P_WIDTH - COL_NG - 3 * NSA_HEADS), w_in.dtype)
    w_p = jnp.concatenate([ga, gb, hq, hf, hi, hgate, nq, ks, kw, kc, vc, vs, vw, ng, pad], axis=1).astype(BF16)

    p = _norm_matmul(x2, row(mix_norm_g), w_p, F32, ts["tm_proj"], ts["tn_proj"], "in_proj")

    half = dh // 2
    inv_freq = ROPE_THETA ** (-jnp.arange(half, dtype=F32) / half)
    invf = jnp.tile(inv_freq, LANES // half).reshape(1, LANES)
    q_r, kk_r = _rope(p, positions.reshape(t, 1), invf, ts["tm_rope"])
    tq = ts["tq"]
    nq = s // tq
    qt = q_r.reshape(b, nq, tq, g, hg, dh).transpose(0, 3, 1, 5, 4, 2).reshape(b, g, nq, dh, hg * tq)
    kk = kk_r.reshape(b, s, 2 * g, dh).transpose(0, 2, 1, 3)
    k_slc, k_win = kk[:, :g], kk[:, g:]

    def values_t(v):
        vt = v.transpose(0, 1, 3, 2).astype(BF16)
        ones = jnp.ones(vt.shape[:2] + (1, vt.shape[3]), BF16)
        zeros = jnp.zeros(vt.shape[:2] + (dh - 1, vt.shape[3]), BF16)
        return jnp.concatenate([vt, ones, zeros], axis=2)

    vvt = values_t(p[:, COL_VSVW:COL_VSVW + 2 * NSA_KV_W].reshape(b, s, 2 * g, dh).transpose(0, 2, 1, 3))
    vt_slc, vt_win = vvt[:, :g], vvt[:, g:]
    gates = p[:, COL_NG:COL_NG + 3 * NSA_HEADS].reshape(b, nq, tq, g, hg, 3).transpose(0, 3, 1, 5, 4, 2)
    gates = gates.reshape(b, g, nq, 3, hg * tq)

    nr = s // CMP_STRIDE
    kcvc = p[:, COL_KCVC:COL_KCVC + 2 * NSA_KV_W].reshape(b, s, 2, g, dh).transpose(2, 0, 3, 1, 4)
    r = kcvc.reshape(2, b, g, nr, CMP_STRIDE * dh)
    pe = cmp_pe.reshape(2, 2, 1, CMP_STRIDE * dh)
    zeros_w2 = jnp.zeros_like(cmp_w2)
    w2p = jnp.stack([jnp.concatenate([cmp_w2, zeros_w2], axis=-1),
                     jnp.concatenate([zeros_w2, cmp_w2], axis=-1)], axis=1).astype(BF16)
    b2t = jnp.tile(cmp_b2, (1, g)).reshape(2, 1, LANES)
    pos_cmp = positions[:, CMP_BLOCK - 1::CMP_STRIDE]
    pos_cmp = jnp.pad(pos_cmp, ((0, 0), (0, nr - pos_cmp.shape[1]))).reshape(b, nr, 1)
    cmp = _compress(r, pe, cmp_w1.astype(BF16), cmp_b1.reshape(2, 1, CMP_HIDDEN), w2p, b2t, pos_cmp, invf)
    cmp = cmp.reshape(2, b, nr, g, dh).transpose(0, 1, 3, 2, 4)

    y_nsa = _nsa(qt, cmp[0], values_t(cmp[1]), k_slc, vt_slc, k_win, vt_win, gates, tq, ts["tk"])
    y_nsa = y_nsa.reshape(b, g, nq, dh, hg, tq).transpose(0, 2, 5, 1, 4, 3).reshape(t, NSA_Q_W)

    y_hgrn = _hgrn(p, lb_logits.astype(F32), row(hgrn_norm_g), b, s, ts["hgrn_chunks"])

    x2 = _merge(x2, p, y_nsa, y_hgrn, w_up_nsa.astype(BF16), w_up_hgrn.astype(BF16), w_out.astype(BF16),
                ts["tm_merge"])

    n_mem = mem.shape[1]
    kv = _norm_matmul(mem.reshape(b * n_mem, d), row(xa_mem_norm_g), w_xkv.astype(BF16), BF16,
                      n_mem, 2 * XA_W, "mem_kv").reshape(b, n_mem, 2 * XA_W)
    x2 = _xattn(x2, row(xa_norm_g), w_xq.astype(BF16), kv, w_xo.astype(BF16), s, ts["tm_xattn"])

    n_exp = router_w.shape[1]
    rw = jnp.pad(router_w, ((0, 0), (0, LANES - n_exp))).astype(BF16)
    rb = jnp.pad(router_b, (0, LANES - n_exp)).reshape(1, LANES).astype(F32)
    assert n_exp == N_EXPERTS
    tt = ts["tm_router"]
    nt = t // tt
    hm, idx, rank, wt, cnt = _router(x2, row(moe_norm_g), rw, rb, tt)
    tm = ts["tm_ffn"]
    n8 = (cnt[:, 0, :n_exp].astype(jnp.int32) + SEG_ALIGN - 1) // SEG_ALIGN * SEG_ALIGN
    src_off = jnp.cumsum(n8, axis=1) - n8
    padded = (jnp.sum(n8, axis=0) + tm - 1) // tm * tm
    ends = jnp.cumsum(padded)
    dst_off = (ends - padded)[None, :] + jnp.cumsum(n8, axis=0) - n8
    tables = (src_off.reshape(-1), n8.reshape(-1), dst_off.reshape(-1))
    offv = jnp.pad(src_off.astype(F32), ((0, 0), (0, LANES - n_exp))).reshape(nt, 1, LANES)
    rows = tt * TOP_K + n_exp * SEG_ALIGN
    n_pad = (t * TOP_K + nt * n_exp * SEG_ALIGN + n_exp * tm + tm - 1) // tm * tm
    n_tiles = n_pad // tm
    tile_ids = jnp.arange(n_tiles, dtype=jnp.int32)
    tile_expert = jnp.sum(((ends // tm)[None, :] <= tile_ids[:, None]).astype(jnp.int32), axis=1)
    tile_expert = jnp.minimum(tile_expert, n_exp - 1)
    n_used = (ends[-1] // tm).reshape(1).astype(jnp.int32)

    xs = _dispatch(tables, hm, idx, rank, offv, jnp.zeros((n_pad, d), F32), tt, rows)
    f = moe_w2.shape[1]
    w1p = _w1_prep(moe_w1.reshape(n_exp * d, 2 * f), ts["tm_w1_prep"]).reshape(n_exp, d, 2 * f)
    b1p = moe_b1.reshape(n_exp, f // LANES, LANES, 2).transpose(0, 1, 3, 2).reshape(n_exp, 1, 2 * f)
    ys = _ffn(tile_expert, n_used, xs, w1p, b1p, moe_w2.astype(BF16), moe_b2.reshape(n_exp, 1, d), tm)
    out = _combine(tables, ys, idx, rank, wt, offv, x2, row(out_norm_g), tt, rows)
    return out.reshape(b, s, d)


def kernel(x, mem, positions, mix_norm_g, w_in, cmp_pe, cmp_w1, cmp_b1, cmp_w2, cmp_b2, hgrn_lb_logits, hgrn_norm_g, w_up_nsa, w_up_hgrn, w_out, xa_norm_g, xa_mem_norm_g, w_xq, w_xkv, w_xo, moe_norm_g, router_w, router_b, moe_w1, moe_b1, moe_w2, moe_b2, final_norm_g):
    depth = w_in.shape[0]
    assert depth == 1, "single-layer block: the final norm is fused into the last layer's combine"
    ts = _tile_sizes(x.shape[1])
    l = 0
    return _layer(x, mem, positions, ts, mix_norm_g[l], w_in[l], cmp_pe[l], cmp_w1[l], cmp_b1[l], cmp_w2[l], cmp_b2[l],
                  hgrn_lb_logits, hgrn_norm_g[l], w_up_nsa[l], w_up_hgrn[l], w_out[l], xa_norm_g[l], xa_mem_norm_g[l],
                  w_xq[l], w_xkv[l], w_xo[l], moe_norm_g[l], router_w[l], router_b[l], moe_w1[l], moe_b1[l], moe_w2[l],
                  moe_b2[l], final_norm_g)
```

```python
import functools

import jax
import jax.numpy as jnp
from jax import lax
from jax.experimental import pallas as pl
from jax.experimental.pallas import tpu as pltpu

EPS = 1e-6
ROPE_THETA = 10000.0
NEG_INF = -1e30
FORCE_SCORE = 1e9

NSA_HEADS = 8
NSA_KV_GROUPS = 2
NSA_Q_PER_GROUP = NSA_HEADS // NSA_KV_GROUPS
NSA_HEAD_DIM = 64
CMP_BLOCK = 32
CMP_STRIDE = 16
CMP_HIDDEN = 256
SLC_BLOCK = 64
SLC_TOPK = 16
N_LOCAL_BLOCKS = 2
WINDOW = 512
NSA_Q_W = NSA_HEADS * NSA_HEAD_DIM
NSA_KV_W = NSA_KV_GROUPS * NSA_HEAD_DIM

HGRN_HEADS = 4
HGRN_DK = 128
HGRN_DV = 128
HGRN_CHUNK = 64
HGRN_SUB = 16
HGRN_W = HGRN_HEADS * HGRN_DK

XA_HEADS = 4
XA_HEAD_DIM = 128
XA_W = XA_HEADS * XA_HEAD_DIM

N_EXPERTS = 32
TOP_K = 4
SWIGLU_ALPHA = 1.702
SWIGLU_LIMIT = 7.0

LANES = 128
SEG_ALIGN = 8
LOG2E = 1.4426950408889634
VMEM_LIMIT = 48 * 1024 * 1024

COL_GA = 0
COL_GB = 1024
COL_HQ = 2048
COL_HF = 2560
COL_HI = 3072
COL_HG = 3584
COL_NQ = 4096
COL_KSKW = 4608
COL_KCVC = 4864
COL_VSVW = 5120
COL_NG = 5376
P_WIDTH = 5632

F32 = jnp.float32
BF16 = jnp.bfloat16


def _params(sem):
    return pltpu.CompilerParams(dimension_semantics=sem, vmem_limit_bytes=VMEM_LIMIT)


def _dot(a, b):
    return jnp.dot(a, b, preferred_element_type=F32)


def _dot_nt(a, b):
    return lax.dot_general(a, b, (((1,), (1,)), ((), ())), preferred_element_type=F32)


def _rms(xf, g):
    return xf * lax.rsqrt(jnp.mean(xf * xf, axis=-1, keepdims=True) + EPS) * g


def _sigmoid(x):
    return 1.0 / (1.0 + jnp.exp(-x))


def _norm_matmul_kernel(x_ref, g_ref, w_ref, o_ref, hn_ref):
    @pl.when(pl.program_id(1) == 0)
    def _():
        hn_ref[...] = _rms(x_ref[...], g_ref[...]).astype(BF16)

    o_ref[...] = _dot(hn_ref[...], w_ref[...]).astype(o_ref.dtype)


def _norm_matmul(x, g, w, out_dtype, tm, tn, name):
    t, d = x.shape
    n = w.shape[1]
    return pl.pallas_call(
        _norm_matmul_kernel,
        out_shape=jax.ShapeDtypeStruct((t, n), out_dtype),
        grid=(t // tm, n // tn),
        in_specs=[
            pl.BlockSpec((tm, d), lambda i, j: (i, 0)),
            pl.BlockSpec((1, d), lambda i, j: (0, 0)),
            pl.BlockSpec((d, tn), lambda i, j: (0, j)),
        ],
        out_specs=pl.BlockSpec((tm, tn), lambda i, j: (i, j)),
        scratch_shapes=[pltpu.VMEM((tm, d), BF16)],
        compiler_params=_params(("arbitrary", "arbitrary")),
        name=name,
    )(x, g, w)


def _rope_coeffs(pos_col, invf):
    ang = pos_col.astype(F32) * invf
    lane = lax.broadcasted_iota(jnp.int32, ang.shape, 1)
    first = (lane & (NSA_HEAD_DIM - 1)) < (NSA_HEAD_DIM // 2)
    c = jnp.cos(ang)
    s = jnp.sin(ang)
    return c, jnp.where(first, -s, s), first


def _rope_tile(x, c, s_signed, first):
    half = NSA_HEAD_DIM // 2
    partner = jnp.where(first, pltpu.roll(x, LANES - half, 1), pltpu.roll(x, half, 1))
    return x * c + partner * s_signed


def _rope_kernel(q_ref, k_ref, pos_ref, invf_ref, qo_ref, ko_ref, *, q_scale):
    c, s_signed, first = _rope_coeffs(pos_ref[...], invf_ref[...])
    for i in range(q_ref.shape[1] // LANES):
        sl = slice(i * LANES, (i + 1) * LANES)
        qo_ref[:, sl] = (_rope_tile(q_ref[:, sl], c, s_signed, first) * q_scale).astype(BF16)
    for i in range(k_ref.shape[1] // LANES):
        sl = slice(i * LANES, (i + 1) * LANES)
        ko_ref[:, sl] = _rope_tile(k_ref[:, sl], c, s_signed, first).astype(BF16)


def _rope(p, pos_col, invf, tm):
    t = p.shape[0]
    kw = 2 * NSA_KV_W
    return pl.pallas_call(
        functools.partial(_rope_kernel, q_scale=NSA_HEAD_DIM ** -0.5 * LOG2E),
        out_shape=(jax.ShapeDtypeStruct((t, NSA_Q_W), BF16), jax.ShapeDtypeStruct((t, kw), BF16)),
        grid=(t // tm,),
        in_specs=[
            pl.BlockSpec((tm, NSA_Q_W), lambda i: (i, COL_NQ // NSA_Q_W)),
            pl.BlockSpec((tm, kw), lambda i: (i, COL_KSKW // kw)),
            pl.BlockSpec((tm, 1), lambda i: (i, 0)),
            pl.BlockSpec((1, LANES), lambda i: (0, 0)),
        ],
        out_specs=(
            pl.BlockSpec((tm, NSA_Q_W), lambda i: (i, 0)),
            pl.BlockSpec((tm, kw), lambda i: (i, 0)),
        ),
        compiler_params=_params(("arbitrary",)),
        name="rope",
    )(p, p, pos_col, invf)


def _gelu_tanh(x):
    return 0.5 * x * (1.0 + jnp.tanh(0.7978845608028654 * (x + 0.044715 * (x * x * x))))


def _compress_kernel(r_ref, pe_ref, w1_ref, b1_ref, w2_ref, b2_ref, pos_ref, invf_ref, o_ref):
    nr = r_ref.shape[3]
    half = r_ref.shape[4]
    acc = None
    for g in range(NSA_KV_GROUPS):
        r = r_ref[0, 0, g]
        top = _dot((r + pe_ref[0, 0]).astype(BF16), w1_ref[0, :half, :])
        bot = _dot((r + pe_ref[0, 1]).astype(BF16), w1_ref[0, half:, :])
        pre = top + pltpu.roll(bot, nr - 1, 0) + b1_ref[0]
        part = _dot(_gelu_tanh(pre).astype(BF16), w2_ref[0, g])
        acc = part if acc is None else acc + part
    out = acc + b2_ref[0]
    c, s_signed, first = _rope_coeffs(pos_ref[0], invf_ref[...])
    roped = _rope_tile(out, c, s_signed, first)
    is_key = pl.program_id(0) == 0
    o_ref[0, 0] = jnp.where(is_key, roped, out).astype(BF16)


def _compress(r, pe, w1, b1, w2p, b2t, pos_cmp, invf):
    _, b, g, nr, half = r.shape
    return pl.pallas_call(
        _compress_kernel,
        out_shape=jax.ShapeDtypeStruct((2, b, nr, LANES), BF16),
        grid=(2, b),
        in_specs=[
            pl.BlockSpec((1, 1, g, nr, half), lambda k, i: (k, i, 0, 0, 0)),
            pl.BlockSpec((1, 2, 1, half), lambda k, i: (k, 0, 0, 0)),
            pl.BlockSpec((1, 2 * half, CMP_HIDDEN), lambda k, i: (k, 0, 0)),
            pl.BlockSpec((1, 1, CMP_HIDDEN), lambda k, i: (k, 0, 0)),
            pl.BlockSpec((1, g, CMP_HIDDEN, LANES), lambda k, i: (k, 0, 0, 0)),
            pl.BlockSpec((1, 1, LANES), lambda k, i: (k, 0, 0)),
            pl.BlockSpec((1, nr, 1), lambda k, i: (i, 0, 0)),
            pl.BlockSpec((1, LANES), lambda k, i: (0, 0)),
        ],
        out_specs=pl.BlockSpec((1, 1, nr, LANES), lambda k, i: (k, i, 0, 0)),
        compiler_params=_params(("arbitrary", "arbitrary")),
        name="compress",
    )(r, pe, w1, b1, w2p, b2t, pos_cmp, invf)


def _nsa_kernel(qt_ref, kc_ref, vct_ref, ksa_ref, vst_ref, kw_ref, vwt_ref, g_ref, o_ref, acc_ref, out_ref,
                sa_ref, sb_ref, qa_ref, *, tq, tk, seq):
    hg = NSA_Q_PER_GROUP
    dh = NSA_HEAD_DIM
    nc = kc_ref.shape[2]
    nb = seq // SLC_BLOCK
    top_k = min(SLC_TOPK, nb)
    s0 = pl.program_id(2) * tq
    t_lane = s0 + lax.broadcasted_iota(jnp.int32, (1, tq), 1)
    gate = _sigmoid(g_ref[0, 0, 0])

    def scores(k_tile, bias):
        s = _dot(k_tile, qt_ref[0, 0, 0])
        return jnp.concatenate([s[:, h * tq:(h + 1) * tq] + bias for h in range(hg)], axis=1)

    def normalised(acc):
        return acc[:dh] / acc[dh:dh + 1]

    n_col = lax.broadcasted_iota(jnp.int32, (nc, 1), 0)
    valid_c = (n_col * CMP_STRIDE + (CMP_BLOCK - 1) <= t_lane) & (n_col < nc - 1)
    win_keys = WINDOW + tq
    w0 = pl.multiple_of(jnp.maximum(s0 - WINDOW, 0), tq)
    wpos = w0 + lax.broadcasted_iota(jnp.int32, (win_keys, 1), 0)
    bias_w = jnp.where((wpos <= t_lane) & (wpos > t_lane - WINDOW), 0.0, NEG_INF)
    s_c = scores(kc_ref[0, 0], jnp.where(valid_c, 0.0, NEG_INF))
    s_w = scores(kw_ref[0, 0, pl.ds(w0, win_keys), :], bias_w)

    e_c = jnp.exp2(s_c - jnp.max(s_c, axis=0, keepdims=True))
    t_all = s0 + (lax.broadcasted_iota(jnp.int32, (1, hg * tq), 1) & (tq - 1))
    row_ok = t_all >= CMP_BLOCK - 1
    pn = e_c * jnp.where(row_ok, 1.0 / jnp.sum(e_c, axis=0, keepdims=True), 0.0)
    out_ref[...] = gate[0:1] * _dot(vct_ref[0, 0], pn.astype(BF16))[:dh]
    p_sum = pn[:, 0:tq]
    for h in range(1, hg):
        p_sum = p_sum + pn[:, h * tq:(h + 1) * tq]

    j_col = lax.broadcasted_iota(jnp.int32, (nb, 1), 0)
    n_row = lax.broadcasted_iota(jnp.int32, (1, nc), 1) * CMP_STRIDE
    overlap = (n_row < j_col * SLC_BLOCK + SLC_BLOCK) & (n_row + CMP_BLOCK > j_col * SLC_BLOCK)
    overlap = jnp.where(overlap, 1.0, 0.0).astype(BF16)
    p_hi = p_sum.astype(BF16)
    p_lo = (p_sum - p_hi.astype(F32)).astype(BF16)
    imp = _dot(overlap, p_hi) + _dot(overlap, p_lo)

    p_w = jnp.exp2(s_w - jnp.max(s_w, axis=0, keepdims=True)).astype(BF16)
    out_ref[...] += gate[2:3] * normalised(_dot(vwt_ref[0, 0, :, pl.ds(w0, win_keys)], p_w))

    cur = t_lane >> 6
    causal_b = j_col <= cur
    forced = (j_col == 0) | (causal_b & (j_col > cur - N_LOCAL_BLOCKS))
    score = jnp.where(forced, FORCE_SCORE, jnp.where(causal_b, imp, -1.0))
    j_f = jnp.broadcast_to(j_col.astype(F32), (nb, tq))
    sel = jnp.zeros((nb, tq), F32)
    for _ in range(top_k):
        mx = jnp.max(score, axis=0, keepdims=True)
        first_idx = jnp.min(jnp.where(score == mx, j_f, float(nb)), axis=0, keepdims=True)
        hit = j_f == first_idx
        sel = jnp.where(hit, 1.0, sel)
        score = jnp.where(hit, -jnp.inf, score)
    sel = jnp.where(causal_b, sel, 0.0)

    sel_bias = ((sel - 1.0) * (-NEG_INF)).astype(BF16)
    qa_ref[:dh] = qt_ref[0, 0, 0]
    qa_ref[dh:] = jnp.concatenate([sel_bias] * hg, axis=1)
    k_col = lax.broadcasted_iota(jnp.int32, (tk, 1), 0)
    last_k0 = seq - tk

    def tile_start(k0):
        return pl.multiple_of(jnp.minimum(k0, last_k0), tk)

    def put_scores(k0, buf_ref):
        buf_ref[...] = _dot(ksa_ref[0, 0, pl.ds(tile_start(k0), tk), :], qa_ref[...])

    def consume(k0, buf_ref, m_old, causal):
        s = buf_ref[...]
        if causal:
            bias = jnp.where(k0 + k_col <= t_lane, 0.0, NEG_INF)
            s = jnp.concatenate([s[:, h * tq:(h + 1) * tq] + bias for h in range(hg)], axis=1)
        m_new = jnp.maximum(m_old, jnp.max(s, axis=0, keepdims=True))
        pv = _dot(vst_ref[0, 0, :, pl.ds(tile_start(k0), tk)], jnp.exp2(s - m_new).astype(BF16))
        acc_ref[...] = jnp.exp2(m_old - m_new) * acc_ref[...] + pv
        return m_new

    def slc_pair(k0, m, last):
        put_scores(k0 + tk, sb_ref)
        m = consume(k0, sa_ref, m, last)
        if not last:
            put_scores(k0 + 2 * tk, sa_ref)
        return consume(k0 + tk, sb_ref, m, last)

    acc_ref[...] = jnp.zeros_like(acc_ref)
    put_scores(jnp.int32(0), sa_ref)
    n_full = s0 // (2 * tk)
    m_s = lax.fori_loop(0, n_full, lambda it, m: slc_pair(it * (2 * tk), m, False),
                        jnp.full((1, hg * tq), NEG_INF, F32))
    slc_pair(n_full * (2 * tk), m_s, True)
    o_ref[0, 0, 0] = (out_ref[...] + gate[1:2] * normalised(acc_ref[...])).astype(o_ref.dtype)


def _nsa(qt, kc, vct, ksa, vst, kw, vwt, gates, tq, tk):
    b, g, nq, dh, lanes = qt.shape
    hg = lanes // tq
    s = nq * tq
    nr = kc.shape[2]
    k_spec = pl.BlockSpec((1, 1, s, dh), lambda i, j, k: (i, j, 0, 0))
    vt_spec = pl.BlockSpec((1, 1, 2 * dh, s), lambda i, j, k: (i, j, 0, 0))
    return pl.pallas_call(
        functools.partial(_nsa_kernel, tq=tq, tk=tk, seq=s),
        out_shape=jax.ShapeDtypeStruct((b, g, nq, dh, hg * tq), BF16),
        grid=(b, g, nq),
        in_specs=[
            pl.BlockSpec((1, 1, 1, dh, hg * tq), lambda i, j, k: (i, j, k, 0, 0)),
            pl.BlockSpec((1, 1, nr, dh), lambda i, j, k: (i, j, 0, 0)),
            pl.BlockSpec((1, 1, 2 * dh, nr), lambda i, j, k: (i, j, 0, 0)),
            pl.BlockSpec((1, 1, s, ksa.shape[3]), lambda i, j, k: (i, j, 0, 0)), vt_spec, k_spec, vt_spec,
            pl.BlockSpec((1, 1, 1, 3, hg * tq), lambda i, j, k: (i, j, k, 0, 0)),
        ],
        out_specs=pl.BlockSpec((1, 1, 1, dh, hg * tq), lambda i, j, k: (i, j, k, 0, 0)),
        scratch_shapes=[pltpu.VMEM((2 * dh, hg * tq), F32), pltpu.VMEM((dh, hg * tq), F32),
                        pltpu.VMEM((tk, hg * tq), F32), pltpu.VMEM((tk, hg * tq), F32),
                        pltpu.VMEM((ksa.shape[3], hg * tq), BF16)],
        compiler_params=_params(("arbitrary", "arbitrary", "arbitrary")),
        name="nsa",
    )(qt, kc, vct, ksa, vst, kw, vwt, gates)


def _cumsum_rows(x):
    n = x.shape[0]
    row = lax.broadcasted_iota(jnp.int32, x.shape, 0)
    d = 1
    while d < n:
        x = x + jnp.where(row >= d, pltpu.roll(x, d, 0), 0.0)
        d *= 2
    return x


def _hgrn_kernel(q_ref, f_ref, i_ref, g_ref, lb_ref, gn_ref, o_ref, st_ref, *, n_chunks):
    @pl.when(pl.program_id(1) == 0)
    def _():
        st_ref[...] = jnp.zeros_like(st_ref)

    c_len = HGRN_CHUNK
    sub = HGRN_SUB
    lbl = lb_ref[...]
    lb_e = jnp.exp(lbl - jnp.max(lbl, axis=0, keepdims=True))
    lb_all = lb_e[0:1] / jnp.sum(lb_e, axis=0, keepdims=True)

    items = [(c, h) for c in range(n_chunks) for h in range(HGRN_HEADS)]
    wave1 = {}
    for c, h in items:
        rs = slice(c * c_len, (c + 1) * c_len)
        ls = slice(h * HGRN_DK, (h + 1) * HGRN_DK)
        lb = lb_all[:, ls]
        f = lb + (1.0 - lb) * _sigmoid(f_ref[rs, ls])
        k = 1.0 - f
        b = _cumsum_rows(jnp.log(f))
        q = q_ref[rs, ls]
        v32 = i_ref[rs, ls]
        b_end = b[c_len - 1:c_len]
        attn = []
        for blk in range(c_len // sub):
            lo, hi = blk * sub, (blk + 1) * sub
            beta = b[lo - 1:lo] if blk > 0 else jnp.zeros_like(b_end)
            qd = (q[lo:hi] * jnp.exp(b[lo:hi] - beta)).astype(BF16)
            kd = (k[:hi] * jnp.exp(beta - b[:hi])).astype(BF16)
            attn.append(_dot_nt(qd, kd))
        update = _dot(v32.T.astype(BF16), (k * jnp.exp(b_end - b)).astype(BF16))
        wave1[c, h] = ((q * jnp.exp(b)).astype(BF16), jnp.exp(b_end), update, attn)

    o_inter = {}
    for h in range(HGRN_HEADS):
        st = st_ref[h]
        for c in range(n_chunks):
            q_dec, decay, update, _ = wave1[c, h]
            o_inter[c, h] = _dot_nt(q_dec, st.astype(BF16))
            st = st * decay + update
        st_ref[h] = st

    for c, h in items:
        rs = slice(c * c_len, (c + 1) * c_len)
        ls = slice(h * HGRN_DK, (h + 1) * HGRN_DK)
        v = i_ref[rs, ls].astype(BF16)
        pieces = []
        for blk, a in enumerate(wave1[c, h][3]):
            lo, hi = blk * sub, (blk + 1) * sub
            ti = lax.broadcasted_iota(jnp.int32, (sub, hi), 0)
            si = lax.broadcasted_iota(jnp.int32, (sub, hi), 1)
            pieces.append(_dot(jnp.where(si <= ti + lo, a, 0.0).astype(BF16), v[:hi]))
        o = o_inter[c, h] + jnp.concatenate(pieces, axis=0)
        gate = g_ref[rs, ls]
        o_ref[rs, ls] = (_rms(o, gn_ref[...]) * (gate * _sigmoid(gate))).astype(o_ref.dtype)


def _hgrn(p, lb_logits, gn, batch, seq, n_chunks):
    t = p.shape[0]
    rows = n_chunks * HGRN_CHUNK
    steps = seq // rows

    def col(cb):
        return pl.BlockSpec((rows, HGRN_W), lambda i, j: (i * steps + j, cb))

    return pl.pallas_call(
        functools.partial(_hgrn_kernel, n_chunks=n_chunks),
        out_shape=jax.ShapeDtypeStruct((t, HGRN_W), BF16),
        grid=(batch, steps),
        in_specs=[
            col(COL_HQ // HGRN_W), col(COL_HF // HGRN_W), col(COL_HI // HGRN_W), col(COL_HG // HGRN_W),
            pl.BlockSpec(lb_logits.shape, lambda i, j: (0, 0)),
            pl.BlockSpec((1, HGRN_DV), lambda i, j: (0, 0)),
        ],
        out_specs=pl.BlockSpec((rows, HGRN_W), lambda i, j: (i * steps + j, 0)),
        scratch_shapes=[pltpu.VMEM((HGRN_HEADS, HGRN_DV, HGRN_DK), F32)],
        compiler_params=_params(("arbitrary", "arbitrary")),
        name="hgrn",
    )(p, p, p, p, lb_logits, gn)


def _merge_kernel(x_ref, ga_ref, gb_ref, yn_ref, yh_ref, wn_ref, wh_ref, wo_ref, o_ref):
    mixed = _sigmoid(ga_ref[...]) * _dot(yn_ref[...], wn_ref[...]) + _sigmoid(gb_ref[...]) * _dot(yh_ref[...], wh_ref[...])
    o_ref[...] = x_ref[...] + _dot(mixed.astype(BF16), wo_ref[...])


def _merge(x, p, y_nsa, y_hgrn, wn, wh, wo, tm):
    t, d = x.shape
    full = lambda a: pl.BlockSpec(a.shape, lambda i: (0, 0))
    return pl.pallas_call(
        _merge_kernel,
        out_shape=jax.ShapeDtypeStruct((t, d), F32),
        grid=(t // tm,),
        in_specs=[
            pl.BlockSpec((tm, d), lambda i: (i, 0)),
            pl.BlockSpec((tm, d), lambda i: (i, COL_GA // d)),
            pl.BlockSpec((tm, d), lambda i: (i, COL_GB // d)),
            pl.BlockSpec((tm, NSA_Q_W), lambda i: (i, 0)),
            pl.BlockSpec((tm, HGRN_W), lambda i: (i, 0)),
            full(wn), full(wh), full(wo),
        ],
        out_specs=pl.BlockSpec((tm, d), lambda i: (i, 0)),
        compiler_params=_params(("arbitrary",)),
        name="merge",
    )(x, p, p, y_nsa, y_hgrn, wn, wh, wo)


def _xattn_kernel(x_ref, g_ref, wq_ref, kv_ref, wo_ref, o_ref):
    x = x_ref[...]
    xq = _dot(_rms(x, g_ref[...]).astype(BF16), wq_ref[...]).astype(BF16)
    heads = [slice(h * XA_HEAD_DIM, (h + 1) * XA_HEAD_DIM) for h in range(XA_HEADS)]
    scores = [_dot_nt(xq[:, ls], kv_ref[0, :, ls]) * (XA_HEAD_DIM ** -0.5) for ls in heads]
    outs = []
    for h, s in enumerate(scores):
        e = jnp.exp(s - jnp.max(s, axis=-1, keepdims=True))
        p = e / jnp.sum(e, axis=-1, keepdims=True)
        outs.append(_dot(p.astype(BF16), kv_ref[0, :, XA_W + h * XA_HEAD_DIM:XA_W + (h + 1) * XA_HEAD_DIM]))
    o_x = jnp.concatenate(outs, axis=-1)
    o_ref[...] = x + _dot(o_x.astype(BF16), wo_ref[...])


def _xattn(x, g, wq, kv, wo, seq, tm):
    t, d = x.shape
    steps = seq // tm
    full = lambda a: pl.BlockSpec(a.shape, lambda i: (0, 0))
    return pl.pallas_call(
        _xattn_kernel,
        out_shape=jax.ShapeDtypeStruct((t, d), F32),
        grid=(t // tm,),
        in_specs=[
            pl.BlockSpec((tm, d), lambda i: (i, 0)),
            full(g), full(wq),
            pl.BlockSpec((1,) + kv.shape[1:], lambda i: (i // steps, 0, 0)),
            full(wo),
        ],
        out_specs=pl.BlockSpec((tm, d), lambda i: (i, 0)),
        compiler_params=_params(("arbitrary",)),
        name="xattn",
    )(x, g, wq, kv, wo)


def _router_kernel(x_ref, g_ref, w_ref, b_ref, hm_ref, idx_ref, rank_ref, wt_ref, cnt_ref):
    tm = x_ref.shape[0]
    hm = _rms(x_ref[...], g_ref[...]).astype(BF16)
    hm_ref[...] = hm
    lane = lax.broadcasted_iota(jnp.int32, (tm, LANES), 1)
    lane_f = lane.astype(F32)
    logits = _dot(hm, w_ref[...]) + b_ref[...]
    logits = jnp.where(lane < N_EXPERTS, logits, -jnp.inf)
    picks, vals = [], []
    onehot_all = jnp.zeros((tm, LANES), F32)
    for _ in range(TOP_K):
        mx = jnp.max(logits, axis=-1, keepdims=True)
        first_idx = jnp.min(jnp.where(logits == mx, lane_f, float(LANES)), axis=-1, keepdims=True)
        hit = lane_f == first_idx
        onehot = jnp.where(hit, 1.0, 0.0)
        logits = jnp.where(hit, -jnp.inf, logits)
        picks.append((first_idx, onehot))
        vals.append(mx)
        onehot_all = onehot_all + onehot
    exps = [jnp.exp(v - vals[0]) for v in vals]
    den = exps[0]
    for e in exps[1:]:
        den = den + e
    r_i = lax.broadcasted_iota(jnp.int32, (tm, tm), 0)
    c_i = lax.broadcasted_iota(jnp.int32, (tm, tm), 1)
    lower = jnp.where(c_i < r_i, 1.0, 0.0).astype(BF16)
    before = _dot(lower, onehot_all.astype(BF16))
    idx_out = jnp.zeros((tm, LANES), F32)
    rank_out = jnp.zeros((tm, LANES), F32)
    wt_out = jnp.zeros((tm, LANES), F32)
    for k in range(TOP_K):
        first_idx, onehot = picks[k]
        rank = jnp.sum(onehot * before, axis=-1, keepdims=True)
        idx_out = jnp.where(lane == k, first_idx, idx_out)
        rank_out = jnp.where(lane == k, rank, rank_out)
        wt_out = jnp.where(lane == k, exps[k] / den, wt_out)
    idx_ref[...] = idx_out.astype(jnp.int32)
    rank_ref[...] = rank_out.astype(jnp.int32)
    wt_ref[...] = wt_out
    cnt_ref[0] = jnp.sum(onehot_all, axis=0, keepdims=True)


def _router(x, g, w, b, tm):
    t, d = x.shape
    full = lambda a: pl.BlockSpec(a.shape, lambda i: (0, 0))
    lane_out = pl.BlockSpec((tm, LANES), lambda i: (i, 0))
    return pl.pallas_call(
        _router_kernel,
        out_shape=(
            jax.ShapeDtypeStruct((t, d), BF16),
            jax.ShapeDtypeStruct((t, LANES), jnp.int32),
            jax.ShapeDtypeStruct((t, LANES), jnp.int32),
            jax.ShapeDtypeStruct((t, LANES), F32),
            jax.ShapeDtypeStruct((t // tm, 1, LANES), F32),
        ),
        grid=(t // tm,),
        in_specs=[pl.BlockSpec((tm, d), lambda i: (i, 0)), full(g), full(w), full(b)],
        out_specs=(pl.BlockSpec((tm, d), lambda i: (i, 0)), lane_out, lane_out, lane_out,
                   pl.BlockSpec((1, 1, LANES), lambda i: (i, 0, 0))),
        compiler_params=_params(("arbitrary",)),
        name="router",
    )(x, g, w, b)


def _slot_matrix(idx_ref, rank_ref, offv_ref, rows, values=None):
    tt = idx_ref.shape[0]
    lane = lax.broadcasted_iota(jnp.int32, (tt, LANES), 1)
    r = lax.broadcasted_iota(jnp.int32, (tt, rows), 1)
    offv = offv_ref[0]
    idx = idx_ref[...]
    rank = rank_ref[...]
    out = jnp.zeros((tt, rows), F32)
    for k in range(TOP_K):
        seg = jnp.sum(jnp.where(lane == idx[:, k:k + 1], offv, 0.0), axis=-1, keepdims=True)
        row_k = seg.astype(jnp.int32) + rank[:, k:k + 1]
        out = jnp.where(r == row_k, 1.0 if values is None else values[:, k:k + 1], out)
    return out


def _segment_copies(src_ref, n8_ref, dst_ref, make_copy, tile_tokens):
    base = pl.program_id(0) * N_EXPERTS
    sizes = []
    size = tile_tokens
    while size >= SEG_ALIGN:
        sizes.append(size)
        size //= 2

    def visit(e, start):
        n8 = n8_ref[base + e]
        src = src_ref[base + e]
        dst = dst_ref[base + e]
        for size in sizes:
            done = n8 & (-2 * size)

            @pl.when((n8 & size) != 0)
            def _():
                cp = make_copy(pl.multiple_of(src + done, SEG_ALIGN), pl.multiple_of(dst + done, SEG_ALIGN), size)
                if start:
                    cp.start()
                else:
                    cp.wait()

    def start_all(e, c):
        visit(e, True)
        return c

    def wait_all(e, c):
        visit(e, False)
        return c

    return start_all, wait_all


def _dispatch_kernel(src_ref, n8_ref, dst_ref, hm_ref, idx_ref, rank_ref, offv_ref, xs_in_ref, xs_ref, buf_ref, sem):
    del xs_in_ref
    rows = buf_ref.shape[0]
    onehot = _slot_matrix(idx_ref, rank_ref, offv_ref, rows).astype(BF16)
    buf_ref[...] = lax.dot_general(onehot, hm_ref[...], (((0,), (0,)), ((), ())), preferred_element_type=F32)

    def make_copy(src, dst, size):
        return pltpu.make_async_copy(buf_ref.at[pl.ds(src, size)], xs_ref.at[pl.ds(dst, size)], sem)

    start_all, wait_all = _segment_copies(src_ref, n8_ref, dst_ref, make_copy, hm_ref.shape[0])
    lax.fori_loop(0, N_EXPERTS, start_all, 0)
    lax.fori_loop(0, N_EXPERTS, wait_all, 0)


def _dispatch(tables, hm, idx, rank, offv, xs_zero, tt, rows):
    t, d = hm.shape
    tile = lambda w: pl.BlockSpec((tt, w), lambda i, *_: (i, 0))
    grid_spec = pltpu.PrefetchScalarGridSpec(
        num_scalar_prefetch=3,
        grid=(t // tt,),
        in_specs=[tile(d), tile(LANES), tile(LANES), pl.BlockSpec((1, 1, LANES), lambda i, *_: (i, 0, 0)),
                  pl.BlockSpec(memory_space=pl.ANY)],
        out_specs=pl.BlockSpec(memory_space=pl.ANY),
        scratch_shapes=[pltpu.VMEM((rows, d), F32), pltpu.SemaphoreType.DMA],
    )
    return pl.pallas_call(
        _dispatch_kernel,
        out_shape=jax.ShapeDtypeStruct(xs_zero.shape, xs_zero.dtype),
        grid_spec=grid_spec,
        input_output_aliases={7: 0},
        compiler_params=_params(("arbitrary",)),
        name="dispatch",
    )(*tables, hm, idx, rank, offv, xs_zero)


def _w1_prep_kernel(w_ref, o_ref):
    grp = 2 * LANES
    r_i = lax.broadcasted_iota(jnp.int32, (grp, grp), 0)
    c_i = lax.broadcasted_iota(jnp.int32, (grp, grp), 1)
    src_col = jnp.where(c_i < LANES, 2 * c_i, 2 * (c_i - LANES) + 1)
    perm = jnp.where(r_i == src_col, 1.0, 0.0).astype(BF16)
    for c in range(w_ref.shape[1] // grp):
        sl = slice(c * grp, (c + 1) * grp)
        o_ref[:, sl] = _dot(w_ref[:, sl].astype(BF16), perm).astype(BF16)


def _w1_prep(w, tm):
    r, n = w.shape
    return pl.pallas_call(
        _w1_prep_kernel,
        out_shape=jax.ShapeDtypeStruct((r, n), BF16),
        grid=(r // tm,),
        in_specs=[pl.BlockSpec((tm, n), lambda i: (i, 0))],
        out_specs=pl.BlockSpec((tm, n), lambda i: (i, 0)),
        compiler_params=_params(("arbitrary",)),
        name="w1_prep",
    )(w)


def _ffn_kernel(te_ref, nu_ref, x_ref, w1_ref, b1_ref, w2_ref, b2_ref, o_ref):
    del te_ref
    used = pl.program_id(0) < nu_ref[0]

    @pl.when(used)
    def _():
        u = _dot(x_ref[...].astype(BF16), w1_ref[0]) + b1_ref[0]
        acts = []
        for c in range(u.shape[1] // (2 * LANES)):
            glu = jnp.minimum(u[:, 2 * c * LANES:(2 * c + 1) * LANES], SWIGLU_LIMIT)
            lin = jnp.clip(u[:, (2 * c + 1) * LANES:(2 * c + 2) * LANES], -SWIGLU_LIMIT, SWIGLU_LIMIT)
            acts.append((glu * _sigmoid(SWIGLU_ALPHA * glu) * (lin + 1.0)).astype(BF16))
        o_ref[...] = _dot(jnp.concatenate(acts, axis=-1), w2_ref[0]) + b2_ref[0]

    @pl.when(jnp.logical_not(used))
    def _():
        o_ref[...] = jnp.zeros_like(o_ref)


def _ffn(tile_expert, n_used, xs, w1, b1, w2, b2, tm):
    n_pad, d = xs.shape
    f2 = w1.shape[2]
    f = w2.shape[1]
    grid_spec = pltpu.PrefetchScalarGridSpec(
        num_scalar_prefetch=2,
        grid=(n_pad // tm,),
        in_specs=[
            pl.BlockSpec((tm, d), lambda i, te, nu: (i, 0)),
            pl.BlockSpec((1, d, f2), lambda i, te, nu: (te[i], 0, 0)),
            pl.BlockSpec((1, 1, f2), lambda i, te, nu: (te[i], 0, 0)),
            pl.BlockSpec((1, f, d), lambda i, te, nu: (te[i], 0, 0)),
            pl.BlockSpec((1, 1, d), lambda i, te, nu: (te[i], 0, 0)),
        ],
        out_specs=pl.BlockSpec((tm, d), lambda i, te, nu: (i, 0)),
    )
    return pl.pallas_call(
        _ffn_kernel,
        out_shape=jax.ShapeDtypeStruct((n_pad, d), F32),
        grid_spec=grid_spec,
        compiler_params=_params(("arbitrary",)),
        name="expert_ffn",
    )(tile_expert, n_used, xs, w1, b1, w2, b2)


def _combine_kernel(src_ref, n8_ref, dst_ref, ys_ref, idx_ref, rank_ref, wt_ref, offv_ref, x_ref, g_ref, o_ref,
                    buf_ref, sem):
    rows = buf_ref.shape[0]

    @pl.when(pl.program_id(0) == 0)
    def _():
        buf_ref[...] = jnp.zeros_like(buf_ref)

    def make_copy(src, dst, size):
        return pltpu.make_async_copy(ys_ref.at[pl.ds(dst, size)], buf_ref.at[pl.ds(src, size)], sem)

    start_all, wait_all = _segment_copies(src_ref, n8_ref, dst_ref, make_copy, x_ref.shape[0])
    lax.fori_loop(0, N_EXPERTS, start_all, 0)
    pw = _slot_matrix(idx_ref, rank_ref, offv_ref, rows, values=wt_ref[...])
    p_hi = pw.astype(BF16)
    p_lo = (pw - p_hi.astype(F32)).astype(BF16)
    lax.fori_loop(0, N_EXPERTS, wait_all, 0)
    ys = buf_ref[...]
    y_hi = ys.astype(BF16)
    y_lo = (ys - y_hi.astype(F32)).astype(BF16)
    y = _dot(p_hi, y_hi) + _dot(p_hi, y_lo) + _dot(p_lo, y_hi)
    o_ref[...] = _rms(x_ref[...] + y, g_ref[...])


def _combine(tables, ys, idx, rank, wt, offv, x, g, tt, rows):
    t, d = x.shape
    tile = lambda w: pl.BlockSpec((tt, w), lambda i, *_: (i, 0))
    grid_spec = pltpu.PrefetchScalarGridSpec(
        num_scalar_prefetch=3,
        grid=(t // tt,),
        in_specs=[pl.BlockSpec(memory_space=pl.ANY), tile(LANES), tile(LANES), tile(LANES),
                  pl.BlockSpec((1, 1, LANES), lambda i, *_: (i, 0, 0)), tile(d),
                  pl.BlockSpec((1, d), lambda i, *_: (0, 0))],
        out_specs=tile(d),
        scratch_shapes=[pltpu.VMEM((rows, d), F32), pltpu.SemaphoreType.DMA],
    )
    return pl.pallas_call(
        _combine_kernel,
        out_shape=jax.ShapeDtypeStruct((t, d), F32),
        grid_spec=grid_spec,
        compiler_params=_params(("arbitrary",)),
        name="combine",
    )(*tables, ys, idx, rank, wt, offv, x, g)


def _tile_sizes(seq):
    return dict(
        tm_proj=1024, tn_proj=P_WIDTH // 4,
        tm_rope=512,
        tq=128, tk=512,
        hgrn_chunks=4,
        tm_merge=512, tm_xattn=512,
        tm_router=512,
        tm_w1_prep=512,
        tm_ffn=512,
    )


def _layer(x, mem, positions, ts, mix_norm_g, w_in, cmp_pe, cmp_w1, cmp_b1, cmp_w2, cmp_b2, lb_logits, hgrn_norm_g,
           w_up_nsa, w_up_hgrn, w_out, xa_norm_g, xa_mem_norm_g, w_xq, w_xkv, w_xo, moe_norm_g, router_w, router_b,
           moe_w1, moe_b1, moe_w2, moe_b2, out_norm_g):
    b, s, d = x.shape
    t = b * s
    g, hg, dh = NSA_KV_GROUPS, NSA_Q_PER_GROUP, NSA_HEAD_DIM
    x2 = x.reshape(t, d)
    row = lambda v: v.reshape(1, -1).astype(F32)

    splits = [0]
    for w in (d, d, NSA_Q_W) + (NSA_KV_W,) * 6 + (3 * NSA_HEADS,) + (HGRN_W,) * 4:
        splits.append(splits[-1] + w)
    seg = lambda i: w_in[:, splits[i]:splits[i + 1]]
    (ga, gb, nq, kc, vc, ks, vs, kw, vw, ng, hq, hf, hi, hgate) = [seg(i) for i in range(14)]
    pad = jnp.zeros((d, P_WIDTH - COL_NG - 3 * NSA_HEADS), w_in.dtype)
    w_p = jnp.concatenate([ga, gb, hq, hf, hi, hgate, nq, ks, kw, kc, vc, vs, vw, ng, pad], axis=1).astype(BF16)

    p = _norm_matmul(x2, row(mix_norm_g), w_p, F32, ts["tm_proj"], ts["tn_proj"], "in_proj")

    half = dh // 2
    inv_freq = ROPE_THETA ** (-jnp.arange(half, dtype=F32) / half)
    invf = jnp.tile(inv_freq, LANES // half).reshape(1, LANES)
    q_r, kk_r = _rope(p, positions.reshape(t, 1), invf, ts["tm_rope"])
    tq = ts["tq"]
    nq = s // tq
    qt = q_r.reshape(b, nq, tq, g, hg, dh).transpose(0, 3, 1, 5, 4, 2).reshape(b, g, nq, dh, hg * tq)
    kk = kk_r.reshape(b, s, 2 * g, dh).transpose(0, 2, 1, 3)
    k_slc, k_win = kk[:, :g], kk[:, g:]
    nb = s // SLC_BLOCK
    block_onehot = (jnp.arange(s)[:, None] // SLC_BLOCK == jnp.arange(nb)[None, :]).astype(BF16)
    ks_aug = jnp.concatenate([k_slc, jnp.broadcast_to(block_onehot, (b, g, s, nb))], axis=-1)

    def values_t(v):
        vt = v.transpose(0, 1, 3, 2).astype(BF16)
        ones = jnp.ones(vt.shape[:2] + (1, vt.shape[3]), BF16)
        zeros = jnp.zeros(vt.shape[:2] + (dh - 1, vt.shape[3]), BF16)
        return jnp.concatenate([vt, ones, zeros], axis=2)

    vvt = values_t(p[:, COL_VSVW:COL_VSVW + 2 * NSA_KV_W].reshape(b, s, 2 * g, dh).transpose(0, 2, 1, 3))
    vt_slc, vt_win = vvt[:, :g], vvt[:, g:]
    gates = p[:, COL_NG:COL_NG + 3 * NSA_HEADS].reshape(b, nq, tq, g, hg, 3).transpose(0, 3, 1, 5, 4, 2)
    gates = gates.reshape(b, g, nq, 3, hg * tq)

    nr = s // CMP_STRIDE
    kcvc = p[:, COL_KCVC:COL_KCVC + 2 * NSA_KV_W].reshape(b, s, 2, g, dh).transpose(2, 0, 3, 1, 4)
    r = kcvc.reshape(2, b, g, nr, CMP_STRIDE * dh)
    pe = cmp_pe.reshape(2, 2, 1, CMP_STRIDE * dh)
    zeros_w2 = jnp.zeros_like(cmp_w2)
    w2p = jnp.stack([jnp.concatenate([cmp_w2, zeros_w2], axis=-1),
                     jnp.concatenate([zeros_w2, cmp_w2], axis=-1)], axis=1).astype(BF16)
    b2t = jnp.tile(cmp_b2, (1, g)).reshape(2, 1, LANES)
    pos_cmp = positions[:, CMP_BLOCK - 1::CMP_STRIDE]
    pos_cmp = jnp.pad(pos_cmp, ((0, 0), (0, nr - pos_cmp.shape[1]))).reshape(b, nr, 1)
    cmp = _compress(r, pe, cmp_w1.astype(BF16), cmp_b1.reshape(2, 1, CMP_HIDDEN), w2p, b2t, pos_cmp, invf)
    cmp = cmp.reshape(2, b, nr, g, dh).transpose(0, 1, 3, 2, 4)

    y_nsa = _nsa(qt, cmp[0], values_t(cmp[1]), ks_aug, vt_slc, k_win, vt_win, gates, tq, ts["tk"])
    y_nsa = y_nsa.reshape(b, g, nq, dh, hg, tq).transpose(0, 2, 5, 1, 4, 3).reshape(t, NSA_Q_W)

    y_hgrn = _hgrn(p, lb_logits.astype(F32), row(hgrn_norm_g), b, s, ts["hgrn_chunks"])

    x2 = _merge(x2, p, y_nsa, y_hgrn, w_up_nsa.astype(BF16), w_up_hgrn.astype(BF16), w_out.astype(BF16),
                ts["tm_merge"])

    n_mem = mem.shape[1]
    kv = _norm_matmul(mem.reshape(b * n_mem, d), row(xa_mem_norm_g), w_xkv.astype(BF16), BF16,
                      n_mem, 2 * XA_W, "mem_kv").reshape(b, n_mem, 2 * XA_W)
    x2 = _xattn(x2, row(xa_norm_g), w_xq.astype(BF16), kv, w_xo.astype(BF16), s, ts["tm_xattn"])

    n_exp = router_w.shape[1]
    rw = jnp.pad(router_w, ((0, 0), (0, LANES - n_exp))).astype(BF16)
    rb = jnp.pad(router_b, (0, LANES - n_exp)).reshape(1, LANES).astype(F32)
    assert n_exp == N_EXPERTS
    tt = ts["tm_router"]
    nt = t // tt
    hm, idx, rank, wt, cnt = _router(x2, row(moe_norm_g), rw, rb, tt)
    tm = ts["tm_ffn"]
    n8 = (cnt[:, 0, :n_exp].astype(jnp.int32) + SEG_ALIGN - 1) // SEG_ALIGN * SEG_ALIGN
    src_off = jnp.cumsum(n8, axis=1) - n8
    padded = (jnp.sum(n8, axis=0) + tm - 1) // tm * tm
    ends = jnp.cumsum(padded)
    dst_off = (ends - padded)[None, :] + jnp.cumsum(n8, axis=0) - n8
    tables = (src_off.reshape(-1), n8.reshape(-1), dst_off.reshape(-1))
    offv = jnp.pad(src_off.astype(F32), ((0, 0), (0, LANES - n_exp))).reshape(nt, 1, LANES)
    rows = tt * TOP_K + n_exp * SEG_ALIGN
    n_pad = (t * TOP_K + nt * n_exp * SEG_ALIGN + n_exp * tm + tm - 1) // tm * tm
    n_tiles = n_pad // tm
    tile_ids = jnp.arange(n_tiles, dtype=jnp.int32)
    tile_expert = jnp.sum(((ends // tm)[None, :] <= tile_ids[:, None]).astype(jnp.int32), axis=1)
    tile_expert = jnp.minimum(tile_expert, n_exp - 1)
    n_used = (ends[-1] // tm).reshape(1).astype(jnp.int32)

    xs = _dispatch(tables, hm, idx, rank, offv, jnp.zeros((n_pad, d), F32), tt, rows)
    f = moe_w2.shape[1]
    w1p = _w1_prep(moe_w1.reshape(n_exp * d, 2 * f), ts["tm_w1_prep"]).reshape(n_exp, d, 2 * f)
    b1p = moe_b1.reshape(n_exp, f // LANES, LANES, 2).transpose(0, 1, 3, 2).reshape(n_exp, 1, 2 * f)
    ys = _ffn(tile_expert, n_used, xs, w1p, b1p, moe_w2.astype(BF16), moe_b2.reshape(n_exp, 1, d), tm)
    out = _combine(tables, ys, idx, rank, wt, offv, x2, row(out_norm_g), tt, rows)
    return out.reshape(b, s, d)


def kernel(x, mem, positions, mix_norm_g, w_in, cmp_pe, cmp_w1, cmp_b1, cmp_w2, cmp_b2, hgrn_lb_logits, hgrn_norm_g, w_up_nsa, w_up_hgrn, w_out, xa_norm_g, xa_mem_norm_g, w_xq, w_xkv, w_xo, moe_norm_g, router_w, router_b, moe_w1, moe_b1, moe_w2, moe_b2, final_norm_g):
    depth = w_in.shape[0]
    assert depth == 1, "single-layer block: the final norm is fused into the last layer's combine"
    ts = _tile_sizes(x.shape[1])
    l = 0
    return _layer(x, mem, positions, ts, mix_norm_g[l], w_in[l], cmp_pe[l], cmp_w1[l], cmp_b1[l], cmp_w2[l], cmp_b2[l],
                  hgrn_lb_logits, hgrn_norm_g[l], w_up_nsa[l], w_up_hgrn[l], w_out[l], xa_norm_g[l], xa_mem_norm_g[l],
                  w_xq[l], w_xkv[l], w_xo[l], moe_norm_g[l], router_w[l], router_b[l], moe_w1[l], moe_b1[l], moe_w2[l],
                  moe_b2[l], final_norm_g)
```

```python
import functools

import jax
import jax.numpy as jnp
from jax import lax
from jax.experimental import pallas as pl
from jax.experimental.pallas import tpu as pltpu

EPS = 1e-6
ROPE_THETA = 10000.0
NEG_INF = -1e30
FORCE_SCORE = 1e9

NSA_HEADS = 8
NSA_KV_GROUPS = 2
NSA_Q_PER_GROUP = NSA_HEADS // NSA_KV_GROUPS
NSA_HEAD_DIM = 64
CMP_BLOCK = 32
CMP_STRIDE = 16
CMP_HIDDEN = 256
SLC_BLOCK = 64
SLC_TOPK = 16
N_LOCAL_BLOCKS = 2
WINDOW = 512
NSA_Q_W = NSA_HEADS * NSA_HEAD_DIM
NSA_KV_W = NSA_KV_GROUPS * NSA_HEAD_DIM

HGRN_HEADS = 4
HGRN_DK = 128
HGRN_DV = 128
HGRN_CHUNK = 64
HGRN_SUB = 16
HGRN_W = HGRN_HEADS * HGRN_DK

XA_HEADS = 4
XA_HEAD_DIM = 128
XA_W = XA_HEADS * XA_HEAD_DIM

N_EXPERTS = 32
TOP_K = 4
SWIGLU_ALPHA = 1.702
SWIGLU_LIMIT = 7.0

LANES = 128
SEG_ALIGN = 8
LOG2E = 1.4426950408889634
VMEM_LIMIT = 48 * 1024 * 1024

COL_GA = 0
COL_GB = 1024
COL_HQ = 2048
COL_HF = 2560
COL_HI = 3072
COL_HG = 3584
COL_NQ = 4096
COL_KSKW = 4608
COL_KCVC = 4864
COL_VSVW = 5120
COL_NG = 5376
P_WIDTH = 5632

F32 = jnp.float32
BF16 = jnp.bfloat16


def _params(sem):
    return pltpu.CompilerParams(dimension_semantics=sem, vmem_limit_bytes=VMEM_LIMIT)


def _dot(a, b):
    return jnp.dot(a, b, preferred_element_type=F32)


def _dot_nt(a, b):
    return lax.dot_general(a, b, (((1,), (1,)), ((), ())), preferred_element_type=F32)


def _rms(xf, g):
    return xf * lax.rsqrt(jnp.mean(xf * xf, axis=-1, keepdims=True) + EPS) * g


def _sigmoid(x):
    return 1.0 / (1.0 + jnp.exp(-x))


def _norm_matmul_kernel(x_ref, g_ref, w_ref, o_ref, hn_ref):
    @pl.when(pl.program_id(1) == 0)
    def _():
        hn_ref[...] = _rms(x_ref[...], g_ref[...]).astype(BF16)

    o_ref[...] = _dot(hn_ref[...], w_ref[...]).astype(o_ref.dtype)


def _norm_matmul(x, g, w, out_dtype, tm, tn, name):
    t, d = x.shape
    n = w.shape[1]
    return pl.pallas_call(
        _norm_matmul_kernel,
        out_shape=jax.ShapeDtypeStruct((t, n), out_dtype),
        grid=(t // tm, n // tn),
        in_specs=[
            pl.BlockSpec((tm, d), lambda i, j: (i, 0)),
            pl.BlockSpec((1, d), lambda i, j: (0, 0)),
            pl.BlockSpec((d, tn), lambda i, j: (0, j)),
        ],
        out_specs=pl.BlockSpec((tm, tn), lambda i, j: (i, j)),
        scratch_shapes=[pltpu.VMEM((tm, d), BF16)],
        compiler_params=_params(("arbitrary", "arbitrary")),
        name=name,
    )(x, g, w)


def _rope_coeffs(pos_col, invf):
    ang = pos_col.astype(F32) * invf
    lane = lax.broadcasted_iota(jnp.int32, ang.shape, 1)
    first = (lane & (NSA_HEAD_DIM - 1)) < (NSA_HEAD_DIM // 2)
    c = jnp.cos(ang)
    s = jnp.sin(ang)
    return c, jnp.where(first, -s, s), first


def _rope_tile(x, c, s_signed, first):
    half = NSA_HEAD_DIM // 2
    partner = jnp.where(first, pltpu.roll(x, LANES - half, 1), pltpu.roll(x, half, 1))
    return x * c + partner * s_signed


def _rope_kernel(q_ref, k_ref, pos_ref, invf_ref, qo_ref, ko_ref, *, q_scale):
    c, s_signed, first = _rope_coeffs(pos_ref[...], invf_ref[...])
    for i in range(q_ref.shape[1] // LANES):
        sl = slice(i * LANES, (i + 1) * LANES)
        qo_ref[:, sl] = (_rope_tile(q_ref[:, sl], c, s_signed, first) * q_scale).astype(BF16)
    for i in range(k_ref.shape[1] // LANES):
        sl = slice(i * LANES, (i + 1) * LANES)
        ko_ref[:, sl] = _rope_tile(k_ref[:, sl], c, s_signed, first).astype(BF16)


def _rope(p, pos_col, invf, tm):
    t = p.shape[0]
    kw = 2 * NSA_KV_W
    return pl.pallas_call(
        functools.partial(_rope_kernel, q_scale=NSA_HEAD_DIM ** -0.5 * LOG2E),
        out_shape=(jax.ShapeDtypeStruct((t, NSA_Q_W), BF16), jax.ShapeDtypeStruct((t, kw), BF16)),
        grid=(t // tm,),
        in_specs=[
            pl.BlockSpec((tm, NSA_Q_W), lambda i: (i, COL_NQ // NSA_Q_W)),
            pl.BlockSpec((tm, kw), lambda i: (i, COL_KSKW // kw)),
            pl.BlockSpec((tm, 1), lambda i: (i, 0)),
            pl.BlockSpec((1, LANES), lambda i: (0, 0)),
        ],
        out_specs=(
            pl.BlockSpec((tm, NSA_Q_W), lambda i: (i, 0)),
            pl.BlockSpec((tm, kw), lambda i: (i, 0)),
        ),
        compiler_params=_params(("arbitrary",)),
        name="rope",
    )(p, p, pos_col, invf)


def _gelu_tanh(x):
    return 0.5 * x * (1.0 + jnp.tanh(0.7978845608028654 * (x + 0.044715 * (x * x * x))))


def _compress_kernel(r_ref, pe_ref, w1_ref, b1_ref, w2_ref, b2_ref, pos_ref, invf_ref, o_ref):
    nr = r_ref.shape[3]
    half = r_ref.shape[4]
    acc = None
    for g in range(NSA_KV_GROUPS):
        r = r_ref[0, 0, g]
        top = _dot((r + pe_ref[0, 0]).astype(BF16), w1_ref[0, :half, :])
        bot = _dot((r + pe_ref[0, 1]).astype(BF16), w1_ref[0, half:, :])
        pre = top + pltpu.roll(bot, nr - 1, 0) + b1_ref[0]
        part = _dot(_gelu_tanh(pre).astype(BF16), w2_ref[0, g])
        acc = part if acc is None else acc + part
    out = acc + b2_ref[0]
    c, s_signed, first = _rope_coeffs(pos_ref[0], invf_ref[...])
    roped = _rope_tile(out, c, s_signed, first)
    is_key = pl.program_id(0) == 0
    o_ref[0, 0] = jnp.where(is_key, roped, out).astype(BF16)


def _compress(r, pe, w1, b1, w2p, b2t, pos_cmp, invf):
    _, b, g, nr, half = r.shape
    return pl.pallas_call(
        _compress_kernel,
        out_shape=jax.ShapeDtypeStruct((2, b, nr, LANES), BF16),
        grid=(2, b),
        in_specs=[
            pl.BlockSpec((1, 1, g, nr, half), lambda k, i: (k, i, 0, 0, 0)),
            pl.BlockSpec((1, 2, 1, half), lambda k, i: (k, 0, 0, 0)),
            pl.BlockSpec((1, 2 * half, CMP_HIDDEN), lambda k, i: (k, 0, 0)),
            pl.BlockSpec((1, 1, CMP_HIDDEN), lambda k, i: (k, 0, 0)),
            pl.BlockSpec((1, g, CMP_HIDDEN, LANES), lambda k, i: (k, 0, 0, 0)),
            pl.BlockSpec((1, 1, LANES), lambda k, i: (k, 0, 0)),
            pl.BlockSpec((1, nr, 1), lambda k, i: (i, 0, 0)),
            pl.BlockSpec((1, LANES), lambda k, i: (0, 0)),
        ],
        out_specs=pl.BlockSpec((1, 1, nr, LANES), lambda k, i: (k, i, 0, 0)),
        compiler_params=_params(("arbitrary", "arbitrary")),
        name="compress",
    )(r, pe, w1, b1, w2p, b2t, pos_cmp, invf)


def _nsa_kernel(qt_ref, kc_ref, vct_ref, ksa_ref, vst_ref, kw_ref, vwt_ref, g_ref, o_ref, acc_ref, out_ref,
                sa_ref, sb_ref, qa_ref, *, tq, tk, seq):
    hg = NSA_Q_PER_GROUP
    dh = NSA_HEAD_DIM
    nc = kc_ref.shape[2]
    nb = seq // SLC_BLOCK
    top_k = min(SLC_TOPK, nb)
    s0 = pl.program_id(2) * tq
    t_lane = s0 + lax.broadcasted_iota(jnp.int32, (1, tq), 1)
    gate = _sigmoid(g_ref[0, 0, 0])

    def scores(k_tile, bias):
        s = _dot(k_tile, qt_ref[0, 0, 0])
        return jnp.concatenate([s[:, h * tq:(h + 1) * tq] + bias for h in range(hg)], axis=1)

    def normalised(acc):
        return acc[:dh] / acc[dh:dh + 1]

    n_col = lax.broadcasted_iota(jnp.int32, (nc, 1), 0)
    valid_c = (n_col * CMP_STRIDE + (CMP_BLOCK - 1) <= t_lane) & (n_col < nc - 1)
    win_keys = WINDOW + tq
    w0 = pl.multiple_of(jnp.maximum(s0 - WINDOW, 0), tq)
    wpos = w0 + lax.broadcasted_iota(jnp.int32, (win_keys, 1), 0)
    bias_w = jnp.where((wpos <= t_lane) & (wpos > t_lane - WINDOW), 0.0, NEG_INF)
    s_c = scores(kc_ref[0, 0], jnp.where(valid_c, 0.0, NEG_INF))
    s_w = scores(kw_ref[0, 0, pl.ds(w0, win_keys), :], bias_w)

    e_c = jnp.exp2(s_c - jnp.max(s_c, axis=0, keepdims=True))
    t_all = s0 + (lax.broadcasted_iota(jnp.int32, (1, hg * tq), 1) & (tq - 1))
    row_ok = t_all >= CMP_BLOCK - 1
    pn = e_c * jnp.where(row_ok, 1.0 / jnp.sum(e_c, axis=0, keepdims=True), 0.0)
    out_ref[...] = gate[0:1] * _dot(vct_ref[0, 0], pn.astype(BF16))[:dh]
    p_sum = pn[:, 0:tq]
    for h in range(1, hg):
        p_sum = p_sum + pn[:, h * tq:(h + 1) * tq]

    j_col = lax.broadcasted_iota(jnp.int32, (nb, 1), 0)
    n_row = lax.broadcasted_iota(jnp.int32, (1, nc), 1) * CMP_STRIDE
    overlap = (n_row < j_col * SLC_BLOCK + SLC_BLOCK) & (n_row + CMP_BLOCK > j_col * SLC_BLOCK)
    overlap = jnp.where(overlap, 1.0, 0.0).astype(BF16)
    p_hi = p_sum.astype(BF16)
    p_lo = (p_sum - p_hi.astype(F32)).astype(BF16)
    imp = _dot(overlap, p_hi) + _dot(overlap, p_lo)

    p_w = jnp.exp2(s_w - jnp.max(s_w, axis=0, keepdims=True)).astype(BF16)
    out_ref[...] += gate[2:3] * normalised(_dot(vwt_ref[0, 0, :, pl.ds(w0, win_keys)], p_w))

    cur = t_lane >> 6
    causal_b = j_col <= cur
    forced = (j_col == 0) | (causal_b & (j_col > cur - N_LOCAL_BLOCKS))
    score = jnp.where(forced, FORCE_SCORE, jnp.where(causal_b, imp, -1.0))
    j_f = jnp.broadcast_to(j_col.astype(F32), (nb, tq))
    sel = jnp.zeros((nb, tq), F32)
    for _ in range(top_k):
        mx = jnp.max(score, axis=0, keepdims=True)
        first_idx = jnp.min(jnp.where(score == mx, j_f, float(nb)), axis=0, keepdims=True)
        hit = j_f == first_idx
        sel = jnp.where(hit, 1.0, sel)
        score = jnp.where(hit, -jnp.inf, score)
    sel = jnp.where(causal_b, sel, 0.0)

    sel_bias = ((sel - 1.0) * (-NEG_INF)).astype(BF16)
    qa_ref[:dh] = qt_ref[0, 0, 0]
    qa_ref[dh:] = jnp.concatenate([sel_bias] * hg, axis=1)
    k_col = lax.broadcasted_iota(jnp.int32, (tk, 1), 0)

    def put_scores(k0, buf_ref):
        buf_ref[...] = _dot(ksa_ref[0, 0, pl.ds(pl.multiple_of(k0, tk), tk), :], qa_ref[...])

    def consume(k0, buf_ref, m_old, causal):
        s = buf_ref[...]
        if causal:
            bias = jnp.where(k0 + k_col <= t_lane, 0.0, NEG_INF)
            s = jnp.concatenate([s[:, h * tq:(h + 1) * tq] + bias for h in range(hg)], axis=1)
        m_new = jnp.maximum(m_old, jnp.max(s, axis=0, keepdims=True))
        pv = _dot(vst_ref[0, 0, :, pl.ds(pl.multiple_of(k0, tk), tk)], jnp.exp2(s - m_new).astype(BF16))
        acc_ref[...] = jnp.exp2(m_old - m_new) * acc_ref[...] + pv
        return m_new

    def slc_pair(it, m):
        k0 = it * (2 * tk)
        put_scores(k0 + tk, sb_ref)
        m = consume(k0, sa_ref, m, False)
        put_scores(k0 + 2 * tk, sa_ref)
        return consume(k0 + tk, sb_ref, m, False)

    acc_ref[...] = jnp.zeros_like(acc_ref)
    put_scores(jnp.int32(0), sa_ref)
    d_tile = s0 // tk
    n_full = d_tile // 2
    m_s = lax.fori_loop(0, n_full, slc_pair, jnp.full((1, hg * tq), NEG_INF, F32))
    e0 = n_full * (2 * tk)

    @pl.when(d_tile % 2 == 1)
    def _():
        put_scores(e0 + tk, sb_ref)
        consume(e0 + tk, sb_ref, consume(e0, sa_ref, m_s, False), True)

    @pl.when(d_tile % 2 == 0)
    def _():
        consume(e0, sa_ref, m_s, True)

    o_ref[0, 0, 0] = (out_ref[...] + gate[1:2] * normalised(acc_ref[...])).astype(o_ref.dtype)


def _nsa(qt, kc, vct, ksa, vst, kw, vwt, gates, tq, tk):
    b, g, nq, dh, lanes = qt.shape
    hg = lanes // tq
    s = nq * tq
    nr = kc.shape[2]
    k_spec = pl.BlockSpec((1, 1, s, dh), lambda i, j, k: (i, j, 0, 0))
    vt_spec = pl.BlockSpec((1, 1, 2 * dh, s), lambda i, j, k: (i, j, 0, 0))
    return pl.pallas_call(
        functools.partial(_nsa_kernel, tq=tq, tk=tk, seq=s),
        out_shape=jax.ShapeDtypeStruct((b, g, nq, dh, hg * tq), BF16),
        grid=(b, g, nq),
        in_specs=[
            pl.BlockSpec((1, 1, 1, dh, hg * tq), lambda i, j, k: (i, j, k, 0, 0)),
            pl.BlockSpec((1, 1, nr, dh), lambda i, j, k: (i, j, 0, 0)),
            pl.BlockSpec((1, 1, 2 * dh, nr), lambda i, j, k: (i, j, 0, 0)),
            pl.BlockSpec((1, 1, s, ksa.shape[3]), lambda i, j, k: (i, j, 0, 0)), vt_spec, k_spec, vt_spec,
            pl.BlockSpec((1, 1, 1, 3, hg * tq), lambda i, j, k: (i, j, k, 0, 0)),
        ],
        out_specs=pl.BlockSpec((1, 1, 1, dh, hg * tq), lambda i, j, k: (i, j, k, 0, 0)),
        scratch_shapes=[pltpu.VMEM((2 * dh, hg * tq), F32), pltpu.VMEM((dh, hg * tq), F32),
                        pltpu.VMEM((tk, hg * tq), F32), pltpu.VMEM((tk, hg * tq), F32),
                        pltpu.VMEM((ksa.shape[3], hg * tq), BF16)],
        compiler_params=_params(("arbitrary", "arbitrary", "arbitrary")),
        name="nsa",
    )(qt, kc, vct, ksa, vst, kw, vwt, gates)


def _cumsum_rows(x):
    n = x.shape[0]
    row = lax.broadcasted_iota(jnp.int32, x.shape, 0)
    d = 1
    while d < n:
        x = x + jnp.where(row >= d, pltpu.roll(x, d, 0), 0.0)
        d *= 2
    return x


def _hgrn_kernel(q_ref, f_ref, i_ref, g_ref, lb_ref, gn_ref, o_ref, st_ref, *, n_chunks):
    @pl.when(pl.program_id(1) == 0)
    def _():
        st_ref[...] = jnp.zeros_like(st_ref)

    c_len = HGRN_CHUNK
    sub = HGRN_SUB
    lbl = lb_ref[...]
    lb_e = jnp.exp(lbl - jnp.max(lbl, axis=0, keepdims=True))
    lb_all = lb_e[0:1] / jnp.sum(lb_e, axis=0, keepdims=True)

    items = [(c, h) for c in range(n_chunks) for h in range(HGRN_HEADS)]
    wave1 = {}
    for c, h in items:
        rs = slice(c * c_len, (c + 1) * c_len)
        ls = slice(h * HGRN_DK, (h + 1) * HGRN_DK)
        lb = lb_all[:, ls]
        f = lb + (1.0 - lb) * _sigmoid(f_ref[rs, ls])
        k = 1.0 - f
        b = _cumsum_rows(jnp.log(f))
        q = q_ref[rs, ls]
        v32 = i_ref[rs, ls]
        b_end = b[c_len - 1:c_len]
        attn = []
        for blk in range(c_len // sub):
            lo, hi = blk * sub, (blk + 1) * sub
            beta = b[lo - 1:lo] if blk > 0 else jnp.zeros_like(b_end)
            qd = (q[lo:hi] * jnp.exp(b[lo:hi] - beta)).astype(BF16)
            kd = (k[:hi] * jnp.exp(beta - b[:hi])).astype(BF16)
            attn.append(_dot_nt(qd, kd))
        update = _dot(v32.T.astype(BF16), (k * jnp.exp(b_end - b)).astype(BF16))
        wave1[c, h] = ((q * jnp.exp(b)).astype(BF16), jnp.exp(b_end), update, attn)

    o_inter = {}
    for h in range(HGRN_HEADS):
        st = st_ref[h]
        for c in range(n_chunks):
            q_dec, decay, update, _ = wave1[c, h]
            o_inter[c, h] = _dot_nt(q_dec, st.astype(BF16))
            st = st * decay + update
        st_ref[h] = st

    for c, h in items:
        rs = slice(c * c_len, (c + 1) * c_len)
        ls = slice(h * HGRN_DK, (h + 1) * HGRN_DK)
        v = i_ref[rs, ls].astype(BF16)
        pieces = []
        for blk, a in enumerate(wave1[c, h][3]):
            lo, hi = blk * sub, (blk + 1) * sub
            ti = lax.broadcasted_iota(jnp.int32, (sub, hi), 0)
            si = lax.broadcasted_iota(jnp.int32, (sub, hi), 1)
            pieces.append(_dot(jnp.where(si <= ti + lo, a, 0.0).astype(BF16), v[:hi]))
        o = o_inter[c, h] + jnp.concatenate(pieces, axis=0)
        gate = g_ref[rs, ls]
        o_ref[rs, ls] = (_rms(o, gn_ref[...]) * (gate * _sigmoid(gate))).astype(o_ref.dtype)


def _hgrn(p, lb_logits, gn, batch, seq, n_chunks):
    t = p.shape[0]
    rows = n_chunks * HGRN_CHUNK
    steps = seq // rows

    def col(cb):
        return pl.BlockSpec((rows, HGRN_W), lambda i, j: (i * steps + j, cb))

    return pl.pallas_call(
        functools.partial(_hgrn_kernel, n_chunks=n_chunks),
        out_shape=jax.ShapeDtypeStruct((t, HGRN_W), BF16),
        grid=(batch, steps),
        in_specs=[
            col(COL_HQ // HGRN_W), col(COL_HF // HGRN_W), col(COL_HI // HGRN_W), col(COL_HG // HGRN_W),
            pl.BlockSpec(lb_logits.shape, lambda i, j: (0, 0)),
            pl.BlockSpec((1, HGRN_DV), lambda i, j: (0, 0)),
        ],
        out_specs=pl.BlockSpec((rows, HGRN_W), lambda i, j: (i * steps + j, 0)),
        scratch_shapes=[pltpu.VMEM((HGRN_HEADS, HGRN_DV, HGRN_DK), F32)],
        compiler_params=_params(("arbitrary", "arbitrary")),
        name="hgrn",
    )(p, p, p, p, lb_logits, gn)


def _merge_kernel(x_ref, ga_ref, gb_ref, yn_ref, yh_ref, wn_ref, wh_ref, wo_ref, o_ref):
    mixed = _sigmoid(ga_ref[...]) * _dot(yn_ref[...], wn_ref[...]) + _sigmoid(gb_ref[...]) * _dot(yh_ref[...], wh_ref[...])
    o_ref[...] = x_ref[...] + _dot(mixed.astype(BF16), wo_ref[...])


def _merge(x, p, y_nsa, y_hgrn, wn, wh, wo, tm):
    t, d = x.shape
    full = lambda a: pl.BlockSpec(a.shape, lambda i: (0, 0))
    return pl.pallas_call(
        _merge_kernel,
        out_shape=jax.ShapeDtypeStruct((t, d), F32),
        grid=(t // tm,),
        in_specs=[
            pl.BlockSpec((tm, d), lambda i: (i, 0)),
            pl.BlockSpec((tm, d), lambda i: (i, COL_GA // d)),
            pl.BlockSpec((tm, d), lambda i: (i, COL_GB // d)),
            pl.BlockSpec((tm, NSA_Q_W), lambda i: (i, 0)),
            pl.BlockSpec((tm, HGRN_W), lambda i: (i, 0)),
            full(wn), full(wh), full(wo),
        ],
        out_specs=pl.BlockSpec((tm, d), lambda i: (i, 0)),
        compiler_params=_params(("arbitrary",)),
        name="merge",
    )(x, p, p, y_nsa, y_hgrn, wn, wh, wo)


def _xattn_kernel(x_ref, g_ref, wq_ref, kv_ref, wo_ref, o_ref):
    x = x_ref[...]
    xq = _dot(_rms(x, g_ref[...]).astype(BF16), wq_ref[...]).astype(BF16)
    heads = [slice(h * XA_HEAD_DIM, (h + 1) * XA_HEAD_DIM) for h in range(XA_HEADS)]
    scores = [_dot_nt(xq[:, ls], kv_ref[0, :, ls]) * (XA_HEAD_DIM ** -0.5) for ls in heads]
    outs = []
    for h, s in enumerate(scores):
        e = jnp.exp(s - jnp.max(s, axis=-1, keepdims=True))
        p = e / jnp.sum(e, axis=-1, keepdims=True)
        outs.append(_dot(p.astype(BF16), kv_ref[0, :, XA_W + h * XA_HEAD_DIM:XA_W + (h + 1) * XA_HEAD_DIM]))
    o_x = jnp.concatenate(outs, axis=-1)
    o_ref[...] = x + _dot(o_x.astype(BF16), wo_ref[...])


def _xattn(x, g, wq, kv, wo, seq, tm):
    t, d = x.shape
    steps = seq // tm
    full = lambda a: pl.BlockSpec(a.shape, lambda i: (0, 0))
    return pl.pallas_call(
        _xattn_kernel,
        out_shape=jax.ShapeDtypeStruct((t, d), F32),
        grid=(t // tm,),
        in_specs=[
            pl.BlockSpec((tm, d), lambda i: (i, 0)),
            full(g), full(wq),
            pl.BlockSpec((1,) + kv.shape[1:], lambda i: (i // steps, 0, 0)),
            full(wo),
        ],
        out_specs=pl.BlockSpec((tm, d), lambda i: (i, 0)),
        compiler_params=_params(("arbitrary",)),
        name="xattn",
    )(x, g, wq, kv, wo)


def _router_kernel(x_ref, g_ref, w_ref, b_ref, hm_ref, idx_ref, rank_ref, wt_ref, cnt_ref):
    tm = x_ref.shape[0]
    hm = _rms(x_ref[...], g_ref[...]).astype(BF16)
    hm_ref[...] = hm
    lane = lax.broadcasted_iota(jnp.int32, (tm, LANES), 1)
    lane_f = lane.astype(F32)
    logits = _dot(hm, w_ref[...]) + b_ref[...]
    logits = jnp.where(lane < N_EXPERTS, logits, -jnp.inf)
    picks, vals = [], []
    onehot_all = jnp.zeros((tm, LANES), F32)
    for _ in range(TOP_K):
        mx = jnp.max(logits, axis=-1, keepdims=True)
        first_idx = jnp.min(jnp.where(logits == mx, lane_f, float(LANES)), axis=-1, keepdims=True)
        hit = lane_f == first_idx
        onehot = jnp.where(hit, 1.0, 0.0)
        logits = jnp.where(hit, -jnp.inf, logits)
        picks.append((first_idx, onehot))
        vals.append(mx)
        onehot_all = onehot_all + onehot
    exps = [jnp.exp(v - vals[0]) for v in vals]
    den = exps[0]
    for e in exps[1:]:
        den = den + e
    r_i = lax.broadcasted_iota(jnp.int32, (tm, tm), 0)
    c_i = lax.broadcasted_iota(jnp.int32, (tm, tm), 1)
    lower = jnp.where(c_i < r_i, 1.0, 0.0).astype(BF16)
    before = _dot(lower, onehot_all.astype(BF16))
    idx_out = jnp.zeros((tm, LANES), F32)
    rank_out = jnp.zeros((tm, LANES), F32)
    wt_out = jnp.zeros((tm, LANES), F32)
    for k in range(TOP_K):
        first_idx, onehot = picks[k]
        rank = jnp.sum(onehot * before, axis=-1, keepdims=True)
        idx_out = jnp.where(lane == k, first_idx, idx_out)
        rank_out = jnp.where(lane == k, rank, rank_out)
        wt_out = jnp.where(lane == k, exps[k] / den, wt_out)
    idx_ref[...] = idx_out.astype(jnp.int32)
    rank_ref[...] = rank_out.astype(jnp.int32)
    wt_ref[...] = wt_out
    cnt_ref[0] = jnp.sum(onehot_all, axis=0, keepdims=True)


def _router(x, g, w, b, tm):
    t, d = x.shape
    full = lambda a: pl.BlockSpec(a.shape, lambda i: (0, 0))
    lane_out = pl.BlockSpec((tm, LANES), lambda i: (i, 0))
    return pl.pallas_call(
        _router_kernel,
        out_shape=(
            jax.ShapeDtypeStruct((t, d), BF16),
            jax.ShapeDtypeStruct((t, LANES), jnp.int32),
            jax.ShapeDtypeStruct((t, LANES), jnp.int32),
            jax.ShapeDtypeStruct((t, LANES), F32),
            jax.ShapeDtypeStruct((t // tm, 1, LANES), F32),
        ),
        grid=(t // tm,),
        in_specs=[pl.BlockSpec((tm, d), lambda i: (i, 0)), full(g), full(w), full(b)],
        out_specs=(pl.BlockSpec((tm, d), lambda i: (i, 0)), lane_out, lane_out, lane_out,
                   pl.BlockSpec((1, 1, LANES), lambda i: (i, 0, 0))),
        compiler_params=_params(("arbitrary",)),
        name="router",
    )(x, g, w, b)


def _slot_matrix(idx_ref, rank_ref, offv_ref, rows, values=None):
    tt = idx_ref.shape[0]
    lane = lax.broadcasted_iota(jnp.int32, (tt, LANES), 1)
    r = lax.broadcasted_iota(jnp.int32, (tt, rows), 1)
    offv = offv_ref[0]
    idx = idx_ref[...]
    rank = rank_ref[...]
    out = jnp.zeros((tt, rows), F32)
    for k in range(TOP_K):
        seg = jnp.sum(jnp.where(lane == idx[:, k:k + 1], offv, 0.0), axis=-1, keepdims=True)
        row_k = seg.astype(jnp.int32) + rank[:, k:k + 1]
        out = jnp.where(r == row_k, 1.0 if values is None else values[:, k:k + 1], out)
    return out


def _segment_copies(src_ref, n8_ref, dst_ref, make_copy, tile_tokens, table_row=None):
    base = (pl.program_id(0) if table_row is None else table_row) * N_EXPERTS
    sizes = []
    size = tile_tokens
    while size >= SEG_ALIGN:
        sizes.append(size)
        size //= 2

    def visit(e, start):
        n8 = n8_ref[base + e]
        src = src_ref[base + e]
        dst = dst_ref[base + e]
        for size in sizes:
            done = n8 & (-2 * size)

            @pl.when((n8 & size) != 0)
            def _():
                cp = make_copy(pl.multiple_of(src + done, SEG_ALIGN), pl.multiple_of(dst + done, SEG_ALIGN), size)
                if start:
                    cp.start()
                else:
                    cp.wait()

    def start_all(e, c):
        visit(e, True)
        return c

    def wait_all(e, c):
        visit(e, False)
        return c

    return start_all, wait_all


def _dispatch_kernel(src_ref, n8_ref, dst_ref, nu_ref, hm_ref, idx_ref, rank_ref, offv_ref, xs_ref, buf_ref, sem, *, tm):
    rows = buf_ref.shape[0]
    n_tiles = xs_ref.shape[0] // tm
    onehot = _slot_matrix(idx_ref, rank_ref, offv_ref, rows).astype(BF16)
    buf_ref[...] = lax.dot_general(onehot, hm_ref[...], (((0,), (0,)), ((), ())), preferred_element_type=F32)

    def make_copy(src, dst, size):
        return pltpu.make_async_copy(buf_ref.at[pl.ds(src, size)], xs_ref.at[pl.ds(dst, size)], sem)

    start_all, wait_all = _segment_copies(src_ref, n8_ref, dst_ref, make_copy, hm_ref.shape[0])
    lax.fori_loop(0, N_EXPERTS, start_all, 0)
    lax.fori_loop(0, N_EXPERTS, wait_all, 0)

    @pl.when(pl.program_id(0) == pl.num_programs(0) - 1)
    def _():
        buf_ref[:tm] = jnp.zeros((tm, buf_ref.shape[1]), F32)
        start_tail, wait_tail = _segment_copies(src_ref, n8_ref, dst_ref, make_copy, tm, pl.num_programs(0))
        lax.fori_loop(0, N_EXPERTS, start_tail, 0)

        def zero_tile(i):
            return make_copy(0, pl.multiple_of(i * tm, tm), tm)

        lax.fori_loop(nu_ref[0], n_tiles, lambda i, c: (zero_tile(i).start(), c)[1], 0)
        lax.fori_loop(0, N_EXPERTS, wait_tail, 0)
        lax.fori_loop(nu_ref[0], n_tiles, lambda i, c: (zero_tile(i).wait(), c)[1], 0)


def _dispatch(tables, n_used, hm, idx, rank, offv, n_pad, tt, rows, tm):
    t, d = hm.shape
    tile = lambda w: pl.BlockSpec((tt, w), lambda i, *_: (i, 0))
    grid_spec = pltpu.PrefetchScalarGridSpec(
        num_scalar_prefetch=4,
        grid=(t // tt,),
        in_specs=[tile(d), tile(LANES), tile(LANES), pl.BlockSpec((1, 1, LANES), lambda i, *_: (i, 0, 0))],
        out_specs=pl.BlockSpec(memory_space=pl.ANY),
        scratch_shapes=[pltpu.VMEM((rows, d), F32), pltpu.SemaphoreType.DMA],
    )
    return pl.pallas_call(
        functools.partial(_dispatch_kernel, tm=tm),
        out_shape=jax.ShapeDtypeStruct((n_pad, d), F32),
        grid_spec=grid_spec,
        compiler_params=_params(("arbitrary",)),
        name="dispatch",
    )(*tables, n_used, hm, idx, rank, offv)


def _w1_prep_kernel(w_ref, o_ref):
    grp = 2 * LANES
    r_i = lax.broadcasted_iota(jnp.int32, (grp, grp), 0)
    c_i = lax.broadcasted_iota(jnp.int32, (grp, grp), 1)
    src_col = jnp.where(c_i < LANES, 2 * c_i, 2 * (c_i - LANES) + 1)
    perm = jnp.where(r_i == src_col, 1.0, 0.0).astype(BF16)
    for c in range(w_ref.shape[1] // grp):
        sl = slice(c * grp, (c + 1) * grp)
        o_ref[:, sl] = _dot(w_ref[:, sl].astype(BF16), perm).astype(BF16)


def _w1_prep(w, tm):
    r, n = w.shape
    return pl.pallas_call(
        _w1_prep_kernel,
        out_shape=jax.ShapeDtypeStruct((r, n), BF16),
        grid=(r // tm,),
        in_specs=[pl.BlockSpec((tm, n), lambda i: (i, 0))],
        out_specs=pl.BlockSpec((tm, n), lambda i: (i, 0)),
        compiler_params=_params(("arbitrary",)),
        name="w1_prep",
    )(w)


def _ffn_kernel(te_ref, nu_ref, x_ref, w1_ref, b1_ref, w2_ref, b2_ref, o_ref):
    del te_ref
    used = pl.program_id(0) < nu_ref[0]

    @pl.when(used)
    def _():
        u = _dot(x_ref[...].astype(BF16), w1_ref[0]) + b1_ref[0]
        acts = []
        for c in range(u.shape[1] // (2 * LANES)):
            glu = jnp.minimum(u[:, 2 * c * LANES:(2 * c + 1) * LANES], SWIGLU_LIMIT)
            lin = jnp.clip(u[:, (2 * c + 1) * LANES:(2 * c + 2) * LANES], -SWIGLU_LIMIT, SWIGLU_LIMIT)
            acts.append((glu * _sigmoid(SWIGLU_ALPHA * glu) * (lin + 1.0)).astype(BF16))
        o_ref[...] = _dot(jnp.concatenate(acts, axis=-1), w2_ref[0]) + b2_ref[0]

    @pl.when(jnp.logical_not(used))
    def _():
        o_ref[...] = jnp.zeros_like(o_ref)


def _ffn(tile_expert, n_used, xs, w1, b1, w2, b2, tm):
    n_pad, d = xs.shape
    f2 = w1.shape[2]
    f = w2.shape[1]
    grid_spec = pltpu.PrefetchScalarGridSpec(
        num_scalar_prefetch=2,
        grid=(n_pad // tm,),
        in_specs=[
            pl.BlockSpec((tm, d), lambda i, te, nu: (jnp.minimum(i, nu[0] - 1), 0)),
            pl.BlockSpec((1, d, f2), lambda i, te, nu: (te[i], 0, 0)),
            pl.BlockSpec((1, 1, f2), lambda i, te, nu: (te[i], 0, 0)),
            pl.BlockSpec((1, f, d), lambda i, te, nu: (te[i], 0, 0)),
            pl.BlockSpec((1, 1, d), lambda i, te, nu: (te[i], 0, 0)),
        ],
        out_specs=pl.BlockSpec((tm, d), lambda i, te, nu: (i, 0)),
    )
    return pl.pallas_call(
        _ffn_kernel,
        out_shape=jax.ShapeDtypeStruct((n_pad, d), F32),
        grid_spec=grid_spec,
        compiler_params=_params(("arbitrary",)),
        name="expert_ffn",
    )(tile_expert, n_used, xs, w1, b1, w2, b2)


def _combine_kernel(src_ref, n8_ref, dst_ref, ys_ref, idx_ref, rank_ref, wt_ref, offv_ref, x_ref, g_ref, o_ref,
                    buf_ref, sem):
    rows = buf_ref.shape[0]

    @pl.when(pl.program_id(0) == 0)
    def _():
        buf_ref[...] = jnp.zeros_like(buf_ref)

    def make_copy(src, dst, size):
        return pltpu.make_async_copy(ys_ref.at[pl.ds(dst, size)], buf_ref.at[pl.ds(src, size)], sem)

    start_all, wait_all = _segment_copies(src_ref, n8_ref, dst_ref, make_copy, x_ref.shape[0])
    lax.fori_loop(0, N_EXPERTS, start_all, 0)
    pw = _slot_matrix(idx_ref, rank_ref, offv_ref, rows, values=wt_ref[...])
    p_hi = pw.astype(BF16)
    p_lo = (pw - p_hi.astype(F32)).astype(BF16)
    lax.fori_loop(0, N_EXPERTS, wait_all, 0)
    ys = buf_ref[...]
    y_hi = ys.astype(BF16)
    y_lo = (ys - y_hi.astype(F32)).astype(BF16)
    y = _dot(p_hi, y_hi) + _dot(p_hi, y_lo) + _dot(p_lo, y_hi)
    o_ref[...] = _rms(x_ref[...] + y, g_ref[...])


def _combine(tables, ys, idx, rank, wt, offv, x, g, tt, rows):
    t, d = x.shape
    tile = lambda w: pl.BlockSpec((tt, w), lambda i, *_: (i, 0))
    grid_spec = pltpu.PrefetchScalarGridSpec(
        num_scalar_prefetch=3,
        grid=(t // tt,),
        in_specs=[pl.BlockSpec(memory_space=pl.ANY), tile(LANES), tile(LANES), tile(LANES),
                  pl.BlockSpec((1, 1, LANES), lambda i, *_: (i, 0, 0)), tile(d),
                  pl.BlockSpec((1, d), lambda i, *_: (0, 0))],
        out_specs=tile(d),
        scratch_shapes=[pltpu.VMEM((rows, d), F32), pltpu.SemaphoreType.DMA],
    )
    return pl.pallas_call(
        _combine_kernel,
        out_shape=jax.ShapeDtypeStruct((t, d), F32),
        grid_spec=grid_spec,
        compiler_params=_params(("arbitrary",)),
        name="combine",
    )(*tables, ys, idx, rank, wt, offv, x, g)


def _tile_sizes(seq):
    return dict(
        tm_proj=1024, tn_proj=P_WIDTH // 4,
        tm_rope=512,
        tq=128, tk=512,
        hgrn_chunks=4,
        tm_merge=512, tm_xattn=512,
        tm_router=512,
        tm_w1_prep=512,
        tm_ffn=512,
    )


def _layer(x, mem, positions, ts, mix_norm_g, w_in, cmp_pe, cmp_w1, cmp_b1, cmp_w2, cmp_b2, lb_logits, hgrn_norm_g,
           w_up_nsa, w_up_hgrn, w_out, xa_norm_g, xa_mem_norm_g, w_xq, w_xkv, w_xo, moe_norm_g, router_w, router_b,
           moe_w1, moe_b1, moe_w2, moe_b2, out_norm_g):
    b, s, d = x.shape
    t = b * s
    g, hg, dh = NSA_KV_GROUPS, NSA_Q_PER_GROUP, NSA_HEAD_DIM
    x2 = x.reshape(t, d)
    row = lambda v: v.reshape(1, -1).astype(F32)

    splits = [0]
    for w in (d, d, NSA_Q_W) + (NSA_KV_W,) * 6 + (3 * NSA_HEADS,) + (HGRN_W,) * 4:
        splits.append(splits[-1] + w)
    seg = lambda i: w_in[:, splits[i]:splits[i + 1]]
    (ga, gb, nq, kc, vc, ks, vs, kw, vw, ng, hq, hf, hi, hgate) = [seg(i) for i in range(14)]
    pad = jnp.zeros((d, P_WIDTH - COL_NG - 3 * NSA_HEADS), w_in.dtype)
    w_p = jnp.concatenate([ga, gb, hq, hf, hi, hgate, nq, ks, kw, kc, vc, vs, vw, ng, pad], axis=1).astype(BF16)

    p = _norm_matmul(x2, row(mix_norm_g), w_p, F32, ts["tm_proj"], ts["tn_proj"], "in_proj")

    half = dh // 2
    inv_freq = ROPE_THETA ** (-jnp.arange(half, dtype=F32) / half)
    invf = jnp.tile(inv_freq, LANES // half).reshape(1, LANES)
    q_r, kk_r = _rope(p, positions.reshape(t, 1), invf, ts["tm_rope"])
    tq = ts["tq"]
    nq = s // tq
    qt = q_r.reshape(b, nq, tq, g, hg, dh).transpose(0, 3, 1, 5, 4, 2).reshape(b, g, nq, dh, hg * tq)
    kk = kk_r.reshape(b, s, 2 * g, dh).transpose(0, 2, 1, 3)
    k_slc, k_win = kk[:, :g], kk[:, g:]
    nb = s // SLC_BLOCK
    block_onehot = (jnp.arange(s)[:, None] // SLC_BLOCK == jnp.arange(nb)[None, :]).astype(BF16)
    ks_aug = jnp.concatenate([k_slc, jnp.broadcast_to(block_onehot, (b, g, s, nb))], axis=-1)

    def values_t(v):
        vt = v.transpose(0, 1, 3, 2).astype(BF16)
        ones = jnp.ones(vt.shape[:2] + (1, vt.shape[3]), BF16)
        zeros = jnp.zeros(vt.shape[:2] + (dh - 1, vt.shape[3]), BF16)
        return jnp.concatenate([vt, ones, zeros], axis=2)

    vvt = values_t(p[:, COL_VSVW:COL_VSVW + 2 * NSA_KV_W].reshape(b, s, 2 * g, dh).transpose(0, 2, 1, 3))
    vt_slc, vt_win = vvt[:, :g], vvt[:, g:]
    gates = p[:, COL_NG:COL_NG + 3 * NSA_HEADS].reshape(b, nq, tq, g, hg, 3).transpose(0, 3, 1, 5, 4, 2)
    gates = gates.reshape(b, g, nq, 3, hg * tq)

    nr = s // CMP_STRIDE
    kcvc = p[:, COL_KCVC:COL_KCVC + 2 * NSA_KV_W].reshape(b, s, 2, g, dh).transpose(2, 0, 3, 1, 4)
    r = kcvc.reshape(2, b, g, nr, CMP_STRIDE * dh)
    pe = cmp_pe.reshape(2, 2, 1, CMP_STRIDE * dh)
    zeros_w2 = jnp.zeros_like(cmp_w2)
    w2p = jnp.stack([jnp.concatenate([cmp_w2, zeros_w2], axis=-1),
                     jnp.concatenate([zeros_w2, cmp_w2], axis=-1)], axis=1).astype(BF16)
    b2t = jnp.tile(cmp_b2, (1, g)).reshape(2, 1, LANES)
    pos_cmp = positions[:, CMP_BLOCK - 1::CMP_STRIDE]
    pos_cmp = jnp.pad(pos_cmp, ((0, 0), (0, nr - pos_cmp.shape[1]))).reshape(b, nr, 1)
    cmp = _compress(r, pe, cmp_w1.astype(BF16), cmp_b1.reshape(2, 1, CMP_HIDDEN), w2p, b2t, pos_cmp, invf)
    cmp = cmp.reshape(2, b, nr, g, dh).transpose(0, 1, 3, 2, 4)

    y_nsa = _nsa(qt, cmp[0], values_t(cmp[1]), ks_aug, vt_slc, k_win, vt_win, gates, tq, ts["tk"])
    y_nsa = y_nsa.reshape(b, g, nq, dh, hg, tq).transpose(0, 2, 5, 1, 4, 3).reshape(t, NSA_Q_W)

    y_hgrn = _hgrn(p, lb_logits.astype(F32), row(hgrn_norm_g), b, s, ts["hgrn_chunks"])

    x2 = _merge(x2, p, y_nsa, y_hgrn, w_up_nsa.astype(BF16), w_up_hgrn.astype(BF16), w_out.astype(BF16),
                ts["tm_merge"])

    n_mem = mem.shape[1]
    kv = _norm_matmul(mem.reshape(b * n_mem, d), row(xa_mem_norm_g), w_xkv.astype(BF16), BF16,
                      n_mem, 2 * XA_W, "mem_kv").reshape(b, n_mem, 2 * XA_W)
    x2 = _xattn(x2, row(xa_norm_g), w_xq.astype(BF16), kv, w_xo.astype(BF16), s, ts["tm_xattn"])

    n_exp = router_w.shape[1]
    rw = jnp.pad(router_w, ((0, 0), (0, LANES - n_exp))).astype(BF16)
    rb = jnp.pad(router_b, (0, LANES - n_exp)).reshape(1, LANES).astype(F32)
    assert n_exp == N_EXPERTS
    tt = ts["tm_router"]
    nt = t // tt
    hm, idx, rank, wt, cnt = _router(x2, row(moe_norm_g), rw, rb, tt)
    tm = ts["tm_ffn"]
    n8 = (cnt[:, 0, :n_exp].astype(jnp.int32) + SEG_ALIGN - 1) // SEG_ALIGN * SEG_ALIGN
    src_off = jnp.cumsum(n8, axis=1) - n8
    region = jnp.sum(n8, axis=0)
    padded = (region + tm - 1) // tm * tm
    ends = jnp.cumsum(padded)
    dst_off = (ends - padded)[None, :] + jnp.cumsum(n8, axis=0) - n8
    tables = tuple(jnp.concatenate([a, tail[None, :]], axis=0).reshape(-1) for a, tail in
                   ((src_off, jnp.zeros_like(region)), (n8, padded - region), (dst_off, ends - padded + region)))
    offv = jnp.pad(src_off.astype(F32), ((0, 0), (0, LANES - n_exp))).reshape(nt, 1, LANES)
    rows = tt * TOP_K + n_exp * SEG_ALIGN
    n_pad = (t * TOP_K + nt * n_exp * SEG_ALIGN + n_exp * tm + tm - 1) // tm * tm
    n_tiles = n_pad // tm
    tile_ids = jnp.arange(n_tiles, dtype=jnp.int32)
    tile_expert = jnp.sum(((ends // tm)[None, :] <= tile_ids[:, None]).astype(jnp.int32), axis=1)
    tile_expert = jnp.minimum(tile_expert, n_exp - 1)
    n_used = (ends[-1] // tm).reshape(1).astype(jnp.int32)

    xs = _dispatch(tables, n_used, hm, idx, rank, offv, n_pad, tt, rows, tm)
    f = moe_w2.shape[1]
    w1p = _w1_prep(moe_w1.reshape(n_exp * d, 2 * f), ts["tm_w1_prep"]).reshape(n_exp, d, 2 * f)
    b1p = moe_b1.reshape(n_exp, f // LANES, LANES, 2).transpose(0, 1, 3, 2).reshape(n_exp, 1, 2 * f)
    ys = _ffn(tile_expert, n_used, xs, w1p, b1p, moe_w2.astype(BF16), moe_b2.reshape(n_exp, 1, d), tm)
    out = _combine(tables, ys, idx, rank, wt, offv, x2, row(out_norm_g), tt, rows)
    return out.reshape(b, s, d)


def kernel(x, mem, positions, mix_norm_g, w_in, cmp_pe, cmp_w1, cmp_b1, cmp_w2, cmp_b2, hgrn_lb_logits, hgrn_norm_g, w_up_nsa, w_up_hgrn, w_out, xa_norm_g, xa_mem_norm_g, w_xq, w_xkv, w_xo, moe_norm_g, router_w, router_b, moe_w1, moe_b1, moe_w2, moe_b2, final_norm_g):
    depth = w_in.shape[0]
    assert depth == 1, "single-layer block: the final norm is fused into the last layer's combine"
    ts = _tile_sizes(x.shape[1])
    l = 0
    return _layer(x, mem, positions, ts, mix_norm_g[l], w_in[l], cmp_pe[l], cmp_w1[l], cmp_b1[l], cmp_w2[l], cmp_b2[l],
                  hgrn_lb_logits, hgrn_norm_g[l], w_up_nsa[l], w_up_hgrn[l], w_out[l], xa_norm_g[l], xa_mem_norm_g[l],
                  w_xq[l], w_xkv[l], w_xo[l], moe_norm_g[l], router_w[l], router_b[l], moe_w1[l], moe_b1[l], moe_w2[l],
                  moe_b2[l], final_norm_g)
```

```python
import functools

import jax
import jax.numpy as jnp
from jax import lax
from jax.experimental import pallas as pl
from jax.experimental.pallas import tpu as pltpu

EPS = 1e-6
ROPE_THETA = 10000.0
NEG_INF = -1e30
FORCE_SCORE = 1e9

NSA_HEADS = 8
NSA_KV_GROUPS = 2
NSA_Q_PER_GROUP = NSA_HEADS // NSA_KV_GROUPS
NSA_HEAD_DIM = 64
CMP_BLOCK = 32
CMP_STRIDE = 16
CMP_HIDDEN = 256
SLC_BLOCK = 64
SLC_TOPK = 16
N_LOCAL_BLOCKS = 2
WINDOW = 512
NSA_Q_W = NSA_HEADS * NSA_HEAD_DIM
NSA_KV_W = NSA_KV_GROUPS * NSA_HEAD_DIM

HGRN_HEADS = 4
HGRN_DK = 128
HGRN_DV = 128
HGRN_CHUNK = 64
HGRN_SUB = 16
HGRN_W = HGRN_HEADS * HGRN_DK

XA_HEADS = 4
XA_HEAD_DIM = 128
XA_W = XA_HEADS * XA_HEAD_DIM

N_EXPERTS = 32
TOP_K = 4
SWIGLU_ALPHA = 1.702
SWIGLU_LIMIT = 7.0

LANES = 128
SEG_ALIGN = 8
LOG2E = 1.4426950408889634
VMEM_LIMIT = 48 * 1024 * 1024

COL_GA = 0
COL_GB = 1024
COL_HQ = 2048
COL_HF = 2560
COL_HI = 3072
COL_HG = 3584
COL_NQ = 4096
COL_KSKW = 4608
COL_KCVC = 4864
COL_VSVW = 5120
COL_NG = 5376
P_WIDTH = 5632

F32 = jnp.float32
BF16 = jnp.bfloat16


def _params(sem):
    return pltpu.CompilerParams(dimension_semantics=sem, vmem_limit_bytes=VMEM_LIMIT)


def _dot(a, b):
    return jnp.dot(a, b, preferred_element_type=F32)


def _dot_nt(a, b):
    return lax.dot_general(a, b, (((1,), (1,)), ((), ())), preferred_element_type=F32)


def _rms(xf, g):
    return xf * lax.rsqrt(jnp.mean(xf * xf, axis=-1, keepdims=True) + EPS) * g


def _sigmoid(x):
    return 1.0 / (1.0 + jnp.exp(-x))


def _norm_matmul_kernel(x_ref, g_ref, w_ref, o_ref, hn_ref):
    @pl.when(pl.program_id(1) == 0)
    def _():
        hn_ref[...] = _rms(x_ref[...], g_ref[...]).astype(BF16)

    o_ref[...] = _dot(hn_ref[...], w_ref[...]).astype(o_ref.dtype)


def _norm_matmul(x, g, w, out_dtype, tm, tn, name):
    t, d = x.shape
    n = w.shape[1]
    return pl.pallas_call(
        _norm_matmul_kernel,
        out_shape=jax.ShapeDtypeStruct((t, n), out_dtype),
        grid=(t // tm, n // tn),
        in_specs=[
            pl.BlockSpec((tm, d), lambda i, j: (i, 0)),
            pl.BlockSpec((1, d), lambda i, j: (0, 0)),
            pl.BlockSpec((d, tn), lambda i, j: (0, j)),
        ],
        out_specs=pl.BlockSpec((tm, tn), lambda i, j: (i, j)),
        scratch_shapes=[pltpu.VMEM((tm, d), BF16)],
        compiler_params=_params(("arbitrary", "arbitrary")),
        name=name,
    )(x, g, w)


def _rope_coeffs(pos_col, invf):
    ang = pos_col.astype(F32) * invf
    lane = lax.broadcasted_iota(jnp.int32, ang.shape, 1)
    first = (lane & (NSA_HEAD_DIM - 1)) < (NSA_HEAD_DIM // 2)
    c = jnp.cos(ang)
    s = jnp.sin(ang)
    return c, jnp.where(first, -s, s), first


def _rope_tile(x, c, s_signed, first):
    half = NSA_HEAD_DIM // 2
    partner = jnp.where(first, pltpu.roll(x, LANES - half, 1), pltpu.roll(x, half, 1))
    return x * c + partner * s_signed


def _rope_kernel(q_ref, k_ref, pos_ref, invf_ref, qo_ref, ko_ref, *, q_scale):
    c, s_signed, first = _rope_coeffs(pos_ref[...], invf_ref[...])
    for i in range(q_ref.shape[1] // LANES):
        sl = slice(i * LANES, (i + 1) * LANES)
        qo_ref[:, sl] = (_rope_tile(q_ref[:, sl], c, s_signed, first) * q_scale).astype(BF16)
    for i in range(k_ref.shape[1] // LANES):
        sl = slice(i * LANES, (i + 1) * LANES)
        ko_ref[:, sl] = _rope_tile(k_ref[:, sl], c, s_signed, first).astype(BF16)


def _rope(p, pos_col, invf, tm):
    t = p.shape[0]
    kw = 2 * NSA_KV_W
    return pl.pallas_call(
        functools.partial(_rope_kernel, q_scale=NSA_HEAD_DIM ** -0.5 * LOG2E),
        out_shape=(jax.ShapeDtypeStruct((t, NSA_Q_W), BF16), jax.ShapeDtypeStruct((t, kw), BF16)),
        grid=(t // tm,),
        in_specs=[
            pl.BlockSpec((tm, NSA_Q_W), lambda i: (i, COL_NQ // NSA_Q_W)),
            pl.BlockSpec((tm, kw), lambda i: (i, COL_KSKW // kw)),
            pl.BlockSpec((tm, 1), lambda i: (i, 0)),
            pl.BlockSpec((1, LANES), lambda i: (0, 0)),
        ],
        out_specs=(
            pl.BlockSpec((tm, NSA_Q_W), lambda i: (i, 0)),
            pl.BlockSpec((tm, kw), lambda i: (i, 0)),
        ),
        compiler_params=_params(("arbitrary",)),
        name="rope",
    )(p, p, pos_col, invf)


def _gelu_tanh(x):
    return 0.5 * x * (1.0 + jnp.tanh(0.7978845608028654 * (x + 0.044715 * (x * x * x))))


def _compress_kernel(r_ref, pe_ref, w1_ref, b1_ref, w2_ref, b2_ref, pos_ref, invf_ref, o_ref):
    nr = r_ref.shape[3]
    half = r_ref.shape[4]
    acc = None
    for g in range(NSA_KV_GROUPS):
        r = r_ref[0, 0, g]
        top = _dot((r + pe_ref[0, 0]).astype(BF16), w1_ref[0, :half, :])
        bot = _dot((r + pe_ref[0, 1]).astype(BF16), w1_ref[0, half:, :])
        pre = top + pltpu.roll(bot, nr - 1, 0) + b1_ref[0]
        part = _dot(_gelu_tanh(pre).astype(BF16), w2_ref[0, g])
        acc = part if acc is None else acc + part
    out = acc + b2_ref[0]
    c, s_signed, first = _rope_coeffs(pos_ref[0], invf_ref[...])
    roped = _rope_tile(out, c, s_signed, first)
    is_key = pl.program_id(0) == 0
    o_ref[0, 0] = jnp.where(is_key, roped, out).astype(BF16)


def _compress(r, pe, w1, b1, w2p, b2t, pos_cmp, invf):
    _, b, g, nr, half = r.shape
    return pl.pallas_call(
        _compress_kernel,
        out_shape=jax.ShapeDtypeStruct((2, b, nr, LANES), BF16),
        grid=(2, b),
        in_specs=[
            pl.BlockSpec((1, 1, g, nr, half), lambda k, i: (k, i, 0, 0, 0)),
            pl.BlockSpec((1, 2, 1, half), lambda k, i: (k, 0, 0, 0)),
            pl.BlockSpec((1, 2 * half, CMP_HIDDEN), lambda k, i: (k, 0, 0)),
            pl.BlockSpec((1, 1, CMP_HIDDEN), lambda k, i: (k, 0, 0)),
            pl.BlockSpec((1, g, CMP_HIDDEN, LANES), lambda k, i: (k, 0, 0, 0)),
            pl.BlockSpec((1, 1, LANES), lambda k, i: (k, 0, 0)),
            pl.BlockSpec((1, nr, 1), lambda k, i: (i, 0, 0)),
            pl.BlockSpec((1, LANES), lambda k, i: (0, 0)),
        ],
        out_specs=pl.BlockSpec((1, 1, nr, LANES), lambda k, i: (k, i, 0, 0)),
        compiler_params=_params(("arbitrary", "arbitrary")),
        name="compress",
    )(r, pe, w1, b1, w2p, b2t, pos_cmp, invf)


def _nsa_kernel(qt_ref, kc_ref, vct_ref, ksa_ref, vst_ref, kw_ref, vwt_ref, g_ref, o_ref, acc_ref, out_ref,
                sa_ref, sb_ref, qa_ref, *, tq, tk, seq):
    hg = NSA_Q_PER_GROUP
    dh = NSA_HEAD_DIM
    nc = kc_ref.shape[2]
    nb = seq // SLC_BLOCK
    top_k = min(SLC_TOPK, nb)
    s0 = pl.program_id(2) * tq
    t_lane = s0 + lax.broadcasted_iota(jnp.int32, (1, tq), 1)
    gate = _sigmoid(g_ref[0, 0, 0])

    def scores(k_tile, bias):
        s = _dot(k_tile, qt_ref[0, 0, 0])
        return jnp.concatenate([s[:, h * tq:(h + 1) * tq] + bias for h in range(hg)], axis=1)

    def normalised(acc):
        return acc[:dh] / acc[dh:dh + 1]

    n_col = lax.broadcasted_iota(jnp.int32, (nc, 1), 0)
    valid_c = (n_col * CMP_STRIDE + (CMP_BLOCK - 1) <= t_lane) & (n_col < nc - 1)
    win_keys = WINDOW + tq
    w0 = pl.multiple_of(jnp.maximum(s0 - WINDOW, 0), tq)
    wpos = w0 + lax.broadcasted_iota(jnp.int32, (win_keys, 1), 0)
    bias_w = jnp.where((wpos <= t_lane) & (wpos > t_lane - WINDOW), 0.0, NEG_INF)
    s_c = scores(kc_ref[0, 0], jnp.where(valid_c, 0.0, NEG_INF))
    s_w = scores(kw_ref[0, 0, pl.ds(w0, win_keys), :], bias_w)

    e_c = jnp.exp2(s_c - jnp.max(s_c, axis=0, keepdims=True))
    t_all = s0 + (lax.broadcasted_iota(jnp.int32, (1, hg * tq), 1) & (tq - 1))
    row_ok = t_all >= CMP_BLOCK - 1
    pn = e_c * jnp.where(row_ok, 1.0 / jnp.sum(e_c, axis=0, keepdims=True), 0.0)
    out_ref[...] = gate[0:1] * _dot(vct_ref[0, 0], pn.astype(BF16))[:dh]
    p_sum = pn[:, 0:tq]
    for h in range(1, hg):
        p_sum = p_sum + pn[:, h * tq:(h + 1) * tq]

    j_col = lax.broadcasted_iota(jnp.int32, (nb, 1), 0)
    n_row = lax.broadcasted_iota(jnp.int32, (1, nc), 1) * CMP_STRIDE
    overlap = (n_row < j_col * SLC_BLOCK + SLC_BLOCK) & (n_row + CMP_BLOCK > j_col * SLC_BLOCK)
    overlap = jnp.where(overlap, 1.0, 0.0).astype(BF16)
    p_hi = p_sum.astype(BF16)
    p_lo = (p_sum - p_hi.astype(F32)).astype(BF16)
    imp = _dot(overlap, p_hi) + _dot(overlap, p_lo)

    p_w = jnp.exp2(s_w - jnp.max(s_w, axis=0, keepdims=True)).astype(BF16)
    out_ref[...] += gate[2:3] * normalised(_dot(vwt_ref[0, 0, :, pl.ds(w0, win_keys)], p_w))

    cur = t_lane >> 6
    causal_b = j_col <= cur
    forced = (j_col == 0) | (causal_b & (j_col > cur - N_LOCAL_BLOCKS))
    score = jnp.where(forced, FORCE_SCORE, jnp.where(causal_b, imp, -1.0))
    j_f = jnp.broadcast_to(j_col.astype(F32), (nb, tq))
    for _ in range(top_k):
        mx = jnp.max(score, axis=0, keepdims=True)
        first_idx = jnp.min(jnp.where(score == mx, j_f, float(nb)), axis=0, keepdims=True)
        score = jnp.where(j_f == first_idx, -jnp.inf, score)
    sel = jnp.where(causal_b & (score == -jnp.inf), 1.0, 0.0)

    sel_bias = ((sel - 1.0) * (-NEG_INF)).astype(BF16)
    qa_ref[:dh] = qt_ref[0, 0, 0]
    qa_ref[dh:] = jnp.concatenate([sel_bias] * hg, axis=1)
    k_col = lax.broadcasted_iota(jnp.int32, (tk, 1), 0)

    def put_scores(k0, buf_ref):
        buf_ref[...] = _dot(ksa_ref[0, 0, pl.ds(pl.multiple_of(k0, tk), tk), :], qa_ref[...])

    def consume(k0, buf_ref, m_old, causal):
        s = buf_ref[...]
        if causal:
            bias = jnp.where(k0 + k_col <= t_lane, 0.0, NEG_INF)
            s = jnp.concatenate([s[:, h * tq:(h + 1) * tq] + bias for h in range(hg)], axis=1)
        m_new = jnp.maximum(m_old, jnp.max(s, axis=0, keepdims=True))
        pv = _dot(vst_ref[0, 0, :, pl.ds(pl.multiple_of(k0, tk), tk)], jnp.exp2(s - m_new).astype(BF16))
        acc_ref[...] = jnp.exp2(m_old - m_new) * acc_ref[...] + pv
        return m_new

    def slc_pair(it, m):
        k0 = it * (2 * tk)
        put_scores(k0 + tk, sb_ref)
        m = consume(k0, sa_ref, m, False)
        put_scores(k0 + 2 * tk, sa_ref)
        return consume(k0 + tk, sb_ref, m, False)

    acc_ref[...] = jnp.zeros_like(acc_ref)
    put_scores(jnp.int32(0), sa_ref)
    d_tile = s0 // tk
    n_full = d_tile // 2
    m_s = lax.fori_loop(0, n_full, slc_pair, jnp.full((1, hg * tq), NEG_INF, F32))
    e0 = n_full * (2 * tk)

    @pl.when(d_tile % 2 == 1)
    def _():
        put_scores(e0 + tk, sb_ref)
        consume(e0 + tk, sb_ref, consume(e0, sa_ref, m_s, False), True)

    @pl.when(d_tile % 2 == 0)
    def _():
        consume(e0, sa_ref, m_s, True)

    o_ref[0, 0, 0] = (out_ref[...] + gate[1:2] * normalised(acc_ref[...])).astype(o_ref.dtype)


def _nsa(qt, kc, vct, ksa, vst, kw, vwt, gates, tq, tk):
    b, g, nq, dh, lanes = qt.shape
    hg = lanes // tq
    s = nq * tq
    nr = kc.shape[2]
    k_spec = pl.BlockSpec((1, 1, s, dh), lambda i, j, k: (i, j, 0, 0))
    vt_spec = pl.BlockSpec((1, 1, 2 * dh, s), lambda i, j, k: (i, j, 0, 0))
    return pl.pallas_call(
        functools.partial(_nsa_kernel, tq=tq, tk=tk, seq=s),
        out_shape=jax.ShapeDtypeStruct((b, g, nq, dh, hg * tq), BF16),
        grid=(b, g, nq),
        in_specs=[
            pl.BlockSpec((1, 1, 1, dh, hg * tq), lambda i, j, k: (i, j, k, 0, 0)),
            pl.BlockSpec((1, 1, nr, dh), lambda i, j, k: (i, j, 0, 0)),
            pl.BlockSpec((1, 1, 2 * dh, nr), lambda i, j, k: (i, j, 0, 0)),
            pl.BlockSpec((1, 1, s, ksa.shape[3]), lambda i, j, k: (i, j, 0, 0)), vt_spec, k_spec, vt_spec,
            pl.BlockSpec((1, 1, 1, 3, hg * tq), lambda i, j, k: (i, j, k, 0, 0)),
        ],
        out_specs=pl.BlockSpec((1, 1, 1, dh, hg * tq), lambda i, j, k: (i, j, k, 0, 0)),
        scratch_shapes=[pltpu.VMEM((2 * dh, hg * tq), F32), pltpu.VMEM((dh, hg * tq), F32),
                        pltpu.VMEM((tk, hg * tq), F32), pltpu.VMEM((tk, hg * tq), F32),
                        pltpu.VMEM((ksa.shape[3], hg * tq), BF16)],
        compiler_params=_params(("arbitrary", "arbitrary", "arbitrary")),
        name="nsa",
    )(qt, kc, vct, ksa, vst, kw, vwt, gates)


def _cumsum_rows(x):
    n = x.shape[0]
    row = lax.broadcasted_iota(jnp.int32, x.shape, 0)
    d = 1
    while d < n:
        x = x + jnp.where(row >= d, pltpu.roll(x, d, 0), 0.0)
        d *= 2
    return x


def _hgrn_kernel(q_ref, f_ref, i_ref, g_ref, lb_ref, gn_ref, o_ref, st_ref, *, n_chunks):
    @pl.when(pl.program_id(1) == 0)
    def _():
        st_ref[...] = jnp.zeros_like(st_ref)

    c_len = HGRN_CHUNK
    sub = HGRN_SUB
    lbl = lb_ref[...]
    lb_e = jnp.exp(lbl - jnp.max(lbl, axis=0, keepdims=True))
    lb_all = lb_e[0:1] / jnp.sum(lb_e, axis=0, keepdims=True)

    items = [(c, h) for c in range(n_chunks) for h in range(HGRN_HEADS)]
    wave1 = {}
    for c, h in items:
        rs = slice(c * c_len, (c + 1) * c_len)
        ls = slice(h * HGRN_DK, (h + 1) * HGRN_DK)
        lb = lb_all[:, ls]
        f = lb + (1.0 - lb) * _sigmoid(f_ref[rs, ls])
        k = 1.0 - f
        b = _cumsum_rows(jnp.log(f))
        q = q_ref[rs, ls]
        v32 = i_ref[rs, ls]
        b_end = b[c_len - 1:c_len]
        attn = []
        for blk in range(c_len // sub):
            lo, hi = blk * sub, (blk + 1) * sub
            beta = b[lo - 1:lo] if blk > 0 else jnp.zeros_like(b_end)
            qd = (q[lo:hi] * jnp.exp(b[lo:hi] - beta)).astype(BF16)
            kd = (k[:hi] * jnp.exp(beta - b[:hi])).astype(BF16)
            attn.append(_dot_nt(qd, kd))
        update = _dot(v32.T.astype(BF16), (k * jnp.exp(b_end - b)).astype(BF16))
        wave1[c, h] = ((q * jnp.exp(b)).astype(BF16), jnp.exp(b_end), update, attn)

    o_inter = {}
    for h in range(HGRN_HEADS):
        st = st_ref[h]
        for c in range(n_chunks):
            q_dec, decay, update, _ = wave1[c, h]
            o_inter[c, h] = _dot_nt(q_dec, st.astype(BF16))
            st = st * decay + update
        st_ref[h] = st

    for c, h in items:
        rs = slice(c * c_len, (c + 1) * c_len)
        ls = slice(h * HGRN_DK, (h + 1) * HGRN_DK)
        v = i_ref[rs, ls].astype(BF16)
        pieces = []
        for blk, a in enumerate(wave1[c, h][3]):
            lo, hi = blk * sub, (blk + 1) * sub
            ti = lax.broadcasted_iota(jnp.int32, (sub, hi), 0)
            si = lax.broadcasted_iota(jnp.int32, (sub, hi), 1)
            pieces.append(_dot(jnp.where(si <= ti + lo, a, 0.0).astype(BF16), v[:hi]))
        o = o_inter[c, h] + jnp.concatenate(pieces, axis=0)
        gate = g_ref[rs, ls]
        o_ref[rs, ls] = (_rms(o, gn_ref[...]) * (gate * _sigmoid(gate))).astype(o_ref.dtype)


def _hgrn(p, lb_logits, gn, batch, seq, n_chunks):
    t = p.shape[0]
    rows = n_chunks * HGRN_CHUNK
    steps = seq // rows

    def col(cb):
        return pl.BlockSpec((rows, HGRN_W), lambda i, j: (i * steps + j, cb))

    return pl.pallas_call(
        functools.partial(_hgrn_kernel, n_chunks=n_chunks),
        out_shape=jax.ShapeDtypeStruct((t, HGRN_W), BF16),
        grid=(batch, steps),
        in_specs=[
            col(COL_HQ // HGRN_W), col(COL_HF // HGRN_W), col(COL_HI // HGRN_W), col(COL_HG // HGRN_W),
            pl.BlockSpec(lb_logits.shape, lambda i, j: (0, 0)),
            pl.BlockSpec((1, HGRN_DV), lambda i, j: (0, 0)),
        ],
        out_specs=pl.BlockSpec((rows, HGRN_W), lambda i, j: (i * steps + j, 0)),
        scratch_shapes=[pltpu.VMEM((HGRN_HEADS, HGRN_DV, HGRN_DK), F32)],
        compiler_params=_params(("arbitrary", "arbitrary")),
        name="hgrn",
    )(p, p, p, p, lb_logits, gn)


def _merge_kernel(x_ref, ga_ref, gb_ref, yn_ref, yh_ref, wn_ref, wh_ref, wo_ref, o_ref):
    mixed = _sigmoid(ga_ref[...]) * _dot(yn_ref[...], wn_ref[...]) + _sigmoid(gb_ref[...]) * _dot(yh_ref[...], wh_ref[...])
    o_ref[...] = x_ref[...] + _dot(mixed.astype(BF16), wo_ref[...])


def _merge(x, p, y_nsa, y_hgrn, wn, wh, wo, tm):
    t, d = x.shape
    full = lambda a: pl.BlockSpec(a.shape, lambda i: (0, 0))
    return pl.pallas_call(
        _merge_kernel,
        out_shape=jax.ShapeDtypeStruct((t, d), F32),
        grid=(t // tm,),
        in_specs=[
            pl.BlockSpec((tm, d), lambda i: (i, 0)),
            pl.BlockSpec((tm, d), lambda i: (i, COL_GA // d)),
            pl.BlockSpec((tm, d), lambda i: (i, COL_GB // d)),
            pl.BlockSpec((tm, NSA_Q_W), lambda i: (i, 0)),
            pl.BlockSpec((tm, HGRN_W), lambda i: (i, 0)),
            full(wn), full(wh), full(wo),
        ],
        out_specs=pl.BlockSpec((tm, d), lambda i: (i, 0)),
        compiler_params=_params(("arbitrary",)),
        name="merge",
    )(x, p, p, y_nsa, y_hgrn, wn, wh, wo)


def _xattn_kernel(x_ref, g_ref, wq_ref, kv_ref, wo_ref, o_ref):
    x = x_ref[...]
    xq = _dot(_rms(x, g_ref[...]).astype(BF16), wq_ref[...]).astype(BF16)
    heads = [slice(h * XA_HEAD_DIM, (h + 1) * XA_HEAD_DIM) for h in range(XA_HEADS)]
    scores = [_dot_nt(xq[:, ls], kv_ref[0, :, ls]) * (XA_HEAD_DIM ** -0.5) for ls in heads]
    outs = []
    for h, s in enumerate(scores):
        e = jnp.exp(s - jnp.max(s, axis=-1, keepdims=True))
        p = e / jnp.sum(e, axis=-1, keepdims=True)
        outs.append(_dot(p.astype(BF16), kv_ref[0, :, XA_W + h * XA_HEAD_DIM:XA_W + (h + 1) * XA_HEAD_DIM]))
    o_x = jnp.concatenate(outs, axis=-1)
    o_ref[...] = x + _dot(o_x.astype(BF16), wo_ref[...])


def _xattn(x, g, wq, kv, wo, seq, tm):
    t, d = x.shape
    steps = seq // tm
    full = lambda a: pl.BlockSpec(a.shape, lambda i: (0, 0))
    return pl.pallas_call(
        _xattn_kernel,
        out_shape=jax.ShapeDtypeStruct((t, d), F32),
        grid=(t // tm,),
        in_specs=[
            pl.BlockSpec((tm, d), lambda i: (i, 0)),
            full(g), full(wq),
            pl.BlockSpec((1,) + kv.shape[1:], lambda i: (i // steps, 0, 0)),
            full(wo),
        ],
        out_specs=pl.BlockSpec((tm, d), lambda i: (i, 0)),
        compiler_params=_params(("arbitrary",)),
        name="xattn",
    )(x, g, wq, kv, wo)


def _router_kernel(x_ref, g_ref, w_ref, b_ref, hm_ref, idx_ref, rank_ref, wt_ref, cnt_ref):
    tm = x_ref.shape[0]
    hm = _rms(x_ref[...], g_ref[...]).astype(BF16)
    hm_ref[...] = hm
    lane = lax.broadcasted_iota(jnp.int32, (tm, LANES), 1)
    lane_f = lane.astype(F32)
    logits = _dot(hm, w_ref[...]) + b_ref[...]
    logits = jnp.where(lane < N_EXPERTS, logits, -jnp.inf)
    picks, vals = [], []
    onehot_all = jnp.zeros((tm, LANES), F32)
    for _ in range(TOP_K):
        mx = jnp.max(logits, axis=-1, keepdims=True)
        first_idx = jnp.min(jnp.where(logits == mx, lane_f, float(LANES)), axis=-1, keepdims=True)
        hit = lane_f == first_idx
        onehot = jnp.where(hit, 1.0, 0.0)
        logits = jnp.where(hit, -jnp.inf, logits)
        picks.append((first_idx, onehot))
        vals.append(mx)
        onehot_all = onehot_all + onehot
    exps = [jnp.exp(v - vals[0]) for v in vals]
    den = exps[0]
    for e in exps[1:]:
        den = den + e
    r_i = lax.broadcasted_iota(jnp.int32, (tm, tm), 0)
    c_i = lax.broadcasted_iota(jnp.int32, (tm, tm), 1)
    lower = jnp.where(c_i < r_i, 1.0, 0.0).astype(BF16)
    before = _dot(lower, onehot_all.astype(BF16))
    idx_out = jnp.zeros((tm, LANES), F32)
    rank_out = jnp.zeros((tm, LANES), F32)
    wt_out = jnp.zeros((tm, LANES), F32)
    for k in range(TOP_K):
        first_idx, onehot = picks[k]
        rank = jnp.sum(onehot * before, axis=-1, keepdims=True)
        idx_out = jnp.where(lane == k, first_idx, idx_out)
        rank_out = jnp.where(lane == k, rank, rank_out)
        wt_out = jnp.where(lane == k, exps[k] / den, wt_out)
    idx_ref[...] = idx_out.astype(jnp.int32)
    rank_ref[...] = rank_out.astype(jnp.int32)
    wt_ref[...] = wt_out
    cnt_ref[0] = jnp.sum(onehot_all, axis=0, keepdims=True)


def _router(x, g, w, b, tm):
    t, d = x.shape
    full = lambda a: pl.BlockSpec(a.shape, lambda i: (0, 0))
    lane_out = pl.BlockSpec((tm, LANES), lambda i: (i, 0))
    return pl.pallas_call(
        _router_kernel,
        out_shape=(
            jax.ShapeDtypeStruct((t, d), BF16),
            jax.ShapeDtypeStruct((t, LANES), jnp.int32),
            jax.ShapeDtypeStruct((t, LANES), jnp.int32),
            jax.ShapeDtypeStruct((t, LANES), F32),
            jax.ShapeDtypeStruct((t // tm, 1, LANES), F32),
        ),
        grid=(t // tm,),
        in_specs=[pl.BlockSpec((tm, d), lambda i: (i, 0)), full(g), full(w), full(b)],
        out_specs=(pl.BlockSpec((tm, d), lambda i: (i, 0)), lane_out, lane_out, lane_out,
                   pl.BlockSpec((1, 1, LANES), lambda i: (i, 0, 0))),
        compiler_params=_params(("arbitrary",)),
        name="router",
    )(x, g, w, b)


def _slot_matrix(idx_ref, rank_ref, offv_ref, rows, values=None):
    tt = idx_ref.shape[0]
    lane = lax.broadcasted_iota(jnp.int32, (tt, LANES), 1)
    r = lax.broadcasted_iota(jnp.int32, (tt, rows), 1)
    offv = offv_ref[0]
    idx = idx_ref[...]
    rank = rank_ref[...]
    out = jnp.zeros((tt, rows), F32)
    for k in range(TOP_K):
        seg = jnp.sum(jnp.where(lane == idx[:, k:k + 1], offv, 0.0), axis=-1, keepdims=True)
        row_k = seg.astype(jnp.int32) + rank[:, k:k + 1]
        out = jnp.where(r == row_k, 1.0 if values is None else values[:, k:k + 1], out)
    return out


def _segment_copies(src_ref, n8_ref, dst_ref, make_copy, tile_tokens, table_row=None):
    base = (pl.program_id(0) if table_row is None else table_row) * N_EXPERTS
    sizes = []
    size = tile_tokens
    while size >= SEG_ALIGN:
        sizes.append(size)
        size //= 2

    def visit(e, start):
        n8 = n8_ref[base + e]
        src = src_ref[base + e]
        dst = dst_ref[base + e]
        for size in sizes:
            done = n8 & (-2 * size)

            @pl.when((n8 & size) != 0)
            def _():
                cp = make_copy(pl.multiple_of(src + done, SEG_ALIGN), pl.multiple_of(dst + done, SEG_ALIGN), size)
                if start:
                    cp.start()
                else:
                    cp.wait()

    def start_all(e, c):
        visit(e, True)
        return c

    def wait_all(e, c):
        visit(e, False)
        return c

    return start_all, wait_all


def _dispatch_kernel(src_ref, n8_ref, dst_ref, nu_ref, hm_ref, idx_ref, rank_ref, offv_ref, xs_ref, buf_ref, sem,
                     *, tm, n_steps):
    rows = buf_ref.shape[1]
    tt = hm_ref.shape[0]
    n_tiles = xs_ref.shape[0] // tm
    step = pl.program_id(0)
    slot = step % 2

    def copies(table_row, buf_slot, tile_tokens=tt):
        def make_copy(src, dst, size):
            return pltpu.make_async_copy(buf_ref.at[buf_slot, pl.ds(src, size)], xs_ref.at[pl.ds(dst, size)],
                                         sem.at[buf_slot])
        return _segment_copies(src_ref, n8_ref, dst_ref, make_copy, tile_tokens, table_row) + (make_copy,)

    @pl.when(step >= 2)
    def _():
        lax.fori_loop(0, N_EXPERTS, copies(step - 2, slot)[1], 0)

    onehot = _slot_matrix(idx_ref, rank_ref, offv_ref, rows).astype(BF16)
    buf_ref[slot] = lax.dot_general(onehot, hm_ref[...], (((0,), (0,)), ((), ())), preferred_element_type=F32)
    start_cur, wait_cur, make_copy = copies(step, slot)
    lax.fori_loop(0, N_EXPERTS, start_cur, 0)

    @pl.when(step == n_steps - 1)
    def _():
        if n_steps >= 2:
            lax.fori_loop(0, N_EXPERTS, copies(step - 1, 1 - slot)[1], 0)
        lax.fori_loop(0, N_EXPERTS, wait_cur, 0)
        buf_ref[slot, :tm] = jnp.zeros((tm, buf_ref.shape[2]), F32)
        start_tail, wait_tail, _ = copies(n_steps, slot, tm)
        lax.fori_loop(0, N_EXPERTS, start_tail, 0)

        def zero_tile(i):
            return make_copy(0, pl.multiple_of(i * tm, tm), tm)

        lax.fori_loop(nu_ref[0], n_tiles, lambda i, c: (zero_tile(i).start(), c)[1], 0)
        lax.fori_loop(0, N_EXPERTS, wait_tail, 0)
        lax.fori_loop(nu_ref[0], n_tiles, lambda i, c: (zero_tile(i).wait(), c)[1], 0)


def _dispatch(tables, n_used, hm, idx, rank, offv, n_pad, tt, rows, tm):
    t, d = hm.shape
    tile = lambda w: pl.BlockSpec((tt, w), lambda i, *_: (i, 0))
    grid_spec = pltpu.PrefetchScalarGridSpec(
        num_scalar_prefetch=4,
        grid=(t // tt,),
        in_specs=[tile(d), tile(LANES), tile(LANES), pl.BlockSpec((1, 1, LANES), lambda i, *_: (i, 0, 0))],
        out_specs=pl.BlockSpec(memory_space=pl.ANY),
        scratch_shapes=[pltpu.VMEM((2, rows, d), F32), pltpu.SemaphoreType.DMA((2,))],
    )
    return pl.pallas_call(
        functools.partial(_dispatch_kernel, tm=tm, n_steps=t // tt),
        out_shape=jax.ShapeDtypeStruct((n_pad, d), F32),
        grid_spec=grid_spec,
        compiler_params=_params(("arbitrary",)),
        name="dispatch",
    )(*tables, n_used, hm, idx, rank, offv)


def _w1_prep_kernel(w_ref, o_ref):
    grp = 2 * LANES
    r_i = lax.broadcasted_iota(jnp.int32, (grp, grp), 0)
    c_i = lax.broadcasted_iota(jnp.int32, (grp, grp), 1)
    src_col = jnp.where(c_i < LANES, 2 * c_i, 2 * (c_i - LANES) + 1)
    perm = jnp.where(r_i == src_col, 1.0, 0.0).astype(BF16)
    for c in range(w_ref.shape[1] // grp):
        sl = slice(c * grp, (c + 1) * grp)
        o_ref[:, sl] = _dot(w_ref[:, sl].astype(BF16), perm).astype(BF16)


def _w1_prep(w, tm):
    r, n = w.shape
    return pl.pallas_call(
        _w1_prep_kernel,
        out_shape=jax.ShapeDtypeStruct((r, n), BF16),
        grid=(r // tm,),
        in_specs=[pl.BlockSpec((tm, n), lambda i: (i, 0))],
        out_specs=pl.BlockSpec((tm, n), lambda i: (i, 0)),
        compiler_params=_params(("arbitrary",)),
        name="w1_prep",
    )(w)


def _ffn_kernel(te_ref, nu_ref, x_ref, w1_ref, b1_ref, w2_ref, b2_ref, o_ref):
    del te_ref
    used = pl.program_id(0) < nu_ref[0]

    @pl.when(used)
    def _():
        u = _dot(x_ref[...].astype(BF16), w1_ref[0]) + b1_ref[0]
        acts = []
        for c in range(u.shape[1] // (2 * LANES)):
            glu = jnp.minimum(u[:, 2 * c * LANES:(2 * c + 1) * LANES], SWIGLU_LIMIT)
            lin = jnp.clip(u[:, (2 * c + 1) * LANES:(2 * c + 2) * LANES], -SWIGLU_LIMIT, SWIGLU_LIMIT)
            acts.append((glu * _sigmoid(SWIGLU_ALPHA * glu) * (lin + 1.0)).astype(BF16))
        o_ref[...] = _dot(jnp.concatenate(acts, axis=-1), w2_ref[0]) + b2_ref[0]

    @pl.when(jnp.logical_not(used))
    def _():
        o_ref[...] = jnp.zeros_like(o_ref)


def _ffn(tile_expert, n_used, xs, w1, b1, w2, b2, tm):
    n_pad, d = xs.shape
    f2 = w1.shape[2]
    f = w2.shape[1]
    grid_spec = pltpu.PrefetchScalarGridSpec(
        num_scalar_prefetch=2,
        grid=(n_pad // tm,),
        in_specs=[
            pl.BlockSpec((tm, d), lambda i, te, nu: (jnp.minimum(i, nu[0] - 1), 0)),
            pl.BlockSpec((1, d, f2), lambda i, te, nu: (te[i], 0, 0)),
            pl.BlockSpec((1, 1, f2), lambda i, te, nu: (te[i], 0, 0)),
            pl.BlockSpec((1, f, d), lambda i, te, nu: (te[i], 0, 0)),
            pl.BlockSpec((1, 1, d), lambda i, te, nu: (te[i], 0, 0)),
        ],
        out_specs=pl.BlockSpec((tm, d), lambda i, te, nu: (i, 0)),
    )
    return pl.pallas_call(
        _ffn_kernel,
        out_shape=jax.ShapeDtypeStruct((n_pad, d), F32),
        grid_spec=grid_spec,
        compiler_params=_params(("arbitrary",)),
        name="expert_ffn",
    )(tile_expert, n_used, xs, w1, b1, w2, b2)


def _combine_kernel(src_ref, n8_ref, dst_ref, ys_ref, idx_ref, rank_ref, wt_ref, offv_ref, x_ref, g_ref, o_ref,
                    buf_ref, sem, *, n_steps):
    rows = buf_ref.shape[1]
    step = pl.program_id(0)
    slot = step % 2

    def copies(table_row, buf_slot):
        def make_copy(src, dst, size):
            return pltpu.make_async_copy(ys_ref.at[pl.ds(dst, size)], buf_ref.at[buf_slot, pl.ds(src, size)],
                                         sem.at[buf_slot])
        return _segment_copies(src_ref, n8_ref, dst_ref, make_copy, x_ref.shape[0], table_row)

    @pl.when(step == 0)
    def _():
        buf_ref[...] = jnp.zeros_like(buf_ref)
        lax.fori_loop(0, N_EXPERTS, copies(0, 0)[0], 0)

    @pl.when(step + 1 < n_steps)
    def _():
        lax.fori_loop(0, N_EXPERTS, copies(step + 1, 1 - slot)[0], 0)

    pw = _slot_matrix(idx_ref, rank_ref, offv_ref, rows, values=wt_ref[...])
    p_hi = pw.astype(BF16)
    p_lo = (pw - p_hi.astype(F32)).astype(BF16)
    lax.fori_loop(0, N_EXPERTS, copies(step, slot)[1], 0)
    ys = buf_ref[slot]
    y_hi = ys.astype(BF16)
    y_lo = (ys - y_hi.astype(F32)).astype(BF16)
    y = _dot(p_hi, y_hi) + _dot(p_hi, y_lo) + _dot(p_lo, y_hi)
    o_ref[...] = _rms(x_ref[...] + y, g_ref[...])


def _combine(tables, ys, idx, rank, wt, offv, x, g, tt, rows):
    t, d = x.shape
    tile = lambda w: pl.BlockSpec((tt, w), lambda i, *_: (i, 0))
    grid_spec = pltpu.PrefetchScalarGridSpec(
        num_scalar_prefetch=3,
        grid=(t // tt,),
        in_specs=[pl.BlockSpec(memory_space=pl.ANY), tile(LANES), tile(LANES), tile(LANES),
                  pl.BlockSpec((1, 1, LANES), lambda i, *_: (i, 0, 0)), tile(d),
                  pl.BlockSpec((1, d), lambda i, *_: (0, 0))],
        out_specs=tile(d),
        scratch_shapes=[pltpu.VMEM((2, rows, d), F32), pltpu.SemaphoreType.DMA((2,))],
    )
    return pl.pallas_call(
        functools.partial(_combine_kernel, n_steps=t // tt),
        out_shape=jax.ShapeDtypeStruct((t, d), F32),
        grid_spec=grid_spec,
        compiler_params=_params(("arbitrary",)),
        name="combine",
    )(*tables, ys, idx, rank, wt, offv, x, g)


def _tile_sizes(seq):
    return dict(
        tm_proj=1024, tn_proj=P_WIDTH // 4,
        tm_rope=512,
        tq=256, tk=512,
        hgrn_chunks=4,
        tm_merge=512, tm_xattn=512,
        tm_router=512,
        tm_w1_prep=512,
        tm_ffn=512,
    )


def _layer(x, mem, positions, ts, mix_norm_g, w_in, cmp_pe, cmp_w1, cmp_b1, cmp_w2, cmp_b2, lb_logits, hgrn_norm_g,
           w_up_nsa, w_up_hgrn, w_out, xa_norm_g, xa_mem_norm_g, w_xq, w_xkv, w_xo, moe_norm_g, router_w, router_b,
           moe_w1, moe_b1, moe_w2, moe_b2, out_norm_g):
    b, s, d = x.shape
    t = b * s
    g, hg, dh = NSA_KV_GROUPS, NSA_Q_PER_GROUP, NSA_HEAD_DIM
    x2 = x.reshape(t, d)
    row = lambda v: v.reshape(1, -1).astype(F32)

    splits = [0]
    for w in (d, d, NSA_Q_W) + (NSA_KV_W,) * 6 + (3 * NSA_HEADS,) + (HGRN_W,) * 4:
        splits.append(splits[-1] + w)
    seg = lambda i: w_in[:, splits[i]:splits[i + 1]]
    (ga, gb, nq, kc, vc, ks, vs, kw, vw, ng, hq, hf, hi, hgate) = [seg(i) for i in range(14)]
    pad = jnp.zeros((d, P_WIDTH - COL_NG - 3 * NSA_HEADS), w_in.dtype)
    w_p = jnp.concatenate([ga, gb, hq, hf, hi, hgate, nq, ks, kw, kc, vc, vs, vw, ng, pad], axis=1).astype(BF16)

    p = _norm_matmul(x2, row(mix_norm_g), w_p, F32, ts["tm_proj"], ts["tn_proj"], "in_proj")

    half = dh // 2
    inv_freq = ROPE_THETA ** (-jnp.arange(half, dtype=F32) / half)
    invf = jnp.tile(inv_freq, LANES // half).reshape(1, LANES)
    q_r, kk_r = _rope(p, positions.reshape(t, 1), invf, ts["tm_rope"])
    tq = ts["tq"]
    nq = s // tq
    qt = q_r.reshape(b, nq, tq, g, hg, dh).transpose(0, 3, 1, 5, 4, 2).reshape(b, g, nq, dh, hg * tq)
    kk = kk_r.reshape(b, s, 2 * g, dh).transpose(0, 2, 1, 3)
    k_slc, k_win = kk[:, :g], kk[:, g:]
    nb = s // SLC_BLOCK
    block_onehot = (jnp.arange(s)[:, None] // SLC_BLOCK == jnp.arange(nb)[None, :]).astype(BF16)
    ks_aug = jnp.concatenate([k_slc, jnp.broadcast_to(block_onehot, (b, g, s, nb))], axis=-1)

    def values_t(v):
        vt = v.transpose(0, 1, 3, 2).astype(BF16)
        ones = jnp.ones(vt.shape[:2] + (1, vt.shape[3]), BF16)
        zeros = jnp.zeros(vt.shape[:2] + (dh - 1, vt.shape[3]), BF16)
        return jnp.concatenate([vt, ones, zeros], axis=2)

    vvt = values_t(p[:, COL_VSVW:COL_VSVW + 2 * NSA_KV_W].reshape(b, s, 2 * g, dh).transpose(0, 2, 1, 3))
    vt_slc, vt_win = vvt[:, :g], vvt[:, g:]
    gates = p[:, COL_NG:COL_NG + 3 * NSA_HEADS].reshape(b, nq, tq, g, hg, 3).transpose(0, 3, 1, 5, 4, 2)
    gates = gates.reshape(b, g, nq, 3, hg * tq)

    nr = s // CMP_STRIDE
    kcvc = p[:, COL_KCVC:COL_KCVC + 2 * NSA_KV_W].reshape(b, s, 2, g, dh).transpose(2, 0, 3, 1, 4)
    r = kcvc.reshape(2, b, g, nr, CMP_STRIDE * dh)
    pe = cmp_pe.reshape(2, 2, 1, CMP_STRIDE * dh)
    zeros_w2 = jnp.zeros_like(cmp_w2)
    w2p = jnp.stack([jnp.concatenate([cmp_w2, zeros_w2], axis=-1),
                     jnp.concatenate([zeros_w2, cmp_w2], axis=-1)], axis=1).astype(BF16)
    b2t = jnp.tile(cmp_b2, (1, g)).reshape(2, 1, LANES)
    pos_cmp = positions[:, CMP_BLOCK - 1::CMP_STRIDE]
    pos_cmp = jnp.pad(pos_cmp, ((0, 0), (0, nr - pos_cmp.shape[1]))).reshape(b, nr, 1)
    cmp = _compress(r, pe, cmp_w1.astype(BF16), cmp_b1.reshape(2, 1, CMP_HIDDEN), w2p, b2t, pos_cmp, invf)
    cmp = cmp.reshape(2, b, nr, g, dh).transpose(0, 1, 3, 2, 4)

    y_nsa = _nsa(qt, cmp[0], values_t(cmp[1]), ks_aug, vt_slc, k_win, vt_win, gates, tq, ts["tk"])
    y_nsa = y_nsa.reshape(b, g, nq, dh, hg, tq).transpose(0, 2, 5, 1, 4, 3).reshape(t, NSA_Q_W)

    y_hgrn = _hgrn(p, lb_logits.astype(F32), row(hgrn_norm_g), b, s, ts["hgrn_chunks"])

    x2 = _merge(x2, p, y_nsa, y_hgrn, w_up_nsa.astype(BF16), w_up_hgrn.astype(BF16), w_out.astype(BF16),
                ts["tm_merge"])

    n_mem = mem.shape[1]
    kv = _norm_matmul(mem.reshape(b * n_mem, d), row(xa_mem_norm_g), w_xkv.astype(BF16), BF16,
                      n_mem, 2 * XA_W, "mem_kv").reshape(b, n_mem, 2 * XA_W)
    x2 = _xattn(x2, row(xa_norm_g), w_xq.astype(BF16), kv, w_xo.astype(BF16), s, ts["tm_xattn"])

    n_exp = router_w.shape[1]
    rw = jnp.pad(router_w, ((0, 0), (0, LANES - n_exp))).astype(BF16)
    rb = jnp.pad(router_b, (0, LANES - n_exp)).reshape(1, LANES).astype(F32)
    assert n_exp == N_EXPERTS
    tt = ts["tm_router"]
    nt = t // tt
    hm, idx, rank, wt, cnt = _router(x2, row(moe_norm_g), rw, rb, tt)
    tm = ts["tm_ffn"]
    n8 = (cnt[:, 0, :n_exp].astype(jnp.int32) + SEG_ALIGN - 1) // SEG_ALIGN * SEG_ALIGN
    src_off = jnp.cumsum(n8, axis=1) - n8
    region = jnp.sum(n8, axis=0)
    padded = (region + tm - 1) // tm * tm
    ends = jnp.cumsum(padded)
    dst_off = (ends - padded)[None, :] + jnp.cumsum(n8, axis=0) - n8
    tables = tuple(jnp.concatenate([a, tail[None, :]], axis=0).reshape(-1) for a, tail in
                   ((src_off, jnp.zeros_like(region)), (n8, padded - region), (dst_off, ends - padded + region)))
    offv = jnp.pad(src_off.astype(F32), ((0, 0), (0, LANES - n_exp))).reshape(nt, 1, LANES)
    rows = tt * TOP_K + n_exp * SEG_ALIGN
    n_pad = (t * TOP_K + nt * n_exp * SEG_ALIGN + n_exp * tm + tm - 1) // tm * tm
    n_tiles = n_pad // tm
    tile_ids = jnp.arange(n_tiles, dtype=jnp.int32)
    tile_expert = jnp.sum(((ends // tm)[None, :] <= tile_ids[:, None]).astype(jnp.int32), axis=1)
    tile_expert = jnp.minimum(tile_expert, n_exp - 1)
    n_used = (ends[-1] // tm).reshape(1).astype(jnp.int32)

    xs = _dispatch(tables, n_used, hm, idx, rank, offv, n_pad, tt, rows, tm)
    f = moe_w2.shape[1]
    w1p = _w1_prep(moe_w1.reshape(n_exp * d, 2 * f), ts["tm_w1_prep"]).reshape(n_exp, d, 2 * f)
    b1p = moe_b1.reshape(n_exp, f // LANES, LANES, 2).transpose(0, 1, 3, 2).reshape(n_exp, 1, 2 * f)
    ys = _ffn(tile_expert, n_used, xs, w1p, b1p, moe_w2.astype(BF16), moe_b2.reshape(n_exp, 1, d), tm)
    out = _combine(tables, ys, idx, rank, wt, offv, x2, row(out_norm_g), tt, rows)
    return out.reshape(b, s, d)


def kernel(x, mem, positions, mix_norm_g, w_in, cmp_pe, cmp_w1, cmp_b1, cmp_w2, cmp_b2, hgrn_lb_logits, hgrn_norm_g, w_up_nsa, w_up_hgrn, w_out, xa_norm_g, xa_mem_norm_g, w_xq, w_xkv, w_xo, moe_norm_g, router_w, router_b, moe_w1, moe_b1, moe_w2, moe_b2, final_norm_g):
    depth = w_in.shape[0]
    assert depth == 1, "single-layer block: the final norm is fused into the last layer's combine"
    ts = _tile_sizes(x.shape[1])
    l = 0
    return _layer(x, mem, positions, ts, mix_norm_g[l], w_in[l], cmp_pe[l], cmp_w1[l], cmp_b1[l], cmp_w2[l], cmp_b2[l],
                  hgrn_lb_logits, hgrn_norm_g[l], w_up_nsa[l], w_up_hgrn[l], w_out[l], xa_norm_g[l], xa_mem_norm_g[l],
                  w_xq[l], w_xkv[l], w_xo[l], moe_norm_g[l], router_w[l], router_b[l], moe_w1[l], moe_b1[l], moe_w2[l],
                  moe_b2[l], final_norm_g)
```

```python
import functools

import jax
import jax.numpy as jnp
from jax import lax
from jax.experimental import pallas as pl
from jax.experimental.pallas import tpu as pltpu

EPS = 1e-6
ROPE_THETA = 10000.0
NEG_INF = -1e30
FORCE_SCORE = 1e9

NSA_HEADS = 8
NSA_KV_GROUPS = 2
NSA_Q_PER_GROUP = NSA_HEADS // NSA_KV_GROUPS
NSA_HEAD_DIM = 64
CMP_BLOCK = 32
CMP_STRIDE = 16
CMP_HIDDEN = 256
SLC_BLOCK = 64
SLC_TOPK = 16
N_LOCAL_BLOCKS = 2
WINDOW = 512
NSA_Q_W = NSA_HEADS * NSA_HEAD_DIM
NSA_KV_W = NSA_KV_GROUPS * NSA_HEAD_DIM

HGRN_HEADS = 4
HGRN_DK = 128
HGRN_DV = 128
HGRN_CHUNK = 64
HGRN_SUB = 16
HGRN_W = HGRN_HEADS * HGRN_DK

XA_HEADS = 4
XA_HEAD_DIM = 128
XA_W = XA_HEADS * XA_HEAD_DIM

N_EXPERTS = 32
TOP_K = 4
SWIGLU_ALPHA = 1.702
SWIGLU_LIMIT = 7.0

LANES = 128
SEG_ALIGN = 8
BF16_SUBLANES = 16
LOG2E = 1.4426950408889634
VMEM_LIMIT = 48 * 1024 * 1024

COL_GA = 0
COL_GB = 1024
COL_HQ = 2048
COL_HF = 2560
COL_HI = 3072
COL_HG = 3584
COL_NQ = 4096
COL_KSKW = 4608
COL_KCVC = 4864
COL_VSVW = 5120
COL_NG = 5376
P_WIDTH = 5632

F32 = jnp.float32
BF16 = jnp.bfloat16


def _params(sem):
    return pltpu.CompilerParams(dimension_semantics=sem, vmem_limit_bytes=VMEM_LIMIT)


def _dot(a, b):
    return jnp.dot(a, b, preferred_element_type=F32)


def _dot_nt(a, b):
    return lax.dot_general(a, b, (((1,), (1,)), ((), ())), preferred_element_type=F32)


def _rms(xf, g):
    return xf * lax.rsqrt(jnp.mean(xf * xf, axis=-1, keepdims=True) + EPS) * g


def _sigmoid(x):
    return 1.0 / (1.0 + jnp.exp(-x))


def _norm_matmul_kernel(x_ref, g_ref, w_ref, o_ref, hn_ref):
    @pl.when(pl.program_id(1) == 0)
    def _():
        hn_ref[...] = _rms(x_ref[...], g_ref[...]).astype(BF16)

    o_ref[...] = _dot(hn_ref[...], w_ref[...]).astype(o_ref.dtype)


def _norm_matmul(x, g, w, out_dtype, tm, tn, name):
    t, d = x.shape
    n = w.shape[1]
    return pl.pallas_call(
        _norm_matmul_kernel,
        out_shape=jax.ShapeDtypeStruct((t, n), out_dtype),
        grid=(t // tm, n // tn),
        in_specs=[
            pl.BlockSpec((tm, d), lambda i, j: (i, 0)),
            pl.BlockSpec((1, d), lambda i, j: (0, 0)),
            pl.BlockSpec((d, tn), lambda i, j: (0, j)),
        ],
        out_specs=pl.BlockSpec((tm, tn), lambda i, j: (i, j)),
        scratch_shapes=[pltpu.VMEM((tm, d), BF16)],
        compiler_params=_params(("arbitrary", "arbitrary")),
        name=name,
    )(x, g, w)


def _rope_coeffs(pos_col, invf):
    ang = pos_col.astype(F32) * invf
    lane = lax.broadcasted_iota(jnp.int32, ang.shape, 1)
    first = (lane & (NSA_HEAD_DIM - 1)) < (NSA_HEAD_DIM // 2)
    c = jnp.cos(ang)
    s = jnp.sin(ang)
    return c, jnp.where(first, -s, s), first


def _rope_tile(x, c, s_signed, first):
    half = NSA_HEAD_DIM // 2
    partner = jnp.where(first, pltpu.roll(x, LANES - half, 1), pltpu.roll(x, half, 1))
    return x * c + partner * s_signed


def _rope_kernel(q_ref, k_ref, pos_ref, invf_ref, qo_ref, ko_ref, *, q_scale):
    c, s_signed, first = _rope_coeffs(pos_ref[...], invf_ref[...])
    for i in range(q_ref.shape[1] // LANES):
        sl = slice(i * LANES, (i + 1) * LANES)
        qo_ref[:, sl] = (_rope_tile(q_ref[:, sl], c, s_signed, first) * q_scale).astype(BF16)
    for i in range(k_ref.shape[1] // LANES):
        sl = slice(i * LANES, (i + 1) * LANES)
        ko_ref[:, sl] = _rope_tile(k_ref[:, sl], c, s_signed, first).astype(BF16)


def _rope(p, pos_col, invf, tm):
    t = p.shape[0]
    kw = 2 * NSA_KV_W
    return pl.pallas_call(
        functools.partial(_rope_kernel, q_scale=NSA_HEAD_DIM ** -0.5 * LOG2E),
        out_shape=(jax.ShapeDtypeStruct((t, NSA_Q_W), BF16), jax.ShapeDtypeStruct((t, kw), BF16)),
        grid=(t // tm,),
        in_specs=[
            pl.BlockSpec((tm, NSA_Q_W), lambda i: (i, COL_NQ // NSA_Q_W)),
            pl.BlockSpec((tm, kw), lambda i: (i, COL_KSKW // kw)),
            pl.BlockSpec((tm, 1), lambda i: (i, 0)),
            pl.BlockSpec((1, LANES), lambda i: (0, 0)),
        ],
        out_specs=(
            pl.BlockSpec((tm, NSA_Q_W), lambda i: (i, 0)),
            pl.BlockSpec((tm, kw), lambda i: (i, 0)),
        ),
        compiler_params=_params(("arbitrary",)),
        name="rope",
    )(p, p, pos_col, invf)


def _gelu_tanh(x):
    return 0.5 * x * (1.0 + jnp.tanh(0.7978845608028654 * (x + 0.044715 * (x * x * x))))


def _compress_kernel(r_ref, pe_ref, w1_ref, b1_ref, w2_ref, b2_ref, pos_ref, invf_ref, o_ref):
    nr = r_ref.shape[3]
    half = r_ref.shape[4]
    acc = None
    for g in range(NSA_KV_GROUPS):
        r = r_ref[0, 0, g]
        top = _dot((r + pe_ref[0, 0]).astype(BF16), w1_ref[0, :half, :])
        bot = _dot((r + pe_ref[0, 1]).astype(BF16), w1_ref[0, half:, :])
        pre = top + pltpu.roll(bot, nr - 1, 0) + b1_ref[0]
        part = _dot(_gelu_tanh(pre).astype(BF16), w2_ref[0, g])
        acc = part if acc is None else acc + part
    out = acc + b2_ref[0]
    c, s_signed, first = _rope_coeffs(pos_ref[0], invf_ref[...])
    roped = _rope_tile(out, c, s_signed, first)
    is_key = pl.program_id(0) == 0
    o_ref[0, 0] = jnp.where(is_key, roped, out).astype(BF16)


def _compress(r, pe, w1, b1, w2p, b2t, pos_cmp, invf):
    _, b, g, nr, half = r.shape
    return pl.pallas_call(
        _compress_kernel,
        out_shape=jax.ShapeDtypeStruct((2, b, nr, LANES), BF16),
        grid=(2, b),
        in_specs=[
            pl.BlockSpec((1, 1, g, nr, half), lambda k, i: (k, i, 0, 0, 0)),
            pl.BlockSpec((1, 2, 1, half), lambda k, i: (k, 0, 0, 0)),
            pl.BlockSpec((1, 2 * half, CMP_HIDDEN), lambda k, i: (k, 0, 0)),
            pl.BlockSpec((1, 1, CMP_HIDDEN), lambda k, i: (k, 0, 0)),
            pl.BlockSpec((1, g, CMP_HIDDEN, LANES), lambda k, i: (k, 0, 0, 0)),
            pl.BlockSpec((1, 1, LANES), lambda k, i: (k, 0, 0)),
            pl.BlockSpec((1, nr, 1), lambda k, i: (i, 0, 0)),
            pl.BlockSpec((1, LANES), lambda k, i: (0, 0)),
        ],
        out_specs=pl.BlockSpec((1, 1, nr, LANES), lambda k, i: (k, i, 0, 0)),
        compiler_params=_params(("arbitrary", "arbitrary")),
        name="compress",
    )(r, pe, w1, b1, w2p, b2t, pos_cmp, invf)


def _nsa_kernel(qt_ref, kc_ref, vct_ref, ksa_ref, vst_ref, kw_ref, vwt_ref, g_ref, o_ref, acc_ref, out_ref,
                sa_ref, sb_ref, qa_ref, *, tq, tk, seq):
    hg = NSA_Q_PER_GROUP
    dh = NSA_HEAD_DIM
    nc = kc_ref.shape[2]
    nb = seq // SLC_BLOCK
    top_k = min(SLC_TOPK, nb)
    s0 = pl.program_id(2) * tq
    t_lane = s0 + lax.broadcasted_iota(jnp.int32, (1, tq), 1)
    gate = _sigmoid(g_ref[0, 0, 0])

    def scores(k_tile, bias):
        s = _dot(k_tile, qt_ref[0, 0, 0])
        return jnp.concatenate([s[:, h * tq:(h + 1) * tq] + bias for h in range(hg)], axis=1)

    def normalised(acc):
        return acc[:dh] / acc[dh:dh + 1]

    n_col = lax.broadcasted_iota(jnp.int32, (nc, 1), 0)
    valid_c = (n_col * CMP_STRIDE + (CMP_BLOCK - 1) <= t_lane) & (n_col < nc - 1)
    win_keys = WINDOW + tq
    w0 = pl.multiple_of(jnp.maximum(s0 - WINDOW, 0), tq)
    wpos = w0 + lax.broadcasted_iota(jnp.int32, (win_keys, 1), 0)
    bias_w = jnp.where((wpos <= t_lane) & (wpos > t_lane - WINDOW), 0.0, NEG_INF)
    s_c = scores(kc_ref[0, 0], jnp.where(valid_c, 0.0, NEG_INF))
    s_w = scores(kw_ref[0, 0, pl.ds(w0, win_keys), :], bias_w)

    e_c = jnp.exp2(s_c - jnp.max(s_c, axis=0, keepdims=True))
    t_all = s0 + (lax.broadcasted_iota(jnp.int32, (1, hg * tq), 1) & (tq - 1))
    row_ok = t_all >= CMP_BLOCK - 1
    pn = e_c * jnp.where(row_ok, 1.0 / jnp.sum(e_c, axis=0, keepdims=True), 0.0)
    out_ref[...] = gate[0:1] * _dot(vct_ref[0, 0], pn.astype(BF16))[:dh]
    p_sum = pn[:, 0:tq]
    for h in range(1, hg):
        p_sum = p_sum + pn[:, h * tq:(h + 1) * tq]

    j_col = lax.broadcasted_iota(jnp.int32, (nb, 1), 0)
    n_row = lax.broadcasted_iota(jnp.int32, (1, nc), 1) * CMP_STRIDE
    overlap = (n_row < j_col * SLC_BLOCK + SLC_BLOCK) & (n_row + CMP_BLOCK > j_col * SLC_BLOCK)
    overlap = jnp.where(overlap, 1.0, 0.0).astype(BF16)
    p_hi = p_sum.astype(BF16)
    p_lo = (p_sum - p_hi.astype(F32)).astype(BF16)
    imp = _dot(overlap, p_hi) + _dot(overlap, p_lo)

    p_w = jnp.exp2(s_w - jnp.max(s_w, axis=0, keepdims=True)).astype(BF16)
    out_ref[...] += gate[2:3] * normalised(_dot(vwt_ref[0, 0, :, pl.ds(w0, win_keys)], p_w))

    cur = t_lane >> 6
    causal_b = j_col <= cur
    forced = (j_col == 0) | (causal_b & (j_col > cur - N_LOCAL_BLOCKS))
    score = jnp.where(forced, FORCE_SCORE, jnp.where(causal_b, imp, -1.0))
    j_f = jnp.broadcast_to(j_col.astype(F32), (nb, tq))
    for _ in range(top_k):
        mx = jnp.max(score, axis=0, keepdims=True)
        first_idx = jnp.min(jnp.where(score == mx, j_f, float(nb)), axis=0, keepdims=True)
        score = jnp.where(j_f == first_idx, -jnp.inf, score)
    sel = jnp.where(causal_b & (score == -jnp.inf), 1.0, 0.0)

    sel_bias = ((sel - 1.0) * (-NEG_INF)).astype(BF16)
    qa_ref[:dh] = qt_ref[0, 0, 0]
    qa_ref[dh:] = jnp.concatenate([sel_bias] * hg, axis=1)
    k_col = lax.broadcasted_iota(jnp.int32, (tk, 1), 0)

    def put_scores(k0, buf_ref):
        buf_ref[...] = _dot(ksa_ref[0, 0, pl.ds(pl.multiple_of(k0, tk), tk), :], qa_ref[...])

    def consume(k0, buf_ref, m_old, causal):
        s = buf_ref[...]
        if causal:
            bias = jnp.where(k0 + k_col <= t_lane, 0.0, NEG_INF)
            s = jnp.concatenate([s[:, h * tq:(h + 1) * tq] + bias for h in range(hg)], axis=1)
        m_new = jnp.maximum(m_old, jnp.max(s, axis=0, keepdims=True))
        pv = _dot(vst_ref[0, 0, :, pl.ds(pl.multiple_of(k0, tk), tk)], jnp.exp2(s - m_new).astype(BF16))
        acc_ref[...] = jnp.exp2(m_old - m_new) * acc_ref[...] + pv
        return m_new

    def slc_pair(it, m):
        k0 = it * (2 * tk)
        put_scores(k0 + tk, sb_ref)
        m = consume(k0, sa_ref, m, False)
        put_scores(k0 + 2 * tk, sa_ref)
        return consume(k0 + tk, sb_ref, m, False)

    acc_ref[...] = jnp.zeros_like(acc_ref)
    put_scores(jnp.int32(0), sa_ref)
    d_tile = s0 // tk
    n_full = d_tile // 2
    m_s = lax.fori_loop(0, n_full, slc_pair, jnp.full((1, hg * tq), NEG_INF, F32))
    e0 = n_full * (2 * tk)

    @pl.when(d_tile % 2 == 1)
    def _():
        put_scores(e0 + tk, sb_ref)
        consume(e0 + tk, sb_ref, consume(e0, sa_ref, m_s, False), True)

    @pl.when(d_tile % 2 == 0)
    def _():
        consume(e0, sa_ref, m_s, True)

    o_ref[0, 0, 0] = (out_ref[...] + gate[1:2] * normalised(acc_ref[...])).astype(o_ref.dtype)


def _nsa(qt, kc, vct, ksa, vst, kw, vwt, gates, tq, tk):
    b, g, nq, dh, lanes = qt.shape
    hg = lanes // tq
    s = nq * tq
    nr = kc.shape[2]
    vr = vst.shape[2]
    k_spec = pl.BlockSpec((1, 1, s, dh), lambda i, j, k: (i, j, 0, 0))
    vt_spec = pl.BlockSpec((1, 1, vr, s), lambda i, j, k: (i, j, 0, 0))
    return pl.pallas_call(
        functools.partial(_nsa_kernel, tq=tq, tk=tk, seq=s),
        out_shape=jax.ShapeDtypeStruct((b, g, nq, dh, hg * tq), BF16),
        grid=(b, g, nq),
        in_specs=[
            pl.BlockSpec((1, 1, 1, dh, hg * tq), lambda i, j, k: (i, j, k, 0, 0)),
            pl.BlockSpec((1, 1, nr, dh), lambda i, j, k: (i, j, 0, 0)),
            pl.BlockSpec((1, 1, vr, nr), lambda i, j, k: (i, j, 0, 0)),
            pl.BlockSpec((1, 1, s, ksa.shape[3]), lambda i, j, k: (i, j, 0, 0)), vt_spec, k_spec, vt_spec,
            pl.BlockSpec((1, 1, 1, 3, hg * tq), lambda i, j, k: (i, j, k, 0, 0)),
        ],
        out_specs=pl.BlockSpec((1, 1, 1, dh, hg * tq), lambda i, j, k: (i, j, k, 0, 0)),
        scratch_shapes=[pltpu.VMEM((vr, hg * tq), F32), pltpu.VMEM((dh, hg * tq), F32),
                        pltpu.VMEM((tk, hg * tq), F32), pltpu.VMEM((tk, hg * tq), F32),
                        pltpu.VMEM((ksa.shape[3], hg * tq), BF16)],
        compiler_params=_params(("arbitrary", "arbitrary", "arbitrary")),
        name="nsa",
    )(qt, kc, vct, ksa, vst, kw, vwt, gates)


def _cumsum_rows(x):
    n = x.shape[0]
    row = lax.broadcasted_iota(jnp.int32, x.shape, 0)
    d = 1
    while d < n:
        x = x + jnp.where(row >= d, pltpu.roll(x, d, 0), 0.0)
        d *= 2
    return x


def _hgrn_kernel(q_ref, f_ref, i_ref, g_ref, lb_ref, gn_ref, o_ref, st_ref, *, n_chunks):
    @pl.when(pl.program_id(1) == 0)
    def _():
        st_ref[...] = jnp.zeros_like(st_ref)

    c_len = HGRN_CHUNK
    sub = HGRN_SUB
    lbl = lb_ref[...]
    lb_e = jnp.exp(lbl - jnp.max(lbl, axis=0, keepdims=True))
    lb_all = lb_e[0:1] / jnp.sum(lb_e, axis=0, keepdims=True)

    items = [(c, h) for c in range(n_chunks) for h in range(HGRN_HEADS)]
    wave1 = {}
    for c, h in items:
        rs = slice(c * c_len, (c + 1) * c_len)
        ls = slice(h * HGRN_DK, (h + 1) * HGRN_DK)
        lb = lb_all[:, ls]
        f = lb + (1.0 - lb) * _sigmoid(f_ref[rs, ls])
        k = 1.0 - f
        b = _cumsum_rows(jnp.log(f))
        q = q_ref[rs, ls]
        v32 = i_ref[rs, ls]
        b_end = b[c_len - 1:c_len]
        attn = []
        for blk in range(c_len // sub):
            lo, hi = blk * sub, (blk + 1) * sub
            mid = lo + sub // 2
            beta = b[mid - 1:mid]
            qd = (q[lo:hi] * jnp.exp(b[lo:hi] - beta)).astype(BF16)
            kd = (k[:hi] * jnp.exp(beta - b[:hi])).astype(BF16)
            attn.append(_dot_nt(qd, kd))
        update = _dot(v32.T.astype(BF16), (k * jnp.exp(b_end - b)).astype(BF16))
        wave1[c, h] = ((q * jnp.exp(b)).astype(BF16), jnp.exp(b_end), update, attn)

    o_inter = {}
    for h in range(HGRN_HEADS):
        st = st_ref[h]
        for c in range(n_chunks):
            q_dec, decay, update, _ = wave1[c, h]
            o_inter[c, h] = _dot_nt(q_dec, st.astype(BF16))
            st = st * decay + update
        st_ref[h] = st

    for c, h in items:
        rs = slice(c * c_len, (c + 1) * c_len)
        ls = slice(h * HGRN_DK, (h + 1) * HGRN_DK)
        v = i_ref[rs, ls].astype(BF16)
        pieces = []
        for blk, a in enumerate(wave1[c, h][3]):
            lo, hi = blk * sub, (blk + 1) * sub
            ti = lax.broadcasted_iota(jnp.int32, (sub, hi), 0)
            si = lax.broadcasted_iota(jnp.int32, (sub, hi), 1)
            pieces.append(_dot(jnp.where(si <= ti + lo, a, 0.0).astype(BF16), v[:hi]))
        o = o_inter[c, h] + jnp.concatenate(pieces, axis=0)
        gate = g_ref[rs, ls]
        o_ref[rs, ls] = (_rms(o, gn_ref[...]) * (gate * _sigmoid(gate))).astype(o_ref.dtype)


def _hgrn(p, lb_logits, gn, batch, seq, n_chunks):
    t = p.shape[0]
    rows = n_chunks * HGRN_CHUNK
    steps = seq // rows

    def col(cb):
        return pl.BlockSpec((rows, HGRN_W), lambda i, j: (i * steps + j, cb))

    return pl.pallas_call(
        functools.partial(_hgrn_kernel, n_chunks=n_chunks),
        out_shape=jax.ShapeDtypeStruct((t, HGRN_W), BF16),
        grid=(batch, steps),
        in_specs=[
            col(COL_HQ // HGRN_W), col(COL_HF // HGRN_W), col(COL_HI // HGRN_W), col(COL_HG // HGRN_W),
            pl.BlockSpec(lb_logits.shape, lambda i, j: (0, 0)),
            pl.BlockSpec((1, HGRN_DV), lambda i, j: (0, 0)),
        ],
        out_specs=pl.BlockSpec((rows, HGRN_W), lambda i, j: (i * steps + j, 0)),
        scratch_shapes=[pltpu.VMEM((HGRN_HEADS, HGRN_DV, HGRN_DK), F32)],
        compiler_params=_params(("arbitrary", "arbitrary")),
        name="hgrn",
    )(p, p, p, p, lb_logits, gn)


def _merge_kernel(x_ref, ga_ref, gb_ref, yn_ref, yh_ref, wn_ref, wh_ref, wo_ref, o_ref):
    mixed = _sigmoid(ga_ref[...]) * _dot(yn_ref[...], wn_ref[...]) + _sigmoid(gb_ref[...]) * _dot(yh_ref[...], wh_ref[...])
    o_ref[...] = x_ref[...] + _dot(mixed.astype(BF16), wo_ref[...])


def _merge(x, p, y_nsa, y_hgrn, wn, wh, wo, tm):
    t, d = x.shape
    full = lambda a: pl.BlockSpec(a.shape, lambda i: (0, 0))
    return pl.pallas_call(
        _merge_kernel,
        out_shape=jax.ShapeDtypeStruct((t, d), F32),
        grid=(t // tm,),
        in_specs=[
            pl.BlockSpec((tm, d), lambda i: (i, 0)),
            pl.BlockSpec((tm, d), lambda i: (i, COL_GA // d)),
            pl.BlockSpec((tm, d), lambda i: (i, COL_GB // d)),
            pl.BlockSpec((tm, NSA_Q_W), lambda i: (i, 0)),
            pl.BlockSpec((tm, HGRN_W), lambda i: (i, 0)),
            full(wn), full(wh), full(wo),
        ],
        out_specs=pl.BlockSpec((tm, d), lambda i: (i, 0)),
        compiler_params=_params(("arbitrary",)),
        name="merge",
    )(x, p, p, y_nsa, y_hgrn, wn, wh, wo)


def _xattn_kernel(x_ref, g_ref, wq_ref, kv_ref, wo_ref, o_ref):
    x = x_ref[...]
    xq = _dot(_rms(x, g_ref[...]).astype(BF16), wq_ref[...]).astype(BF16)
    heads = [slice(h * XA_HEAD_DIM, (h + 1) * XA_HEAD_DIM) for h in range(XA_HEADS)]
    scores = [_dot_nt(xq[:, ls], kv_ref[0, :, ls]) * (XA_HEAD_DIM ** -0.5) for ls in heads]
    outs = []
    for h, s in enumerate(scores):
        e = jnp.exp(s - jnp.max(s, axis=-1, keepdims=True))
        p = e / jnp.sum(e, axis=-1, keepdims=True)
        outs.append(_dot(p.astype(BF16), kv_ref[0, :, XA_W + h * XA_HEAD_DIM:XA_W + (h + 1) * XA_HEAD_DIM]))
    o_x = jnp.concatenate(outs, axis=-1)
    o_ref[...] = x + _dot(o_x.astype(BF16), wo_ref[...])


def _xattn(x, g, wq, kv, wo, seq, tm):
    t, d = x.shape
    steps = seq // tm
    full = lambda a: pl.BlockSpec(a.shape, lambda i: (0, 0))
    return pl.pallas_call(
        _xattn_kernel,
        out_shape=jax.ShapeDtypeStruct((t, d), F32),
        grid=(t // tm,),
        in_specs=[
            pl.BlockSpec((tm, d), lambda i: (i, 0)),
            full(g), full(wq),
            pl.BlockSpec((1,) + kv.shape[1:], lambda i: (i // steps, 0, 0)),
            full(wo),
        ],
        out_specs=pl.BlockSpec((tm, d), lambda i: (i, 0)),
        compiler_params=_params(("arbitrary",)),
        name="xattn",
    )(x, g, wq, kv, wo)


def _router_kernel(x_ref, g_ref, w_ref, b_ref, hm_ref, idx_ref, rank_ref, wt_ref, cnt_ref):
    tm = x_ref.shape[0]
    hm = _rms(x_ref[...], g_ref[...]).astype(BF16)
    hm_ref[...] = hm
    lane = lax.broadcasted_iota(jnp.int32, (tm, LANES), 1)
    lane_f = lane.astype(F32)
    logits = _dot(hm, w_ref[...]) + b_ref[...]
    logits = jnp.where(lane < N_EXPERTS, logits, -jnp.inf)
    picks, vals = [], []
    onehot_all = jnp.zeros((tm, LANES), F32)
    for _ in range(TOP_K):
        mx = jnp.max(logits, axis=-1, keepdims=True)
        first_idx = jnp.min(jnp.where(logits == mx, lane_f, float(LANES)), axis=-1, keepdims=True)
        hit = lane_f == first_idx
        onehot = jnp.where(hit, 1.0, 0.0)
        logits = jnp.where(hit, -jnp.inf, logits)
        picks.append((first_idx, onehot))
        vals.append(mx)
        onehot_all = onehot_all + onehot
    exps = [jnp.exp(v - vals[0]) for v in vals]
    den = exps[0]
    for e in exps[1:]:
        den = den + e
    r_i = lax.broadcasted_iota(jnp.int32, (tm, tm), 0)
    c_i = lax.broadcasted_iota(jnp.int32, (tm, tm), 1)
    lower = jnp.where(c_i < r_i, 1.0, 0.0).astype(BF16)
    before = _dot(lower, onehot_all.astype(BF16))
    idx_out = jnp.zeros((tm, LANES), F32)
    rank_out = jnp.zeros((tm, LANES), F32)
    wt_out = jnp.zeros((tm, LANES), F32)
    for k in range(TOP_K):
        first_idx, onehot = picks[k]
        rank = jnp.sum(onehot * before, axis=-1, keepdims=True)
        idx_out = jnp.where(lane == k, first_idx, idx_out)
        rank_out = jnp.where(lane == k, rank, rank_out)
        wt_out = jnp.where(lane == k, exps[k] / den, wt_out)
    idx_ref[...] = idx_out.astype(jnp.int32)
    rank_ref[...] = rank_out.astype(jnp.int32)
    wt_ref[...] = wt_out
    cnt_ref[0] = jnp.sum(onehot_all, axis=0, keepdims=True)


def _router(x, g, w, b, tm):
    t, d = x.shape
    full = lambda a: pl.BlockSpec(a.shape, lambda i: (0, 0))
    lane_out = pl.BlockSpec((tm, LANES), lambda i: (i, 0))
    return pl.pallas_call(
        _router_kernel,
        out_shape=(
            jax.ShapeDtypeStruct((t, d), BF16),
            jax.ShapeDtypeStruct((t, LANES), jnp.int32),
            jax.ShapeDtypeStruct((t, LANES), jnp.int32),
            jax.ShapeDtypeStruct((t, LANES), F32),
            jax.ShapeDtypeStruct((t // tm, 1, LANES), F32),
        ),
        grid=(t // tm,),
        in_specs=[pl.BlockSpec((tm, d), lambda i: (i, 0)), full(g), full(w), full(b)],
        out_specs=(pl.BlockSpec((tm, d), lambda i: (i, 0)), lane_out, lane_out, lane_out,
                   pl.BlockSpec((1, 1, LANES), lambda i: (i, 0, 0))),
        compiler_params=_params(("arbitrary",)),
        name="router",
    )(x, g, w, b)


def _slot_matrix(idx_ref, rank_ref, offv_ref, rows, values=None):
    tt = idx_ref.shape[0]
    lane = lax.broadcasted_iota(jnp.int32, (tt, LANES), 1)
    r = lax.broadcasted_iota(jnp.int32, (tt, rows), 1)
    offv = offv_ref[0]
    idx = idx_ref[...]
    rank = rank_ref[...]
    onehot = jnp.zeros((tt, rows), F32)
    weighted = jnp.zeros((tt, rows), F32)
    for k in range(TOP_K):
        seg = jnp.sum(jnp.where(lane == idx[:, k:k + 1], offv, 0.0), axis=-1, keepdims=True)
        hit = r == seg.astype(jnp.int32) + rank[:, k:k + 1]
        onehot = jnp.where(hit, 1.0, onehot)
        if values is not None:
            weighted = jnp.where(hit, values[:, k:k + 1], weighted)
    return onehot if values is None else (onehot, weighted)


def _segment_copies(src_ref, n8_ref, dst_ref, make_copy, tile_tokens, table_row=None):
    base = (pl.program_id(0) if table_row is None else table_row) * N_EXPERTS
    sizes = []
    size = tile_tokens
    while size >= SEG_ALIGN:
        sizes.append(size)
        size //= 2

    def visit(e, start):
        n8 = n8_ref[base + e]
        src = src_ref[base + e]
        dst = dst_ref[base + e]
        for size in sizes:
            done = n8 & (-2 * size)

            @pl.when((n8 & size) != 0)
            def _():
                cp = make_copy(pl.multiple_of(src + done, SEG_ALIGN), pl.multiple_of(dst + done, SEG_ALIGN), size)
                if start:
                    cp.start()
                else:
                    cp.wait()

    def start_all(e, c):
        visit(e, True)
        return c

    def wait_all(e, c):
        visit(e, False)
        return c

    return start_all, wait_all


def _dispatch_kernel(src_ref, n8_ref, dst_ref, nu_ref, hm_ref, idx_ref, rank_ref, wt_ref, offv_ref, xs_ref, buf_ref,
                     sem, *, tm, n_steps):
    rows = buf_ref.shape[1]
    d = hm_ref.shape[1]
    tt = hm_ref.shape[0]
    n_tiles = xs_ref.shape[0] // tm
    step = pl.program_id(0)
    slot = step % 2

    def copies(table_row, buf_slot, tile_tokens=tt):
        def make_copy(src, dst, size):
            return pltpu.make_async_copy(buf_ref.at[buf_slot, pl.ds(src, size)], xs_ref.at[pl.ds(dst, size)],
                                         sem.at[buf_slot])
        return _segment_copies(src_ref, n8_ref, dst_ref, make_copy, tile_tokens, table_row) + (make_copy,)

    @pl.when(step >= 2)
    def _():
        lax.fori_loop(0, N_EXPERTS, copies(step - 2, slot)[1], 0)

    onehot, weighted = _slot_matrix(idx_ref, rank_ref, offv_ref, rows, values=wt_ref[...])
    tn = (((0,), (0,)), ((), ()))
    buf_ref[slot, :, :d] = lax.dot_general(onehot.astype(BF16), hm_ref[...], tn, preferred_element_type=F32)
    w_hi = weighted.astype(BF16)
    w_lo = (weighted - w_hi.astype(F32)).astype(BF16)
    ones = jnp.ones((tt, LANES), BF16)
    buf_ref[slot, :, d:] = (lax.dot_general(w_hi, ones, tn, preferred_element_type=F32)
                            + lax.dot_general(w_lo, ones, tn, preferred_element_type=F32))
    start_cur, wait_cur, make_copy = copies(step, slot)
    lax.fori_loop(0, N_EXPERTS, start_cur, 0)

    @pl.when(step == n_steps - 1)
    def _():
        if n_steps >= 2:
            lax.fori_loop(0, N_EXPERTS, copies(step - 1, 1 - slot)[1], 0)
        lax.fori_loop(0, N_EXPERTS, wait_cur, 0)
        buf_ref[slot, :tm] = jnp.zeros((tm, buf_ref.shape[2]), F32)
        start_tail, wait_tail, _ = copies(n_steps, slot, tm)
        lax.fori_loop(0, N_EXPERTS, start_tail, 0)

        def zero_tile(i):
            return make_copy(0, pl.multiple_of(i * tm, tm), tm)

        lax.fori_loop(nu_ref[0], n_tiles, lambda i, c: (zero_tile(i).start(), c)[1], 0)
        lax.fori_loop(0, N_EXPERTS, wait_tail, 0)
        lax.fori_loop(nu_ref[0], n_tiles, lambda i, c: (zero_tile(i).wait(), c)[1], 0)


def _dispatch(tables, n_used, hm, idx, rank, wt, offv, n_pad, tt, rows, tm):
    t, d = hm.shape
    tile = lambda w: pl.BlockSpec((tt, w), lambda i, *_: (i, 0))
    grid_spec = pltpu.PrefetchScalarGridSpec(
        num_scalar_prefetch=4,
        grid=(t // tt,),
        in_specs=[tile(d), tile(LANES), tile(LANES), tile(LANES),
                  pl.BlockSpec((1, 1, LANES), lambda i, *_: (i, 0, 0))],
        out_specs=pl.BlockSpec(memory_space=pl.ANY),
        scratch_shapes=[pltpu.VMEM((2, rows, d + LANES), F32), pltpu.SemaphoreType.DMA((2,))],
    )
    return pl.pallas_call(
        functools.partial(_dispatch_kernel, tm=tm, n_steps=t // tt),
        out_shape=jax.ShapeDtypeStruct((n_pad, d + LANES), F32),
        grid_spec=grid_spec,
        compiler_params=_params(("arbitrary",)),
        name="dispatch",
    )(*tables, n_used, hm, idx, rank, wt, offv)


def _w1_prep_kernel(w_ref, o_ref):
    grp = 2 * LANES
    r_i = lax.broadcasted_iota(jnp.int32, (grp, grp), 0)
    c_i = lax.broadcasted_iota(jnp.int32, (grp, grp), 1)
    src_col = jnp.where(c_i < LANES, 2 * c_i, 2 * (c_i - LANES) + 1)
    perm = jnp.where(r_i == src_col, 1.0, 0.0).astype(BF16)
    for c in range(w_ref.shape[1] // grp):
        sl = slice(c * grp, (c + 1) * grp)
        o_ref[:, sl] = _dot(w_ref[:, sl].astype(BF16), perm).astype(BF16)


def _w1_prep(w, tm):
    r, n = w.shape
    return pl.pallas_call(
        _w1_prep_kernel,
        out_shape=jax.ShapeDtypeStruct((r, n), BF16),
        grid=(r // tm,),
        in_specs=[pl.BlockSpec((tm, n), lambda i: (i, 0))],
        out_specs=pl.BlockSpec((tm, n), lambda i: (i, 0)),
        compiler_params=_params(("arbitrary",)),
        name="w1_prep",
    )(w)


def _ffn_kernel(te_ref, nu_ref, x_ref, w1_ref, b1_ref, w2_ref, b2_ref, o_ref):
    del te_ref
    used = pl.program_id(0) < nu_ref[0]

    @pl.when(used)
    def _():
        d = o_ref.shape[1]
        u = _dot(x_ref[:, :d].astype(BF16), w1_ref[0]) + b1_ref[0]
        acts = []
        for c in range(u.shape[1] // (2 * LANES)):
            glu = jnp.minimum(u[:, 2 * c * LANES:(2 * c + 1) * LANES], SWIGLU_LIMIT)
            lin = jnp.clip(u[:, (2 * c + 1) * LANES:(2 * c + 2) * LANES], -SWIGLU_LIMIT, SWIGLU_LIMIT)
            acts.append((glu * _sigmoid(SWIGLU_ALPHA * glu) * (lin + 1.0)).astype(BF16))
        y = _dot(jnp.concatenate(acts, axis=-1), w2_ref[0]) + b2_ref[0]
        weight = x_ref[:, d:]
        o_ref[...] = jnp.concatenate([y[:, c * LANES:(c + 1) * LANES] * weight for c in range(d // LANES)], axis=1)

    @pl.when(jnp.logical_not(used))
    def _():
        o_ref[...] = jnp.zeros_like(o_ref)


def _ffn(tile_expert, n_used, xs, w1, b1, w2, b2, tm):
    n_pad = xs.shape[0]
    d = w1.shape[1]
    f2 = w1.shape[2]
    f = w2.shape[1]
    grid_spec = pltpu.PrefetchScalarGridSpec(
        num_scalar_prefetch=2,
        grid=(n_pad // tm,),
        in_specs=[
            pl.BlockSpec((tm, xs.shape[1]), lambda i, te, nu: (jnp.minimum(i, nu[0] - 1), 0)),
            pl.BlockSpec((1, d, f2), lambda i, te, nu: (te[i], 0, 0)),
            pl.BlockSpec((1, 1, f2), lambda i, te, nu: (te[i], 0, 0)),
            pl.BlockSpec((1, f, d), lambda i, te, nu: (te[i], 0, 0)),
            pl.BlockSpec((1, 1, d), lambda i, te, nu: (te[i], 0, 0)),
        ],
        out_specs=pl.BlockSpec((tm, d), lambda i, te, nu: (i, 0)),
    )
    return pl.pallas_call(
        _ffn_kernel,
        out_shape=jax.ShapeDtypeStruct((n_pad, d), F32),
        grid_spec=grid_spec,
        compiler_params=_params(("arbitrary",)),
        name="expert_ffn",
    )(tile_expert, n_used, xs, w1, b1, w2, b2)


def _combine_kernel(src_ref, n8_ref, dst_ref, ys_ref, idx_ref, rank_ref, offv_ref, x_ref, g_ref, o_ref,
                    buf_ref, sem, *, n_steps):
    rows = buf_ref.shape[1]
    step = pl.program_id(0)
    slot = step % 2

    def copies(table_row, buf_slot):
        def make_copy(src, dst, size):
            return pltpu.make_async_copy(ys_ref.at[pl.ds(dst, size)], buf_ref.at[buf_slot, pl.ds(src, size)],
                                         sem.at[buf_slot])
        return _segment_copies(src_ref, n8_ref, dst_ref, make_copy, x_ref.shape[0], table_row)

    @pl.when(step == 0)
    def _():
        buf_ref[...] = jnp.zeros_like(buf_ref)
        lax.fori_loop(0, N_EXPERTS, copies(0, 0)[0], 0)

    @pl.when(step + 1 < n_steps)
    def _():
        lax.fori_loop(0, N_EXPERTS, copies(step + 1, 1 - slot)[0], 0)

    onehot = _slot_matrix(idx_ref, rank_ref, offv_ref, rows).astype(BF16)
    lax.fori_loop(0, N_EXPERTS, copies(step, slot)[1], 0)
    ys = buf_ref[slot]
    y_hi = ys.astype(BF16)
    y_lo = (ys - y_hi.astype(F32)).astype(BF16)
    y = _dot(onehot, y_hi) + _dot(onehot, y_lo)
    o_ref[...] = _rms(x_ref[...] + y, g_ref[...])


def _combine(tables, ys, idx, rank, offv, x, g, tt, rows):
    t, d = x.shape
    tile = lambda w: pl.BlockSpec((tt, w), lambda i, *_: (i, 0))
    grid_spec = pltpu.PrefetchScalarGridSpec(
        num_scalar_prefetch=3,
        grid=(t // tt,),
        in_specs=[pl.BlockSpec(memory_space=pl.ANY), tile(LANES), tile(LANES),
                  pl.BlockSpec((1, 1, LANES), lambda i, *_: (i, 0, 0)), tile(d),
                  pl.BlockSpec((1, d), lambda i, *_: (0, 0))],
        out_specs=tile(d),
        scratch_shapes=[pltpu.VMEM((2, rows, d), F32), pltpu.SemaphoreType.DMA((2,))],
    )
    return pl.pallas_call(
        functools.partial(_combine_kernel, n_steps=t // tt),
        out_shape=jax.ShapeDtypeStruct((t, d), F32),
        grid_spec=grid_spec,
        compiler_params=_params(("arbitrary",)),
        name="combine",
    )(*tables, ys, idx, rank, offv, x, g)


def _tile_sizes(seq):
    return dict(
        tm_proj=1024, tn_proj=P_WIDTH // 4,
        tm_rope=512,
        tq=256, tk=512,
        hgrn_chunks=4,
        tm_merge=512, tm_xattn=512,
        tm_router=512,
        tm_w1_prep=512,
        tm_ffn=512,
    )


def _layer(x, mem, positions, ts, mix_norm_g, w_in, cmp_pe, cmp_w1, cmp_b1, cmp_w2, cmp_b2, lb_logits, hgrn_norm_g,
           w_up_nsa, w_up_hgrn, w_out, xa_norm_g, xa_mem_norm_g, w_xq, w_xkv, w_xo, moe_norm_g, router_w, router_b,
           moe_w1, moe_b1, moe_w2, moe_b2, out_norm_g):
    b, s, d = x.shape
    t = b * s
    g, hg, dh = NSA_KV_GROUPS, NSA_Q_PER_GROUP, NSA_HEAD_DIM
    x2 = x.reshape(t, d)
    row = lambda v: v.reshape(1, -1).astype(F32)

    splits = [0]
    for w in (d, d, NSA_Q_W) + (NSA_KV_W,) * 6 + (3 * NSA_HEADS,) + (HGRN_W,) * 4:
        splits.append(splits[-1] + w)
    seg = lambda i: w_in[:, splits[i]:splits[i + 1]]
    (ga, gb, nq, kc, vc, ks, vs, kw, vw, ng, hq, hf, hi, hgate) = [seg(i) for i in range(14)]
    pad = jnp.zeros((d, P_WIDTH - COL_NG - 3 * NSA_HEADS), w_in.dtype)
    w_p = jnp.concatenate([ga, gb, hq, hf, hi, hgate, nq, ks, kw, kc, vc, vs, vw, ng, pad], axis=1).astype(BF16)

    p = _norm_matmul(x2, row(mix_norm_g), w_p, F32, ts["tm_proj"], ts["tn_proj"], "in_proj")

    half = dh // 2
    inv_freq = ROPE_THETA ** (-jnp.arange(half, dtype=F32) / half)
    invf = jnp.tile(inv_freq, LANES // half).reshape(1, LANES)
    q_r, kk_r = _rope(p, positions.reshape(t, 1), invf, ts["tm_rope"])
    tq = ts["tq"]
    nq = s // tq
    qt = q_r.reshape(b, nq, tq, g, hg, dh).transpose(0, 3, 1, 5, 4, 2).reshape(b, g, nq, dh, hg * tq)
    kk = kk_r.reshape(b, s, 2 * g, dh).transpose(0, 2, 1, 3)
    k_slc, k_win = kk[:, :g], kk[:, g:]
    nb = s // SLC_BLOCK
    block_onehot = (jnp.arange(s)[:, None] // SLC_BLOCK == jnp.arange(nb)[None, :]).astype(BF16)
    ks_aug = jnp.concatenate([k_slc, jnp.broadcast_to(block_onehot, (b, g, s, nb))], axis=-1)

    def values_t(v):
        vt = v.transpose(0, 1, 3, 2).astype(BF16)
        ones = jnp.ones(vt.shape[:2] + (1, vt.shape[3]), BF16)
        zeros = jnp.zeros(vt.shape[:2] + (BF16_SUBLANES - 1, vt.shape[3]), BF16)
        return jnp.concatenate([vt, ones, zeros], axis=2)

    vvt = values_t(p[:, COL_VSVW:COL_VSVW + 2 * NSA_KV_W].reshape(b, s, 2 * g, dh).transpose(0, 2, 1, 3))
    vt_slc, vt_win = vvt[:, :g], vvt[:, g:]
    gates = p[:, COL_NG:COL_NG + 3 * NSA_HEADS].reshape(b, nq, tq, g, hg, 3).transpose(0, 3, 1, 5, 4, 2)
    gates = gates.reshape(b, g, nq, 3, hg * tq)

    nr = s // CMP_STRIDE
    kcvc = p[:, COL_KCVC:COL_KCVC + 2 * NSA_KV_W].reshape(b, s, 2, g, dh).transpose(2, 0, 3, 1, 4)
    r = kcvc.reshape(2, b, g, nr, CMP_STRIDE * dh)
    pe = cmp_pe.reshape(2, 2, 1, CMP_STRIDE * dh)
    zeros_w2 = jnp.zeros_like(cmp_w2)
    w2p = jnp.stack([jnp.concatenate([cmp_w2, zeros_w2], axis=-1),
                     jnp.concatenate([zeros_w2, cmp_w2], axis=-1)], axis=1).astype(BF16)
    b2t = jnp.tile(cmp_b2, (1, g)).reshape(2, 1, LANES)
    pos_cmp = positions[:, CMP_BLOCK - 1::CMP_STRIDE]
    pos_cmp = jnp.pad(pos_cmp, ((0, 0), (0, nr - pos_cmp.shape[1]))).reshape(b, nr, 1)
    cmp = _compress(r, pe, cmp_w1.astype(BF16), cmp_b1.reshape(2, 1, CMP_HIDDEN), w2p, b2t, pos_cmp, invf)
    cmp = cmp.reshape(2, b, nr, g, dh).transpose(0, 1, 3, 2, 4)

    y_nsa = _nsa(qt, cmp[0], values_t(cmp[1]), ks_aug, vt_slc, k_win, vt_win, gates, tq, ts["tk"])
    y_nsa = y_nsa.reshape(b, g, nq, dh, hg, tq).transpose(0, 2, 5, 1, 4, 3).reshape(t, NSA_Q_W)

    y_hgrn = _hgrn(p, lb_logits.astype(F32), row(hgrn_norm_g), b, s, ts["hgrn_chunks"])

    x2 = _merge(x2, p, y_nsa, y_hgrn, w_up_nsa.astype(BF16), w_up_hgrn.astype(BF16), w_out.astype(BF16),
                ts["tm_merge"])

    n_mem = mem.shape[1]
    kv = _norm_matmul(mem.reshape(b * n_mem, d), row(xa_mem_norm_g), w_xkv.astype(BF16), BF16,
                      n_mem, 2 * XA_W, "mem_kv").reshape(b, n_mem, 2 * XA_W)
    x2 = _xattn(x2, row(xa_norm_g), w_xq.astype(BF16), kv, w_xo.astype(BF16), s, ts["tm_xattn"])

    n_exp = router_w.shape[1]
    rw = jnp.pad(router_w, ((0, 0), (0, LANES - n_exp))).astype(BF16)
    rb = jnp.pad(router_b, (0, LANES - n_exp)).reshape(1, LANES).astype(F32)
    assert n_exp == N_EXPERTS
    tt = ts["tm_router"]
    nt = t // tt
    hm, idx, rank, wt, cnt = _router(x2, row(moe_norm_g), rw, rb, tt)
    tm = ts["tm_ffn"]
    n8 = (cnt[:, 0, :n_exp].astype(jnp.int32) + SEG_ALIGN - 1) // SEG_ALIGN * SEG_ALIGN
    src_off = jnp.cumsum(n8, axis=1) - n8
    region = jnp.sum(n8, axis=0)
    padded = (region + tm - 1) // tm * tm
    ends = jnp.cumsum(padded)
    dst_off = (ends - padded)[None, :] + jnp.cumsum(n8, axis=0) - n8
    tables = tuple(jnp.concatenate([a, tail[None, :]], axis=0).reshape(-1) for a, tail in
                   ((src_off, jnp.zeros_like(region)), (n8, padded - region), (dst_off, ends - padded + region)))
    offv = jnp.pad(src_off.astype(F32), ((0, 0), (0, LANES - n_exp))).reshape(nt, 1, LANES)
    rows = tt * TOP_K + n_exp * SEG_ALIGN
    n_pad = (t * TOP_K + nt * n_exp * SEG_ALIGN + n_exp * tm + tm - 1) // tm * tm
    n_tiles = n_pad // tm
    tile_ids = jnp.arange(n_tiles, dtype=jnp.int32)
    tile_expert = jnp.sum(((ends // tm)[None, :] <= tile_ids[:, None]).astype(jnp.int32), axis=1)
    tile_expert = jnp.minimum(tile_expert, n_exp - 1)
    n_used = (ends[-1] // tm).reshape(1).astype(jnp.int32)

    xs = _dispatch(tables, n_used, hm, idx, rank, wt, offv, n_pad, tt, rows, tm)
    f = moe_w2.shape[1]
    w1p = _w1_prep(moe_w1.reshape(n_exp * d, 2 * f), ts["tm_w1_prep"]).reshape(n_exp, d, 2 * f)
    b1p = moe_b1.reshape(n_exp, f // LANES, LANES, 2).transpose(0, 1, 3, 2).reshape(n_exp, 1, 2 * f)
    ys = _ffn(tile_expert, n_used, xs, w1p, b1p, moe_w2.astype(BF16), moe_b2.reshape(n_exp, 1, d), tm)
    out = _combine(tables, ys, idx, rank, offv, x2, row(out_norm_g), tt, rows)
    return out.reshape(b, s, d)


def kernel(x, mem, positions, mix_norm_g, w_in, cmp_pe, cmp_w1, cmp_b1, cmp_w2, cmp_b2, hgrn_lb_logits, hgrn_norm_g, w_up_nsa, w_up_hgrn, w_out, xa_norm_g, xa_mem_norm_g, w_xq, w_xkv, w_xo, moe_norm_g, router_w, router_b, moe_w1, moe_b1, moe_w2, moe_b2, final_norm_g):
    depth = w_in.shape[0]
    assert depth == 1, "single-layer block: the final norm is fused into the last layer's combine"
    ts = _tile_sizes(x.shape[1])
    l = 0
    return _layer(x, mem, positions, ts, mix_norm_g[l], w_in[l], cmp_pe[l], cmp_w1[l], cmp_b1[l], cmp_w2[l], cmp_b2[l],
                  hgrn_lb_logits, hgrn_norm_g[l], w_up_nsa[l], w_up_hgrn[l], w_out[l], xa_norm_g[l], xa_mem_norm_g[l],
                  w_xq[l], w_xkv[l], w_xo[l], moe_norm_g[l], router_w[l], router_b[l], moe_w1[l], moe_b1[l], moe_w2[l],
                  moe_b2[l], final_norm_g)
```

```python
import functools

import jax
import jax.numpy as jnp
from jax import lax
from jax.experimental import pallas as pl
from jax.experimental.pallas import tpu as pltpu

EPS = 1e-6
ROPE_THETA = 10000.0
NEG_INF = -1e30
FORCE_SCORE = 1e9

NSA_HEADS = 8
NSA_KV_GROUPS = 2
NSA_Q_PER_GROUP = NSA_HEADS // NSA_KV_GROUPS
NSA_HEAD_DIM = 64
CMP_BLOCK = 32
CMP_STRIDE = 16
CMP_HIDDEN = 256
SLC_BLOCK = 64
SLC_TOPK = 16
N_LOCAL_BLOCKS = 2
WINDOW = 512
NSA_Q_W = NSA_HEADS * NSA_HEAD_DIM
NSA_KV_W = NSA_KV_GROUPS * NSA_HEAD_DIM

HGRN_HEADS = 4
HGRN_DK = 128
HGRN_DV = 128
HGRN_CHUNK = 64
HGRN_SUB = 16
HGRN_W = HGRN_HEADS * HGRN_DK

XA_HEADS = 4
XA_HEAD_DIM = 128
XA_W = XA_HEADS * XA_HEAD_DIM

N_EXPERTS = 32
TOP_K = 4
SWIGLU_ALPHA = 1.702
SWIGLU_LIMIT = 7.0

LANES = 128
SEG_ALIGN = 8
BF16_SUBLANES = 16
LOG2E = 1.4426950408889634
VMEM_LIMIT = 48 * 1024 * 1024

COL_GA = 0
COL_GB = 1024
COL_HQ = 2048
COL_HF = 2560
COL_HI = 3072
COL_HG = 3584
COL_NQ = 4096
COL_KSKW = 4608
COL_KCVC = 4864
COL_VSVW = 5120
COL_NG = 5376
P_WIDTH = 5632

F32 = jnp.float32
BF16 = jnp.bfloat16


def _params(sem):
    return pltpu.CompilerParams(dimension_semantics=sem, vmem_limit_bytes=VMEM_LIMIT)


def _dot(a, b):
    return jnp.dot(a, b, preferred_element_type=F32)


def _dot_nt(a, b):
    return lax.dot_general(a, b, (((1,), (1,)), ((), ())), preferred_element_type=F32)


def _rms(xf, g):
    return xf * lax.rsqrt(jnp.mean(xf * xf, axis=-1, keepdims=True) + EPS) * g


def _sigmoid(x):
    return 1.0 / (1.0 + jnp.exp(-x))


def _norm_matmul_kernel(x_ref, g_ref, w_ref, o_ref, hn_ref):
    @pl.when(pl.program_id(1) == 0)
    def _():
        hn_ref[...] = _rms(x_ref[...], g_ref[...]).astype(BF16)

    o_ref[...] = _dot(hn_ref[...], w_ref[...]).astype(o_ref.dtype)


def _norm_matmul(x, g, w, out_dtype, tm, tn, name):
    t, d = x.shape
    n = w.shape[1]
    return pl.pallas_call(
        _norm_matmul_kernel,
        out_shape=jax.ShapeDtypeStruct((t, n), out_dtype),
        grid=(t // tm, n // tn),
        in_specs=[
            pl.BlockSpec((tm, d), lambda i, j: (i, 0)),
            pl.BlockSpec((1, d), lambda i, j: (0, 0)),
            pl.BlockSpec((d, tn), lambda i, j: (0, j)),
        ],
        out_specs=pl.BlockSpec((tm, tn), lambda i, j: (i, j)),
        scratch_shapes=[pltpu.VMEM((tm, d), BF16)],
        compiler_params=_params(("arbitrary", "arbitrary")),
        name=name,
    )(x, g, w)


def _rope_coeffs(pos_col, invf):
    ang = pos_col.astype(F32) * invf
    lane = lax.broadcasted_iota(jnp.int32, ang.shape, 1)
    first = (lane & (NSA_HEAD_DIM - 1)) < (NSA_HEAD_DIM // 2)
    c = jnp.cos(ang)
    s = jnp.sin(ang)
    return c, jnp.where(first, -s, s), first


def _rope_tile(x, c, s_signed, first):
    half = NSA_HEAD_DIM // 2
    partner = jnp.where(first, pltpu.roll(x, LANES - half, 1), pltpu.roll(x, half, 1))
    return x * c + partner * s_signed


def _rope_kernel(q_ref, k_ref, pos_ref, invf_ref, qo_ref, ko_ref, *, q_scale):
    c, s_signed, first = _rope_coeffs(pos_ref[...], invf_ref[...])
    for i in range(q_ref.shape[1] // LANES):
        sl = slice(i * LANES, (i + 1) * LANES)
        qo_ref[:, sl] = (_rope_tile(q_ref[:, sl], c, s_signed, first) * q_scale).astype(BF16)
    for i in range(k_ref.shape[1] // LANES):
        sl = slice(i * LANES, (i + 1) * LANES)
        ko_ref[:, sl] = _rope_tile(k_ref[:, sl], c, s_signed, first).astype(BF16)


def _rope(p, pos_col, invf, tm):
    t = p.shape[0]
    kw = 2 * NSA_KV_W
    return pl.pallas_call(
        functools.partial(_rope_kernel, q_scale=NSA_HEAD_DIM ** -0.5 * LOG2E),
        out_shape=(jax.ShapeDtypeStruct((t, NSA_Q_W), BF16), jax.ShapeDtypeStruct((t, kw), BF16)),
        grid=(t // tm,),
        in_specs=[
            pl.BlockSpec((tm, NSA_Q_W), lambda i: (i, COL_NQ // NSA_Q_W)),
            pl.BlockSpec((tm, kw), lambda i: (i, COL_KSKW // kw)),
            pl.BlockSpec((tm, 1), lambda i: (i, 0)),
            pl.BlockSpec((1, LANES), lambda i: (0, 0)),
        ],
        out_specs=(
            pl.BlockSpec((tm, NSA_Q_W), lambda i: (i, 0)),
            pl.BlockSpec((tm, kw), lambda i: (i, 0)),
        ),
        compiler_params=_params(("arbitrary",)),
        name="rope",
    )(p, p, pos_col, invf)


def _gelu_tanh(x):
    return 0.5 * x * (1.0 + jnp.tanh(0.7978845608028654 * (x + 0.044715 * (x * x * x))))


def _compress_kernel(r_ref, pe_ref, w1_ref, b1_ref, w2_ref, b2_ref, pos_ref, invf_ref, o_ref):
    nr = r_ref.shape[3]
    half = r_ref.shape[4]
    acc = None
    for g in range(NSA_KV_GROUPS):
        r = r_ref[0, 0, g]
        top = _dot((r + pe_ref[0, 0]).astype(BF16), w1_ref[0, :half, :])
        bot = _dot((r + pe_ref[0, 1]).astype(BF16), w1_ref[0, half:, :])
        pre = top + pltpu.roll(bot, nr - 1, 0) + b1_ref[0]
        part = _dot(_gelu_tanh(pre).astype(BF16), w2_ref[0, g])
        acc = part if acc is None else acc + part
    out = acc + b2_ref[0]
    c, s_signed, first = _rope_coeffs(pos_ref[0], invf_ref[...])
    roped = _rope_tile(out, c, s_signed, first)
    is_key = pl.program_id(0) == 0
    o_ref[0, 0] = jnp.where(is_key, roped, out).astype(BF16)


def _compress(r, pe, w1, b1, w2p, b2t, pos_cmp, invf):
    _, b, g, nr, half = r.shape
    return pl.pallas_call(
        _compress_kernel,
        out_shape=jax.ShapeDtypeStruct((2, b, nr, LANES), BF16),
        grid=(2, b),
        in_specs=[
            pl.BlockSpec((1, 1, g, nr, half), lambda k, i: (k, i, 0, 0, 0)),
            pl.BlockSpec((1, 2, 1, half), lambda k, i: (k, 0, 0, 0)),
            pl.BlockSpec((1, 2 * half, CMP_HIDDEN), lambda k, i: (k, 0, 0)),
            pl.BlockSpec((1, 1, CMP_HIDDEN), lambda k, i: (k, 0, 0)),
            pl.BlockSpec((1, g, CMP_HIDDEN, LANES), lambda k, i: (k, 0, 0, 0)),
            pl.BlockSpec((1, 1, LANES), lambda k, i: (k, 0, 0)),
            pl.BlockSpec((1, nr, 1), lambda k, i: (i, 0, 0)),
            pl.BlockSpec((1, LANES), lambda k, i: (0, 0)),
        ],
        out_specs=pl.BlockSpec((1, 1, nr, LANES), lambda k, i: (k, i, 0, 0)),
        compiler_params=_params(("arbitrary", "arbitrary")),
        name="compress",
    )(r, pe, w1, b1, w2p, b2t, pos_cmp, invf)


def _nsa_kernel(qt_ref, kc_ref, vct_ref, ksa_ref, vst_ref, kw_ref, vwt_ref, g_ref, o_ref, acc_ref, out_ref,
                sa_ref, sb_ref, qa_ref, *, tq, tk, seq):
    hg = NSA_Q_PER_GROUP
    dh = NSA_HEAD_DIM
    nc = kc_ref.shape[2]
    nb = seq // SLC_BLOCK
    top_k = min(SLC_TOPK, nb)
    s0 = pl.program_id(2) * tq
    t_lane = s0 + lax.broadcasted_iota(jnp.int32, (1, tq), 1)
    gate = _sigmoid(g_ref[0, 0, 0])

    def scores(k_tile, bias):
        s = _dot(k_tile, qt_ref[0, 0, 0])
        return jnp.concatenate([s[:, h * tq:(h + 1) * tq] + bias for h in range(hg)], axis=1)

    def normalised(acc):
        return acc[:dh] / acc[dh:dh + 1]

    n_col = lax.broadcasted_iota(jnp.int32, (nc, 1), 0)
    valid_c = (n_col * CMP_STRIDE + (CMP_BLOCK - 1) <= t_lane) & (n_col < nc - 1)
    win_keys = WINDOW + tq
    w0 = pl.multiple_of(jnp.maximum(s0 - WINDOW, 0), tq)
    wpos = w0 + lax.broadcasted_iota(jnp.int32, (win_keys, 1), 0)
    bias_w = jnp.where((wpos <= t_lane) & (wpos > t_lane - WINDOW), 0.0, NEG_INF)
    s_c = scores(kc_ref[0, 0], jnp.where(valid_c, 0.0, NEG_INF))
    s_w = scores(kw_ref[0, 0, pl.ds(w0, win_keys), :], bias_w)

    e_c = jnp.exp2(s_c - jnp.max(s_c, axis=0, keepdims=True))
    t_all = s0 + (lax.broadcasted_iota(jnp.int32, (1, hg * tq), 1) & (tq - 1))
    row_ok = t_all >= CMP_BLOCK - 1
    pn = e_c * jnp.where(row_ok, 1.0 / jnp.sum(e_c, axis=0, keepdims=True), 0.0)
    out_ref[...] = gate[0:1] * _dot(vct_ref[0, 0], pn.astype(BF16))[:dh]
    p_sum = pn[:, 0:tq]
    for h in range(1, hg):
        p_sum = p_sum + pn[:, h * tq:(h + 1) * tq]

    j_col = lax.broadcasted_iota(jnp.int32, (nb, 1), 0)
    n_row = lax.broadcasted_iota(jnp.int32, (1, nc), 1) * CMP_STRIDE
    overlap = (n_row < j_col * SLC_BLOCK + SLC_BLOCK) & (n_row + CMP_BLOCK > j_col * SLC_BLOCK)
    overlap = jnp.where(overlap, 1.0, 0.0).astype(BF16)
    p_hi = p_sum.astype(BF16)
    p_lo = (p_sum - p_hi.astype(F32)).astype(BF16)
    imp = _dot(overlap, p_hi) + _dot(overlap, p_lo)

    p_w = jnp.exp2(s_w - jnp.max(s_w, axis=0, keepdims=True)).astype(BF16)
    out_ref[...] += gate[2:3] * normalised(_dot(vwt_ref[0, 0, :, pl.ds(w0, win_keys)], p_w))

    cur = t_lane >> 6
    causal_b = j_col <= cur
    forced = (j_col == 0) | (causal_b & (j_col > cur - N_LOCAL_BLOCKS))
    score = jnp.where(forced, FORCE_SCORE, jnp.where(causal_b, imp, -1.0))
    j_f = jnp.broadcast_to(j_col.astype(F32), (nb, tq))
    for _ in range(top_k):
        mx = jnp.max(score, axis=0, keepdims=True)
        first_idx = jnp.min(jnp.where(score == mx, j_f, float(nb)), axis=0, keepdims=True)
        score = jnp.where(j_f == first_idx, -jnp.inf, score)
    sel = jnp.where(causal_b & (score == -jnp.inf), 1.0, 0.0)

    sel_bias = ((sel - 1.0) * (-NEG_INF)).astype(BF16)
    qa_ref[:dh] = qt_ref[0, 0, 0]
    qa_ref[dh:] = jnp.concatenate([sel_bias] * hg, axis=1)
    k_col = lax.broadcasted_iota(jnp.int32, (tk, 1), 0)

    def put_scores(k0, buf_ref):
        buf_ref[...] = _dot(ksa_ref[0, 0, pl.ds(pl.multiple_of(k0, tk), tk), :], qa_ref[...])

    def consume(k0, buf_ref, m_old, causal):
        s = buf_ref[...]
        if causal:
            bias = jnp.where(k0 + k_col <= t_lane, 0.0, NEG_INF)
            s = jnp.concatenate([s[:, h * tq:(h + 1) * tq] + bias for h in range(hg)], axis=1)
        m_new = jnp.maximum(m_old, jnp.max(s, axis=0, keepdims=True))
        pv = _dot(vst_ref[0, 0, :, pl.ds(pl.multiple_of(k0, tk), tk)], jnp.exp2(s - m_new).astype(BF16))
        acc_ref[...] = jnp.exp2(m_old - m_new) * acc_ref[...] + pv
        return m_new

    def slc_pair(it, m):
        k0 = it * (2 * tk)
        put_scores(k0 + tk, sb_ref)
        m = consume(k0, sa_ref, m, False)
        put_scores(k0 + 2 * tk, sa_ref)
        return consume(k0 + tk, sb_ref, m, False)

    acc_ref[...] = jnp.zeros_like(acc_ref)
    put_scores(jnp.int32(0), sa_ref)
    d_tile = s0 // tk
    n_full = d_tile // 2
    m_s = lax.fori_loop(0, n_full, slc_pair, jnp.full((1, hg * tq), NEG_INF, F32))
    e0 = n_full * (2 * tk)

    @pl.when(d_tile % 2 == 1)
    def _():
        put_scores(e0 + tk, sb_ref)
        consume(e0 + tk, sb_ref, consume(e0, sa_ref, m_s, False), True)

    @pl.when(d_tile % 2 == 0)
    def _():
        consume(e0, sa_ref, m_s, True)

    o_ref[0, 0, 0] = (out_ref[...] + gate[1:2] * normalised(acc_ref[...])).astype(o_ref.dtype)


def _nsa(qt, kc, vct, ksa, vst, kw, vwt, gates, tq, tk):
    b, g, nq, dh, lanes = qt.shape
    hg = lanes // tq
    s = nq * tq
    nr = kc.shape[2]
    vr = vst.shape[2]
    k_spec = pl.BlockSpec((1, 1, s, dh), lambda i, j, k: (i, j, 0, 0))
    vt_spec = pl.BlockSpec((1, 1, vr, s), lambda i, j, k: (i, j, 0, 0))
    return pl.pallas_call(
        functools.partial(_nsa_kernel, tq=tq, tk=tk, seq=s),
        out_shape=jax.ShapeDtypeStruct((b, g, nq, dh, hg * tq), BF16),
        grid=(b, g, nq),
        in_specs=[
            pl.BlockSpec((1, 1, 1, dh, hg * tq), lambda i, j, k: (i, j, k, 0, 0)),
            pl.BlockSpec((1, 1, nr, dh), lambda i, j, k: (i, j, 0, 0)),
            pl.BlockSpec((1, 1, vr, nr), lambda i, j, k: (i, j, 0, 0)),
            pl.BlockSpec((1, 1, s, ksa.shape[3]), lambda i, j, k: (i, j, 0, 0)), vt_spec, k_spec, vt_spec,
            pl.BlockSpec((1, 1, 1, 3, hg * tq), lambda i, j, k: (i, j, k, 0, 0)),
        ],
        out_specs=pl.BlockSpec((1, 1, 1, dh, hg * tq), lambda i, j, k: (i, j, k, 0, 0)),
        scratch_shapes=[pltpu.VMEM((vr, hg * tq), F32), pltpu.VMEM((dh, hg * tq), F32),
                        pltpu.VMEM((tk, hg * tq), F32), pltpu.VMEM((tk, hg * tq), F32),
                        pltpu.VMEM((ksa.shape[3], hg * tq), BF16)],
        compiler_params=_params(("arbitrary", "arbitrary", "arbitrary")),
        name="nsa",
    )(qt, kc, vct, ksa, vst, kw, vwt, gates)


def _cumsum_rows(x):
    n = x.shape[0]
    row = lax.broadcasted_iota(jnp.int32, x.shape, 0)
    d = 1
    while d < n:
        x = x + jnp.where(row >= d, pltpu.roll(x, d, 0), 0.0)
        d *= 2
    return x


def _hgrn_kernel(q_ref, f_ref, i_ref, g_ref, lb_ref, gn_ref, o_ref, st_ref, *, n_chunks):
    @pl.when(pl.program_id(1) == 0)
    def _():
        st_ref[...] = jnp.zeros_like(st_ref)

    c_len = HGRN_CHUNK
    sub = HGRN_SUB
    lbl = lb_ref[...]
    lb_e = jnp.exp(lbl - jnp.max(lbl, axis=0, keepdims=True))
    lb_all = lb_e[0:1] / jnp.sum(lb_e, axis=0, keepdims=True)

    items = [(c, h) for c in range(n_chunks) for h in range(HGRN_HEADS)]
    wave1 = {}
    for c, h in items:
        rs = slice(c * c_len, (c + 1) * c_len)
        ls = slice(h * HGRN_DK, (h + 1) * HGRN_DK)
        lb = lb_all[:, ls]
        f = lb + (1.0 - lb) * _sigmoid(f_ref[rs, ls])
        k = 1.0 - f
        b = _cumsum_rows(jnp.log(f))
        q = q_ref[rs, ls]
        v32 = i_ref[rs, ls]
        b_end = b[c_len - 1:c_len]
        attn = []
        for blk in range(c_len // sub):
            lo, hi = blk * sub, (blk + 1) * sub
            mid = lo + sub // 2
            beta = b[mid - 1:mid]
            qd = (q[lo:hi] * jnp.exp(b[lo:hi] - beta)).astype(BF16)
            kd = (k[:hi] * jnp.exp(beta - b[:hi])).astype(BF16)
            attn.append(_dot_nt(qd, kd))
        update = _dot(v32.T.astype(BF16), (k * jnp.exp(b_end - b)).astype(BF16))
        wave1[c, h] = ((q * jnp.exp(b)).astype(BF16), jnp.exp(b_end), update, attn)

    o_inter = {}
    for h in range(HGRN_HEADS):
        st = st_ref[h]
        for c in range(n_chunks):
            q_dec, decay, update, _ = wave1[c, h]
            o_inter[c, h] = _dot_nt(q_dec, st.astype(BF16))
            st = st * decay + update
        st_ref[h] = st

    for c, h in items:
        rs = slice(c * c_len, (c + 1) * c_len)
        ls = slice(h * HGRN_DK, (h + 1) * HGRN_DK)
        v = i_ref[rs, ls].astype(BF16)
        pieces = []
        for blk, a in enumerate(wave1[c, h][3]):
            lo, hi = blk * sub, (blk + 1) * sub
            ti = lax.broadcasted_iota(jnp.int32, (sub, hi), 0)
            si = lax.broadcasted_iota(jnp.int32, (sub, hi), 1)
            pieces.append(_dot(jnp.where(si <= ti + lo, a, 0.0).astype(BF16), v[:hi]))
        o = o_inter[c, h] + jnp.concatenate(pieces, axis=0)
        gate = g_ref[rs, ls]
        o_ref[rs, ls] = (_rms(o, gn_ref[...]) * (gate * _sigmoid(gate))).astype(o_ref.dtype)


def _hgrn(p, lb_logits, gn, batch, seq, n_chunks):
    t = p.shape[0]
    rows = n_chunks * HGRN_CHUNK
    steps = seq // rows

    def col(cb):
        return pl.BlockSpec((rows, HGRN_W), lambda i, j: (i * steps + j, cb))

    return pl.pallas_call(
        functools.partial(_hgrn_kernel, n_chunks=n_chunks),
        out_shape=jax.ShapeDtypeStruct((t, HGRN_W), BF16),
        grid=(batch, steps),
        in_specs=[
            col(COL_HQ // HGRN_W), col(COL_HF // HGRN_W), col(COL_HI // HGRN_W), col(COL_HG // HGRN_W),
            pl.BlockSpec(lb_logits.shape, lambda i, j: (0, 0)),
            pl.BlockSpec((1, HGRN_DV), lambda i, j: (0, 0)),
        ],
        out_specs=pl.BlockSpec((rows, HGRN_W), lambda i, j: (i * steps + j, 0)),
        scratch_shapes=[pltpu.VMEM((HGRN_HEADS, HGRN_DV, HGRN_DK), F32)],
        compiler_params=_params(("arbitrary", "arbitrary")),
        name="hgrn",
    )(p, p, p, p, lb_logits, gn)


def _merge_kernel(x_ref, ga_ref, gb_ref, yn_ref, yh_ref, wn_ref, wh_ref, wo_ref, o_ref):
    mixed = _sigmoid(ga_ref[...]) * _dot(yn_ref[...], wn_ref[...]) + _sigmoid(gb_ref[...]) * _dot(yh_ref[...], wh_ref[...])
    o_ref[...] = x_ref[...] + _dot(mixed.astype(BF16), wo_ref[...])


def _merge(x, p, y_nsa, y_hgrn, wn, wh, wo, tm):
    t, d = x.shape
    full = lambda a: pl.BlockSpec(a.shape, lambda i: (0, 0))
    return pl.pallas_call(
        _merge_kernel,
        out_shape=jax.ShapeDtypeStruct((t, d), F32),
        grid=(t // tm,),
        in_specs=[
            pl.BlockSpec((tm, d), lambda i: (i, 0)),
            pl.BlockSpec((tm, d), lambda i: (i, COL_GA // d)),
            pl.BlockSpec((tm, d), lambda i: (i, COL_GB // d)),
            pl.BlockSpec((tm, NSA_Q_W), lambda i: (i, 0)),
            pl.BlockSpec((tm, HGRN_W), lambda i: (i, 0)),
            full(wn), full(wh), full(wo),
        ],
        out_specs=pl.BlockSpec((tm, d), lambda i: (i, 0)),
        compiler_params=_params(("arbitrary",)),
        name="merge",
    )(x, p, p, y_nsa, y_hgrn, wn, wh, wo)


def _xattn_kernel(x_ref, g_ref, wq_ref, kv_ref, wo_ref, o_ref):
    x = x_ref[...]
    xq = _dot(_rms(x, g_ref[...]).astype(BF16), wq_ref[...]).astype(BF16)
    heads = [slice(h * XA_HEAD_DIM, (h + 1) * XA_HEAD_DIM) for h in range(XA_HEADS)]
    scores = [_dot_nt(xq[:, ls], kv_ref[0, :, ls]) * (XA_HEAD_DIM ** -0.5) for ls in heads]
    outs = []
    for h, s in enumerate(scores):
        e = jnp.exp(s - jnp.max(s, axis=-1, keepdims=True))
        p = e / jnp.sum(e, axis=-1, keepdims=True)
        outs.append(_dot(p.astype(BF16), kv_ref[0, :, XA_W + h * XA_HEAD_DIM:XA_W + (h + 1) * XA_HEAD_DIM]))
    o_x = jnp.concatenate(outs, axis=-1)
    o_ref[...] = x + _dot(o_x.astype(BF16), wo_ref[...])


def _xattn(x, g, wq, kv, wo, seq, tm):
    t, d = x.shape
    steps = seq // tm
    full = lambda a: pl.BlockSpec(a.shape, lambda i: (0, 0))
    return pl.pallas_call(
        _xattn_kernel,
        out_shape=jax.ShapeDtypeStruct((t, d), F32),
        grid=(t // tm,),
        in_specs=[
            pl.BlockSpec((tm, d), lambda i: (i, 0)),
            full(g), full(wq),
            pl.BlockSpec((1,) + kv.shape[1:], lambda i: (i // steps, 0, 0)),
            full(wo),
        ],
        out_specs=pl.BlockSpec((tm, d), lambda i: (i, 0)),
        compiler_params=_params(("arbitrary",)),
        name="xattn",
    )(x, g, wq, kv, wo)


def _router_kernel(x_ref, g_ref, w_ref, b_ref, hm_ref, idx_ref, rank_ref, wt_ref, cnt_ref):
    tm = x_ref.shape[0]
    hm = _rms(x_ref[...], g_ref[...]).astype(BF16)
    hm_ref[...] = hm
    lane = lax.broadcasted_iota(jnp.int32, (tm, LANES), 1)
    lane_f = lane.astype(F32)
    logits = _dot(hm, w_ref[...]) + b_ref[...]
    logits = jnp.where(lane < N_EXPERTS, logits, -jnp.inf)
    picks, vals = [], []
    onehot_all = jnp.zeros((tm, LANES), F32)
    for _ in range(TOP_K):
        mx = jnp.max(logits, axis=-1, keepdims=True)
        first_idx = jnp.min(jnp.where(logits == mx, lane_f, float(LANES)), axis=-1, keepdims=True)
        hit = lane_f == first_idx
        onehot = jnp.where(hit, 1.0, 0.0)
        logits = jnp.where(hit, -jnp.inf, logits)
        picks.append((first_idx, onehot))
        vals.append(mx)
        onehot_all = onehot_all + onehot
    exps = [jnp.exp(v - vals[0]) for v in vals]
    den = exps[0]
    for e in exps[1:]:
        den = den + e
    r_i = lax.broadcasted_iota(jnp.int32, (tm, tm), 0)
    c_i = lax.broadcasted_iota(jnp.int32, (tm, tm), 1)
    lower = jnp.where(c_i < r_i, 1.0, 0.0).astype(BF16)
    before = _dot(lower, onehot_all.astype(BF16))
    idx_out = jnp.zeros((tm, LANES), F32)
    rank_out = jnp.zeros((tm, LANES), F32)
    wt_out = jnp.zeros((tm, LANES), F32)
    for k in range(TOP_K):
        first_idx, onehot = picks[k]
        rank = jnp.sum(onehot * before, axis=-1, keepdims=True)
        idx_out = jnp.where(lane == k, first_idx, idx_out)
        rank_out = jnp.where(lane == k, rank, rank_out)
        wt_out = jnp.where(lane == k, exps[k] / den, wt_out)
    idx_ref[...] = idx_out.astype(jnp.int32)
    rank_ref[...] = rank_out.astype(jnp.int32)
    wt_ref[...] = wt_out
    cnt_ref[0] = jnp.sum(onehot_all, axis=0, keepdims=True)


def _router(x, g, w, b, tm):
    t, d = x.shape
    full = lambda a: pl.BlockSpec(a.shape, lambda i: (0, 0))
    lane_out = pl.BlockSpec((tm, LANES), lambda i: (i, 0))
    return pl.pallas_call(
        _router_kernel,
        out_shape=(
            jax.ShapeDtypeStruct((t, d), BF16),
            jax.ShapeDtypeStruct((t, LANES), jnp.int32),
            jax.ShapeDtypeStruct((t, LANES), jnp.int32),
            jax.ShapeDtypeStruct((t, LANES), F32),
            jax.ShapeDtypeStruct((t // tm, 1, LANES), F32),
        ),
        grid=(t // tm,),
        in_specs=[pl.BlockSpec((tm, d), lambda i: (i, 0)), full(g), full(w), full(b)],
        out_specs=(pl.BlockSpec((tm, d), lambda i: (i, 0)), lane_out, lane_out, lane_out,
                   pl.BlockSpec((1, 1, LANES), lambda i: (i, 0, 0))),
        compiler_params=_params(("arbitrary",)),
        name="router",
    )(x, g, w, b)


def _slots_by_row(row_t, values_t, rows):
    r = lax.broadcasted_iota(jnp.int32, (rows, row_t.shape[1]), 0)
    onehot = jnp.zeros(r.shape, F32)
    weighted = jnp.zeros(r.shape, F32)
    for k in range(TOP_K):
        hit = r == row_t[k:k + 1]
        onehot = jnp.where(hit, 1.0, onehot)
        weighted = jnp.where(hit, values_t[k:k + 1], weighted)
    return onehot, weighted


def _slots_by_token(row_n, rows):
    r = lax.broadcasted_iota(jnp.int32, (row_n.shape[0], rows), 1)
    onehot = jnp.zeros(r.shape, F32)
    for k in range(TOP_K):
        onehot = jnp.where(r == row_n[:, k:k + 1], 1.0, onehot)
    return onehot


def _segment_copies(src_ref, n8_ref, dst_ref, make_copy, tile_tokens, table_row=None):
    base = (pl.program_id(0) if table_row is None else table_row) * N_EXPERTS
    sizes = []
    size = tile_tokens
    while size >= SEG_ALIGN:
        sizes.append(size)
        size //= 2

    def visit(e, start):
        n8 = n8_ref[base + e]
        src = src_ref[base + e]
        dst = dst_ref[base + e]
        for size in sizes:
            done = n8 & (-2 * size)

            @pl.when((n8 & size) != 0)
            def _():
                cp = make_copy(pl.multiple_of(src + done, SEG_ALIGN), pl.multiple_of(dst + done, SEG_ALIGN), size)
                if start:
                    cp.start()
                else:
                    cp.wait()

    def start_all(e, c):
        visit(e, True)
        return c

    def wait_all(e, c):
        visit(e, False)
        return c

    return start_all, wait_all


def _dispatch_kernel(src_ref, n8_ref, dst_ref, nu_ref, hm_ref, rowt_ref, wtt_ref, xs_ref, buf_ref, sem,
                     *, tm, n_steps):
    rows = buf_ref.shape[1]
    d = hm_ref.shape[1]
    tt = hm_ref.shape[0]
    n_tiles = xs_ref.shape[0] // tm
    step = pl.program_id(0)
    slot = step % 2

    def copies(table_row, buf_slot, tile_tokens=tt):
        def make_copy(src, dst, size):
            return pltpu.make_async_copy(buf_ref.at[buf_slot, pl.ds(src, size)], xs_ref.at[pl.ds(dst, size)],
                                         sem.at[buf_slot])
        return _segment_copies(src_ref, n8_ref, dst_ref, make_copy, tile_tokens, table_row) + (make_copy,)

    @pl.when(step >= 2)
    def _():
        lax.fori_loop(0, N_EXPERTS, copies(step - 2, slot)[1], 0)

    onehot, weighted = _slots_by_row(rowt_ref[0], wtt_ref[0], rows)
    buf_ref[slot, :, :d] = _dot(onehot.astype(BF16), hm_ref[...])
    w_hi = weighted.astype(BF16)
    w_lo = (weighted - w_hi.astype(F32)).astype(BF16)
    ones = jnp.ones((tt, LANES), BF16)
    buf_ref[slot, :, d:] = _dot(w_hi, ones) + _dot(w_lo, ones)
    start_cur, wait_cur, make_copy = copies(step, slot)
    lax.fori_loop(0, N_EXPERTS, start_cur, 0)

    @pl.when(step == n_steps - 1)
    def _():
        if n_steps >= 2:
            lax.fori_loop(0, N_EXPERTS, copies(step - 1, 1 - slot)[1], 0)
        lax.fori_loop(0, N_EXPERTS, wait_cur, 0)
        buf_ref[slot, :tm] = jnp.zeros((tm, buf_ref.shape[2]), F32)
        start_tail, wait_tail, _ = copies(n_steps, slot, tm)
        lax.fori_loop(0, N_EXPERTS, start_tail, 0)

        def zero_tile(i):
            return make_copy(0, pl.multiple_of(i * tm, tm), tm)

        lax.fori_loop(nu_ref[0], n_tiles, lambda i, c: (zero_tile(i).start(), c)[1], 0)
        lax.fori_loop(0, N_EXPERTS, wait_tail, 0)
        lax.fori_loop(nu_ref[0], n_tiles, lambda i, c: (zero_tile(i).wait(), c)[1], 0)


def _dispatch(tables, n_used, hm, row_t, wt_t, n_pad, tt, rows, tm):
    t, d = hm.shape
    per_tile = pl.BlockSpec((1,) + row_t.shape[1:], lambda i, *_: (i, 0, 0))
    grid_spec = pltpu.PrefetchScalarGridSpec(
        num_scalar_prefetch=4,
        grid=(t // tt,),
        in_specs=[pl.BlockSpec((tt, d), lambda i, *_: (i, 0)), per_tile, per_tile],
        out_specs=pl.BlockSpec(memory_space=pl.ANY),
        scratch_shapes=[pltpu.VMEM((2, rows, d + LANES), F32), pltpu.SemaphoreType.DMA((2,))],
    )
    return pl.pallas_call(
        functools.partial(_dispatch_kernel, tm=tm, n_steps=t // tt),
        out_shape=jax.ShapeDtypeStruct((n_pad, d + LANES), F32),
        grid_spec=grid_spec,
        compiler_params=_params(("arbitrary",)),
        name="dispatch",
    )(*tables, n_used, hm, row_t, wt_t)


def _w1_prep_kernel(w_ref, o_ref):
    grp = 2 * LANES
    r_i = lax.broadcasted_iota(jnp.int32, (grp, grp), 0)
    c_i = lax.broadcasted_iota(jnp.int32, (grp, grp), 1)
    src_col = jnp.where(c_i < LANES, 2 * c_i, 2 * (c_i - LANES) + 1)
    perm = jnp.where(r_i == src_col, 1.0, 0.0).astype(BF16)
    for c in range(w_ref.shape[1] // grp):
        sl = slice(c * grp, (c + 1) * grp)
        o_ref[:, sl] = _dot(w_ref[:, sl].astype(BF16), perm).astype(BF16)


def _w1_prep(w, tm):
    r, n = w.shape
    return pl.pallas_call(
        _w1_prep_kernel,
        out_shape=jax.ShapeDtypeStruct((r, n), BF16),
        grid=(r // tm,),
        in_specs=[pl.BlockSpec((tm, n), lambda i: (i, 0))],
        out_specs=pl.BlockSpec((tm, n), lambda i: (i, 0)),
        compiler_params=_params(("arbitrary",)),
        name="w1_prep",
    )(w)


def _ffn_kernel(te_ref, nu_ref, x_ref, w1_ref, b1_ref, w2_ref, b2_ref, o_ref):
    del te_ref
    used = pl.program_id(0) < nu_ref[0]

    @pl.when(used)
    def _():
        d = o_ref.shape[1]
        u = _dot(x_ref[:, :d].astype(BF16), w1_ref[0]) + b1_ref[0]
        acts = []
        for c in range(u.shape[1] // (2 * LANES)):
            glu = jnp.minimum(u[:, 2 * c * LANES:(2 * c + 1) * LANES], SWIGLU_LIMIT)
            lin = jnp.clip(u[:, (2 * c + 1) * LANES:(2 * c + 2) * LANES], -SWIGLU_LIMIT, SWIGLU_LIMIT)
            acts.append((glu * _sigmoid(SWIGLU_ALPHA * glu) * (lin + 1.0)).astype(BF16))
        y = _dot(jnp.concatenate(acts, axis=-1), w2_ref[0]) + b2_ref[0]
        weight = x_ref[:, d:]
        o_ref[...] = jnp.concatenate([y[:, c * LANES:(c + 1) * LANES] * weight for c in range(d // LANES)], axis=1)

    @pl.when(jnp.logical_not(used))
    def _():
        o_ref[...] = jnp.zeros_like(o_ref)


def _ffn(tile_expert, n_used, xs, w1, b1, w2, b2, tm):
    n_pad = xs.shape[0]
    d = w1.shape[1]
    f2 = w1.shape[2]
    f = w2.shape[1]
    grid_spec = pltpu.PrefetchScalarGridSpec(
        num_scalar_prefetch=2,
        grid=(n_pad // tm,),
        in_specs=[
            pl.BlockSpec((tm, xs.shape[1]), lambda i, te, nu: (jnp.minimum(i, nu[0] - 1), 0)),
            pl.BlockSpec((1, d, f2), lambda i, te, nu: (te[i], 0, 0)),
            pl.BlockSpec((1, 1, f2), lambda i, te, nu: (te[i], 0, 0)),
            pl.BlockSpec((1, f, d), lambda i, te, nu: (te[i], 0, 0)),
            pl.BlockSpec((1, 1, d), lambda i, te, nu: (te[i], 0, 0)),
        ],
        out_specs=pl.BlockSpec((tm, d), lambda i, te, nu: (i, 0)),
    )
    return pl.pallas_call(
        _ffn_kernel,
        out_shape=jax.ShapeDtypeStruct((n_pad, d), F32),
        grid_spec=grid_spec,
        compiler_params=_params(("arbitrary",)),
        name="expert_ffn",
    )(tile_expert, n_used, xs, w1, b1, w2, b2)


def _combine_kernel(src_ref, n8_ref, dst_ref, ys_ref, rown_ref, x_ref, g_ref, o_ref, buf_ref, sem, *, n_steps):
    rows = buf_ref.shape[1]
    step = pl.program_id(0)
    slot = step % 2

    def copies(table_row, buf_slot):
        def make_copy(src, dst, size):
            return pltpu.make_async_copy(ys_ref.at[pl.ds(dst, size)], buf_ref.at[buf_slot, pl.ds(src, size)],
                                         sem.at[buf_slot])
        return _segment_copies(src_ref, n8_ref, dst_ref, make_copy, x_ref.shape[0], table_row)

    @pl.when(step == 0)
    def _():
        buf_ref[...] = jnp.zeros_like(buf_ref)
        lax.fori_loop(0, N_EXPERTS, copies(0, 0)[0], 0)

    @pl.when(step + 1 < n_steps)
    def _():
        lax.fori_loop(0, N_EXPERTS, copies(step + 1, 1 - slot)[0], 0)

    onehot = _slots_by_token(rown_ref[...], rows).astype(BF16)
    lax.fori_loop(0, N_EXPERTS, copies(step, slot)[1], 0)
    ys = buf_ref[slot]
    y_hi = ys.astype(BF16)
    y_lo = (ys - y_hi.astype(F32)).astype(BF16)
    y = _dot(onehot, y_hi) + _dot(onehot, y_lo)
    o_ref[...] = _rms(x_ref[...] + y, g_ref[...])


def _combine(tables, ys, row_n, x, g, tt, rows):
    t, d = x.shape
    tile = lambda w: pl.BlockSpec((tt, w), lambda i, *_: (i, 0))
    grid_spec = pltpu.PrefetchScalarGridSpec(
        num_scalar_prefetch=3,
        grid=(t // tt,),
        in_specs=[pl.BlockSpec(memory_space=pl.ANY), tile(LANES), tile(d),
                  pl.BlockSpec((1, d), lambda i, *_: (0, 0))],
        out_specs=tile(d),
        scratch_shapes=[pltpu.VMEM((2, rows, d), F32), pltpu.SemaphoreType.DMA((2,))],
    )
    return pl.pallas_call(
        functools.partial(_combine_kernel, n_steps=t // tt),
        out_shape=jax.ShapeDtypeStruct((t, d), F32),
        grid_spec=grid_spec,
        compiler_params=_params(("arbitrary",)),
        name="combine",
    )(*tables, ys, row_n, x, g)


def _tile_sizes(seq):
    return dict(
        tm_proj=1024, tn_proj=P_WIDTH // 4,
        tm_rope=512,
        tq=256, tk=512,
        hgrn_chunks=4,
        tm_merge=512, tm_xattn=512,
        tm_router=512,
        tm_w1_prep=512,
        tm_ffn=512,
    )


def _layer(x, mem, positions, ts, mix_norm_g, w_in, cmp_pe, cmp_w1, cmp_b1, cmp_w2, cmp_b2, lb_logits, hgrn_norm_g,
           w_up_nsa, w_up_hgrn, w_out, xa_norm_g, xa_mem_norm_g, w_xq, w_xkv, w_xo, moe_norm_g, router_w, router_b,
           moe_w1, moe_b1, moe_w2, moe_b2, out_norm_g):
    b, s, d = x.shape
    t = b * s
    g, hg, dh = NSA_KV_GROUPS, NSA_Q_PER_GROUP, NSA_HEAD_DIM
    x2 = x.reshape(t, d)
    row = lambda v: v.reshape(1, -1).astype(F32)

    splits = [0]
    for w in (d, d, NSA_Q_W) + (NSA_KV_W,) * 6 + (3 * NSA_HEADS,) + (HGRN_W,) * 4:
        splits.append(splits[-1] + w)
    seg = lambda i: w_in[:, splits[i]:splits[i + 1]]
    (ga, gb, nq, kc, vc, ks, vs, kw, vw, ng, hq, hf, hi, hgate) = [seg(i) for i in range(14)]
    pad = jnp.zeros((d, P_WIDTH - COL_NG - 3 * NSA_HEADS), w_in.dtype)
    w_p = jnp.concatenate([ga, gb, hq, hf, hi, hgate, nq, ks, kw, kc, vc, vs, vw, ng, pad], axis=1).astype(BF16)

    p = _norm_matmul(x2, row(mix_norm_g), w_p, F32, ts["tm_proj"], ts["tn_proj"], "in_proj")

    half = dh // 2
    inv_freq = ROPE_THETA ** (-jnp.arange(half, dtype=F32) / half)
    invf = jnp.tile(inv_freq, LANES // half).reshape(1, LANES)
    q_r, kk_r = _rope(p, positions.reshape(t, 1), invf, ts["tm_rope"])
    tq = ts["tq"]
    nq = s // tq
    qt = q_r.reshape(b, nq, tq, g, hg, dh).transpose(0, 3, 1, 5, 4, 2).reshape(b, g, nq, dh, hg * tq)
    kk = kk_r.reshape(b, s, 2 * g, dh).transpose(0, 2, 1, 3)
    k_slc, k_win = kk[:, :g], kk[:, g:]
    nb = s // SLC_BLOCK
    block_onehot = (jnp.arange(s)[:, None] // SLC_BLOCK == jnp.arange(nb)[None, :]).astype(BF16)
    ks_aug = jnp.concatenate([k_slc, jnp.broadcast_to(block_onehot, (b, g, s, nb))], axis=-1)

    def values_t(v):
        vt = v.transpose(0, 1, 3, 2).astype(BF16)
        ones = jnp.ones(vt.shape[:2] + (1, vt.shape[3]), BF16)
        zeros = jnp.zeros(vt.shape[:2] + (BF16_SUBLANES - 1, vt.shape[3]), BF16)
        return jnp.concatenate([vt, ones, zeros], axis=2)

    vvt = values_t(p[:, COL_VSVW:COL_VSVW + 2 * NSA_KV_W].reshape(b, s, 2 * g, dh).transpose(0, 2, 1, 3))
    vt_slc, vt_win = vvt[:, :g], vvt[:, g:]
    gates = p[:, COL_NG:COL_NG + 3 * NSA_HEADS].reshape(b, nq, tq, g, hg, 3).transpose(0, 3, 1, 5, 4, 2)
    gates = gates.reshape(b, g, nq, 3, hg * tq)

    nr = s // CMP_STRIDE
    kcvc = p[:, COL_KCVC:COL_KCVC + 2 * NSA_KV_W].reshape(b, s, 2, g, dh).transpose(2, 0, 3, 1, 4)
    r = kcvc.reshape(2, b, g, nr, CMP_STRIDE * dh)
    pe = cmp_pe.reshape(2, 2, 1, CMP_STRIDE * dh)
    zeros_w2 = jnp.zeros_like(cmp_w2)
    w2p = jnp.stack([jnp.concatenate([cmp_w2, zeros_w2], axis=-1),
                     jnp.concatenate([zeros_w2, cmp_w2], axis=-1)], axis=1).astype(BF16)
    b2t = jnp.tile(cmp_b2, (1, g)).reshape(2, 1, LANES)
    pos_cmp = positions[:, CMP_BLOCK - 1::CMP_STRIDE]
    pos_cmp = jnp.pad(pos_cmp, ((0, 0), (0, nr - pos_cmp.shape[1]))).reshape(b, nr, 1)
    cmp = _compress(r, pe, cmp_w1.astype(BF16), cmp_b1.reshape(2, 1, CMP_HIDDEN), w2p, b2t, pos_cmp, invf)
    cmp = cmp.reshape(2, b, nr, g, dh).transpose(0, 1, 3, 2, 4)

    y_nsa = _nsa(qt, cmp[0], values_t(cmp[1]), ks_aug, vt_slc, k_win, vt_win, gates, tq, ts["tk"])
    y_nsa = y_nsa.reshape(b, g, nq, dh, hg, tq).transpose(0, 2, 5, 1, 4, 3).reshape(t, NSA_Q_W)

    y_hgrn = _hgrn(p, lb_logits.astype(F32), row(hgrn_norm_g), b, s, ts["hgrn_chunks"])

    x2 = _merge(x2, p, y_nsa, y_hgrn, w_up_nsa.astype(BF16), w_up_hgrn.astype(BF16), w_out.astype(BF16),
                ts["tm_merge"])

    n_mem = mem.shape[1]
    kv = _norm_matmul(mem.reshape(b * n_mem, d), row(xa_mem_norm_g), w_xkv.astype(BF16), BF16,
                      n_mem, 2 * XA_W, "mem_kv").reshape(b, n_mem, 2 * XA_W)
    x2 = _xattn(x2, row(xa_norm_g), w_xq.astype(BF16), kv, w_xo.astype(BF16), s, ts["tm_xattn"])

    n_exp = router_w.shape[1]
    rw = jnp.pad(router_w, ((0, 0), (0, LANES - n_exp))).astype(BF16)
    rb = jnp.pad(router_b, (0, LANES - n_exp)).reshape(1, LANES).astype(F32)
    assert n_exp == N_EXPERTS
    tt = ts["tm_router"]
    nt = t // tt
    hm, idx, rank, wt, cnt = _router(x2, row(moe_norm_g), rw, rb, tt)
    tm = ts["tm_ffn"]
    n8 = (cnt[:, 0, :n_exp].astype(jnp.int32) + SEG_ALIGN - 1) // SEG_ALIGN * SEG_ALIGN
    src_off = jnp.cumsum(n8, axis=1) - n8
    region = jnp.sum(n8, axis=0)
    padded = (region + tm - 1) // tm * tm
    ends = jnp.cumsum(padded)
    dst_off = (ends - padded)[None, :] + jnp.cumsum(n8, axis=0) - n8
    tables = tuple(jnp.concatenate([a, tail[None, :]], axis=0).reshape(-1) for a, tail in
                   ((src_off, jnp.zeros_like(region)), (n8, padded - region), (dst_off, ends - padded + region)))
    picks = idx[:, :TOP_K].reshape(nt, tt, TOP_K)
    local_row = jnp.take_along_axis(src_off[:, None, :], picks, axis=2) + rank[:, :TOP_K].reshape(nt, tt, TOP_K)
    row_n = jnp.pad(local_row.reshape(t, TOP_K), ((0, 0), (0, LANES - TOP_K)))
    row_t = jnp.pad(local_row.transpose(0, 2, 1), ((0, 0), (0, SEG_ALIGN - TOP_K), (0, 0)))
    wt_t = jnp.pad(wt[:, :TOP_K].reshape(nt, tt, TOP_K).transpose(0, 2, 1), ((0, 0), (0, SEG_ALIGN - TOP_K), (0, 0)))
    rows = tt * TOP_K + n_exp * SEG_ALIGN
    n_pad = (t * TOP_K + nt * n_exp * SEG_ALIGN + n_exp * tm + tm - 1) // tm * tm
    n_tiles = n_pad // tm
    tile_ids = jnp.arange(n_tiles, dtype=jnp.int32)
    tile_expert = jnp.sum(((ends // tm)[None, :] <= tile_ids[:, None]).astype(jnp.int32), axis=1)
    tile_expert = jnp.minimum(tile_expert, n_exp - 1)
    n_used = (ends[-1] // tm).reshape(1).astype(jnp.int32)

    xs = _dispatch(tables, n_used, hm, row_t, wt_t, n_pad, tt, rows, tm)
    f = moe_w2.shape[1]
    w1p = _w1_prep(moe_w1.reshape(n_exp * d, 2 * f), ts["tm_w1_prep"]).reshape(n_exp, d, 2 * f)
    b1p = moe_b1.reshape(n_exp, f // LANES, LANES, 2).transpose(0, 1, 3, 2).reshape(n_exp, 1, 2 * f)
    ys = _ffn(tile_expert, n_used, xs, w1p, b1p, moe_w2.astype(BF16), moe_b2.reshape(n_exp, 1, d), tm)
    out = _combine(tables, ys, row_n, x2, row(out_norm_g), tt, rows)
    return out.reshape(b, s, d)


def kernel(x, mem, positions, mix_norm_g, w_in, cmp_pe, cmp_w1, cmp_b1, cmp_w2, cmp_b2, hgrn_lb_logits, hgrn_norm_g, w_up_nsa, w_up_hgrn, w_out, xa_norm_g, xa_mem_norm_g, w_xq, w_xkv, w_xo, moe_norm_g, router_w, router_b, moe_w1, moe_b1, moe_w2, moe_b2, final_norm_g):
    depth = w_in.shape[0]
    assert depth == 1, "single-layer block: the final norm is fused into the last layer's combine"
    ts = _tile_sizes(x.shape[1])
    l = 0
    return _layer(x, mem, positions, ts, mix_norm_g[l], w_in[l], cmp_pe[l], cmp_w1[l], cmp_b1[l], cmp_w2[l], cmp_b2[l],
                  hgrn_lb_logits, hgrn_norm_g[l], w_up_nsa[l], w_up_hgrn[l], w_out[l], xa_norm_g[l], xa_mem_norm_g[l],
                  w_xq[l], w_xkv[l], w_xo[l], moe_norm_g[l], router_w[l], router_b[l], moe_w1[l], moe_b1[l], moe_w2[l],
                  moe_b2[l], final_norm_g)
```

```python
import functools

import jax
import jax.numpy as jnp
from jax import lax
from jax.experimental import pallas as pl
from jax.experimental.pallas import tpu as pltpu

EPS = 1e-6
ROPE_THETA = 10000.0
NEG_INF = -1e30
FORCE_SCORE = 1e9

NSA_HEADS = 8
NSA_KV_GROUPS = 2
NSA_Q_PER_GROUP = NSA_HEADS // NSA_KV_GROUPS
NSA_HEAD_DIM = 64
CMP_BLOCK = 32
CMP_STRIDE = 16
CMP_HIDDEN = 256
SLC_BLOCK = 64
SLC_TOPK = 16
N_LOCAL_BLOCKS = 2
WINDOW = 512
NSA_Q_W = NSA_HEADS * NSA_HEAD_DIM
NSA_KV_W = NSA_KV_GROUPS * NSA_HEAD_DIM

HGRN_HEADS = 4
HGRN_DK = 128
HGRN_DV = 128
HGRN_CHUNK = 64
HGRN_SUB = 16
HGRN_W = HGRN_HEADS * HGRN_DK

XA_HEADS = 4
XA_HEAD_DIM = 128
XA_W = XA_HEADS * XA_HEAD_DIM

N_EXPERTS = 32
TOP_K = 4
SWIGLU_ALPHA = 1.702
SWIGLU_LIMIT = 7.0

LANES = 128
SEG_ALIGN = 8
BF16_SUBLANES = 16
LOG2E = 1.4426950408889634
VMEM_LIMIT = 48 * 1024 * 1024

COL_GA = 0
COL_GB = 1024
COL_HQ = 2048
COL_HF = 2560
COL_HI = 3072
COL_HG = 3584
COL_NQ = 4096
COL_KSKW = 4608
COL_KCVC = 4864
COL_VSVW = 5120
COL_NG = 5376
P_WIDTH = 5632

F32 = jnp.float32
BF16 = jnp.bfloat16


def _params(sem):
    return pltpu.CompilerParams(dimension_semantics=sem, vmem_limit_bytes=VMEM_LIMIT)


def _dot(a, b):
    return jnp.dot(a, b, preferred_element_type=F32)


def _dot_nt(a, b):
    return lax.dot_general(a, b, (((1,), (1,)), ((), ())), preferred_element_type=F32)


def _rms(xf, g):
    return xf * lax.rsqrt(jnp.mean(xf * xf, axis=-1, keepdims=True) + EPS) * g


def _sigmoid(x):
    return 1.0 / (1.0 + jnp.exp(-x))


def _norm_matmul_kernel(x_ref, g_ref, w_ref, o_ref, hn_ref):
    @pl.when(pl.program_id(1) == 0)
    def _():
        hn_ref[...] = _rms(x_ref[...], g_ref[...]).astype(BF16)

    o_ref[...] = _dot(hn_ref[...], w_ref[...]).astype(o_ref.dtype)


def _norm_matmul(x, g, w, out_dtype, tm, tn, name):
    t, d = x.shape
    n = w.shape[1]
    return pl.pallas_call(
        _norm_matmul_kernel,
        out_shape=jax.ShapeDtypeStruct((t, n), out_dtype),
        grid=(t // tm, n // tn),
        in_specs=[
            pl.BlockSpec((tm, d), lambda i, j: (i, 0)),
            pl.BlockSpec((1, d), lambda i, j: (0, 0)),
            pl.BlockSpec((d, tn), lambda i, j: (0, j)),
        ],
        out_specs=pl.BlockSpec((tm, tn), lambda i, j: (i, j)),
        scratch_shapes=[pltpu.VMEM((tm, d), BF16)],
        compiler_params=_params(("arbitrary", "arbitrary")),
        name=name,
    )(x, g, w)


def _rope_coeffs(pos_col, invf):
    ang = pos_col.astype(F32) * invf
    lane = lax.broadcasted_iota(jnp.int32, ang.shape, 1)
    first = (lane & (NSA_HEAD_DIM - 1)) < (NSA_HEAD_DIM // 2)
    c = jnp.cos(ang)
    s = jnp.sin(ang)
    return c, jnp.where(first, -s, s), first


def _rope_tile(x, c, s_signed, first):
    half = NSA_HEAD_DIM // 2
    partner = jnp.where(first, pltpu.roll(x, LANES - half, 1), pltpu.roll(x, half, 1))
    return x * c + partner * s_signed


def _rope_kernel(q_ref, k_ref, pos_ref, invf_ref, qo_ref, ko_ref, *, q_scale):
    c, s_signed, first = _rope_coeffs(pos_ref[...], invf_ref[...])
    for i in range(q_ref.shape[1] // LANES):
        sl = slice(i * LANES, (i + 1) * LANES)
        qo_ref[:, sl] = (_rope_tile(q_ref[:, sl], c, s_signed, first) * q_scale).astype(BF16)
    for i in range(k_ref.shape[1] // LANES):
        sl = slice(i * LANES, (i + 1) * LANES)
        ko_ref[:, sl] = _rope_tile(k_ref[:, sl], c, s_signed, first).astype(BF16)


def _rope(p, pos_col, invf, tm):
    t = p.shape[0]
    kw = 2 * NSA_KV_W
    return pl.pallas_call(
        functools.partial(_rope_kernel, q_scale=NSA_HEAD_DIM ** -0.5 * LOG2E),
        out_shape=(jax.ShapeDtypeStruct((t, NSA_Q_W), BF16), jax.ShapeDtypeStruct((t, kw), BF16)),
        grid=(t // tm,),
        in_specs=[
            pl.BlockSpec((tm, NSA_Q_W), lambda i: (i, COL_NQ // NSA_Q_W)),
            pl.BlockSpec((tm, kw), lambda i: (i, COL_KSKW // kw)),
            pl.BlockSpec((tm, 1), lambda i: (i, 0)),
            pl.BlockSpec((1, LANES), lambda i: (0, 0)),
        ],
        out_specs=(
            pl.BlockSpec((tm, NSA_Q_W), lambda i: (i, 0)),
            pl.BlockSpec((tm, kw), lambda i: (i, 0)),
        ),
        compiler_params=_params(("arbitrary",)),
        name="rope",
    )(p, p, pos_col, invf)


def _gelu_tanh(x):
    return 0.5 * x * (1.0 + jnp.tanh(0.7978845608028654 * (x + 0.044715 * (x * x * x))))


def _compress_kernel(r_ref, pe_ref, w1_ref, b1_ref, w2_ref, b2_ref, pos_ref, invf_ref, o_ref):
    nr = r_ref.shape[3]
    half = r_ref.shape[4]
    acc = None
    for g in range(NSA_KV_GROUPS):
        r = r_ref[0, 0, g]
        top = _dot((r + pe_ref[0, 0]).astype(BF16), w1_ref[0, :half, :])
        bot = _dot((r + pe_ref[0, 1]).astype(BF16), w1_ref[0, half:, :])
        pre = top + pltpu.roll(bot, nr - 1, 0) + b1_ref[0]
        part = _dot(_gelu_tanh(pre).astype(BF16), w2_ref[0, g])
        acc = part if acc is None else acc + part
    out = acc + b2_ref[0]
    c, s_signed, first = _rope_coeffs(pos_ref[0], invf_ref[...])
    roped = _rope_tile(out, c, s_signed, first)
    is_key = pl.program_id(0) == 0
    o_ref[0, 0] = jnp.where(is_key, roped, out).astype(BF16)


def _compress(r, pe, w1, b1, w2p, b2t, pos_cmp, invf):
    _, b, g, nr, half = r.shape
    return pl.pallas_call(
        _compress_kernel,
        out_shape=jax.ShapeDtypeStruct((2, b, nr, LANES), BF16),
        grid=(2, b),
        in_specs=[
            pl.BlockSpec((1, 1, g, nr, half), lambda k, i: (k, i, 0, 0, 0)),
            pl.BlockSpec((1, 2, 1, half), lambda k, i: (k, 0, 0, 0)),
            pl.BlockSpec((1, 2 * half, CMP_HIDDEN), lambda k, i: (k, 0, 0)),
            pl.BlockSpec((1, 1, CMP_HIDDEN), lambda k, i: (k, 0, 0)),
            pl.BlockSpec((1, g, CMP_HIDDEN, LANES), lambda k, i: (k, 0, 0, 0)),
            pl.BlockSpec((1, 1, LANES), lambda k, i: (k, 0, 0)),
            pl.BlockSpec((1, nr, 1), lambda k, i: (i, 0, 0)),
            pl.BlockSpec((1, LANES), lambda k, i: (0, 0)),
        ],
        out_specs=pl.BlockSpec((1, 1, nr, LANES), lambda k, i: (k, i, 0, 0)),
        compiler_params=_params(("arbitrary", "arbitrary")),
        name="compress",
    )(r, pe, w1, b1, w2p, b2t, pos_cmp, invf)


def _nsa_kernel(qt_ref, kc_ref, vct_ref, ksa_ref, vst_ref, kw_ref, vwt_ref, g_ref, o_ref, acc_ref, out_ref,
                sa_ref, sb_ref, qa_ref, *, tq, tk, seq):
    hg = NSA_Q_PER_GROUP
    dh = NSA_HEAD_DIM
    nc = kc_ref.shape[2]
    nb = seq // SLC_BLOCK
    top_k = min(SLC_TOPK, nb)
    s0 = pl.program_id(2) * tq
    t_lane = s0 + lax.broadcasted_iota(jnp.int32, (1, tq), 1)
    gate = _sigmoid(g_ref[0, 0, 0])

    def scores(k_tile, bias):
        s = _dot(k_tile, qt_ref[0, 0, 0])
        return jnp.concatenate([s[:, h * tq:(h + 1) * tq] + bias for h in range(hg)], axis=1)

    def normalised(acc):
        return acc[:dh] / acc[dh:dh + 1]

    n_col = lax.broadcasted_iota(jnp.int32, (nc, 1), 0)
    valid_c = (n_col * CMP_STRIDE + (CMP_BLOCK - 1) <= t_lane) & (n_col < nc - 1)
    win_keys = WINDOW + tq
    w0 = pl.multiple_of(jnp.maximum(s0 - WINDOW, 0), tq)
    wpos = w0 + lax.broadcasted_iota(jnp.int32, (win_keys, 1), 0)
    bias_w = jnp.where((wpos <= t_lane) & (wpos > t_lane - WINDOW), 0.0, NEG_INF)
    s_c = scores(kc_ref[0, 0], jnp.where(valid_c, 0.0, NEG_INF))
    s_w = scores(kw_ref[0, 0, pl.ds(w0, win_keys), :], bias_w)

    e_c = jnp.exp2(s_c - jnp.max(s_c, axis=0, keepdims=True))
    t_all = s0 + (lax.broadcasted_iota(jnp.int32, (1, hg * tq), 1) & (tq - 1))
    row_ok = t_all >= CMP_BLOCK - 1
    pn = e_c * jnp.where(row_ok, 1.0 / jnp.sum(e_c, axis=0, keepdims=True), 0.0)
    out_ref[...] = gate[0:1] * _dot(vct_ref[0, 0], pn.astype(BF16))[:dh]
    p_sum = pn[:, 0:tq]
    for h in range(1, hg):
        p_sum = p_sum + pn[:, h * tq:(h + 1) * tq]

    j_col = lax.broadcasted_iota(jnp.int32, (nb, 1), 0)
    n_row = lax.broadcasted_iota(jnp.int32, (1, nc), 1) * CMP_STRIDE
    overlap = (n_row < j_col * SLC_BLOCK + SLC_BLOCK) & (n_row + CMP_BLOCK > j_col * SLC_BLOCK)
    overlap = jnp.where(overlap, 1.0, 0.0).astype(BF16)
    p_hi = p_sum.astype(BF16)
    p_lo = (p_sum - p_hi.astype(F32)).astype(BF16)
    imp = _dot(overlap, p_hi) + _dot(overlap, p_lo)

    p_w = jnp.exp2(s_w - jnp.max(s_w, axis=0, keepdims=True)).astype(BF16)
    out_ref[...] += gate[2:3] * normalised(_dot(vwt_ref[0, 0, :, pl.ds(w0, win_keys)], p_w))

    cur = t_lane >> 6
    causal_b = j_col <= cur
    forced = (j_col == 0) | (causal_b & (j_col > cur - N_LOCAL_BLOCKS))
    score = jnp.where(forced, FORCE_SCORE, jnp.where(causal_b, imp, -1.0))
    j_f = jnp.broadcast_to(j_col.astype(F32), (nb, tq))
    for _ in range(top_k):
        mx = jnp.max(score, axis=0, keepdims=True)
        first_idx = jnp.min(jnp.where(score == mx, j_f, float(nb)), axis=0, keepdims=True)
        score = jnp.where(j_f == first_idx, -jnp.inf, score)
    sel = jnp.where(causal_b & (score == -jnp.inf), 1.0, 0.0)

    sel_bias = ((sel - 1.0) * (-NEG_INF)).astype(BF16)
    qa_ref[:dh] = qt_ref[0, 0, 0]
    qa_ref[dh:] = jnp.concatenate([sel_bias] * hg, axis=1)
    k_col = lax.broadcasted_iota(jnp.int32, (tk, 1), 0)

    def put_scores(k0, buf_ref):
        buf_ref[...] = _dot(ksa_ref[0, 0, pl.ds(pl.multiple_of(k0, tk), tk), :], qa_ref[...])

    def consume(k0, buf_ref, m_old, causal):
        s = buf_ref[...]
        if causal:
            bias = jnp.where(k0 + k_col <= t_lane, 0.0, NEG_INF)
            s = jnp.concatenate([s[:, h * tq:(h + 1) * tq] + bias for h in range(hg)], axis=1)
        m_new = jnp.maximum(m_old, jnp.max(s, axis=0, keepdims=True))
        pv = _dot(vst_ref[0, 0, :, pl.ds(pl.multiple_of(k0, tk), tk)], jnp.exp2(s - m_new).astype(BF16))
        acc_ref[...] = jnp.exp2(m_old - m_new) * acc_ref[...] + pv
        return m_new

    def slc_pair(it, m):
        k0 = it * (2 * tk)
        put_scores(k0 + tk, sb_ref)
        m = consume(k0, sa_ref, m, False)
        put_scores(k0 + 2 * tk, sa_ref)
        return consume(k0 + tk, sb_ref, m, False)

    acc_ref[...] = jnp.zeros_like(acc_ref)
    put_scores(jnp.int32(0), sa_ref)
    d_tile = s0 // tk
    n_full = d_tile // 2
    m_s = lax.fori_loop(0, n_full, slc_pair, jnp.full((1, hg * tq), NEG_INF, F32))
    e0 = n_full * (2 * tk)

    @pl.when(d_tile % 2 == 1)
    def _():
        put_scores(e0 + tk, sb_ref)
        consume(e0 + tk, sb_ref, consume(e0, sa_ref, m_s, False), True)

    @pl.when(d_tile % 2 == 0)
    def _():
        consume(e0, sa_ref, m_s, True)

    o_ref[0, 0, 0] = (out_ref[...] + gate[1:2] * normalised(acc_ref[...])).astype(o_ref.dtype)


def _nsa(qt, kc, vct, ksa, vst, kw, vwt, gates, tq, tk):
    b, g, nq, dh, lanes = qt.shape
    hg = lanes // tq
    s = nq * tq
    nr = kc.shape[2]
    vr = vst.shape[2]
    k_spec = pl.BlockSpec((1, 1, s, dh), lambda i, j, k: (i, j, 0, 0))
    vt_spec = pl.BlockSpec((1, 1, vr, s), lambda i, j, k: (i, j, 0, 0))
    return pl.pallas_call(
        functools.partial(_nsa_kernel, tq=tq, tk=tk, seq=s),
        out_shape=jax.ShapeDtypeStruct((b, g, nq, dh, hg * tq), BF16),
        grid=(b, g, nq),
        in_specs=[
            pl.BlockSpec((1, 1, 1, dh, hg * tq), lambda i, j, k: (i, j, k, 0, 0)),
            pl.BlockSpec((1, 1, nr, dh), lambda i, j, k: (i, j, 0, 0)),
            pl.BlockSpec((1, 1, vr, nr), lambda i, j, k: (i, j, 0, 0)),
            pl.BlockSpec((1, 1, s, ksa.shape[3]), lambda i, j, k: (i, j, 0, 0)), vt_spec, k_spec, vt_spec,
            pl.BlockSpec((1, 1, 1, 3, hg * tq), lambda i, j, k: (i, j, k, 0, 0)),
        ],
        out_specs=pl.BlockSpec((1, 1, 1, dh, hg * tq), lambda i, j, k: (i, j, k, 0, 0)),
        scratch_shapes=[pltpu.VMEM((vr, hg * tq), F32), pltpu.VMEM((dh, hg * tq), F32),
                        pltpu.VMEM((tk, hg * tq), F32), pltpu.VMEM((tk, hg * tq), F32),
                        pltpu.VMEM((ksa.shape[3], hg * tq), BF16)],
        compiler_params=_params(("arbitrary", "arbitrary", "arbitrary")),
        name="nsa",
    )(qt, kc, vct, ksa, vst, kw, vwt, gates)


def _cumsum_rows(x):
    n = x.shape[0]
    row = lax.broadcasted_iota(jnp.int32, x.shape, 0)
    d = 1
    while d < n:
        x = x + jnp.where(row >= d, pltpu.roll(x, d, 0), 0.0)
        d *= 2
    return x


def _hgrn_kernel(q_ref, f_ref, i_ref, g_ref, lb_ref, gn_ref, o_ref, st_ref, *, n_chunks):
    @pl.when(pl.program_id(1) == 0)
    def _():
        st_ref[...] = jnp.zeros_like(st_ref)

    c_len = HGRN_CHUNK
    sub = HGRN_SUB
    lbl = lb_ref[...]
    lb_e = jnp.exp(lbl - jnp.max(lbl, axis=0, keepdims=True))
    lb_all = lb_e[0:1] / jnp.sum(lb_e, axis=0, keepdims=True)

    items = [(c, h) for c in range(n_chunks) for h in range(HGRN_HEADS)]
    wave1 = {}
    for c, h in items:
        rs = slice(c * c_len, (c + 1) * c_len)
        ls = slice(h * HGRN_DK, (h + 1) * HGRN_DK)
        lb = lb_all[:, ls]
        f = lb + (1.0 - lb) * _sigmoid(f_ref[rs, ls])
        k = 1.0 - f
        b = _cumsum_rows(jnp.log(f))
        q = q_ref[rs, ls]
        v32 = i_ref[rs, ls]
        b_end = b[c_len - 1:c_len]
        attn = []
        for blk in range(c_len // sub):
            lo, hi = blk * sub, (blk + 1) * sub
            mid = lo + sub // 2
            beta = b[mid - 1:mid]
            qd = (q[lo:hi] * jnp.exp(b[lo:hi] - beta)).astype(BF16)
            kd = (k[:hi] * jnp.exp(beta - b[:hi])).astype(BF16)
            attn.append(_dot_nt(qd, kd))
        update = _dot(v32.T.astype(BF16), (k * jnp.exp(b_end - b)).astype(BF16))
        wave1[c, h] = ((q * jnp.exp(b)).astype(BF16), jnp.exp(b_end), update, attn)

    o_inter = {}
    for h in range(HGRN_HEADS):
        st = st_ref[h]
        for c in range(n_chunks):
            q_dec, decay, update, _ = wave1[c, h]
            o_inter[c, h] = _dot_nt(q_dec, st.astype(BF16))
            st = st * decay + update
        st_ref[h] = st

    for c, h in items:
        rs = slice(c * c_len, (c + 1) * c_len)
        ls = slice(h * HGRN_DK, (h + 1) * HGRN_DK)
        v = i_ref[rs, ls].astype(BF16)
        pieces = []
        for blk, a in enumerate(wave1[c, h][3]):
            lo, hi = blk * sub, (blk + 1) * sub
            ti = lax.broadcasted_iota(jnp.int32, (sub, hi), 0)
            si = lax.broadcasted_iota(jnp.int32, (sub, hi), 1)
            pieces.append(_dot(jnp.where(si <= ti + lo, a, 0.0).astype(BF16), v[:hi]))
        o = o_inter[c, h] + jnp.concatenate(pieces, axis=0)
        gate = g_ref[rs, ls]
        o_ref[rs, ls] = (_rms(o, gn_ref[...]) * (gate * _sigmoid(gate))).astype(o_ref.dtype)


def _hgrn(p, lb_logits, gn, batch, seq, n_chunks):
    t = p.shape[0]
    rows = n_chunks * HGRN_CHUNK
    steps = seq // rows

    def col(cb):
        return pl.BlockSpec((rows, HGRN_W), lambda i, j: (i * steps + j, cb))

    return pl.pallas_call(
        functools.partial(_hgrn_kernel, n_chunks=n_chunks),
        out_shape=jax.ShapeDtypeStruct((t, HGRN_W), BF16),
        grid=(batch, steps),
        in_specs=[
            col(COL_HQ // HGRN_W), col(COL_HF // HGRN_W), col(COL_HI // HGRN_W), col(COL_HG // HGRN_W),
            pl.BlockSpec(lb_logits.shape, lambda i, j: (0, 0)),
            pl.BlockSpec((1, HGRN_DV), lambda i, j: (0, 0)),
        ],
        out_specs=pl.BlockSpec((rows, HGRN_W), lambda i, j: (i * steps + j, 0)),
        scratch_shapes=[pltpu.VMEM((HGRN_HEADS, HGRN_DV, HGRN_DK), F32)],
        compiler_params=_params(("arbitrary", "arbitrary")),
        name="hgrn",
    )(p, p, p, p, lb_logits, gn)


def _merge_kernel(x_ref, ga_ref, gb_ref, yn_ref, yh_ref, wn_ref, wh_ref, wo_ref, o_ref):
    mixed = _sigmoid(ga_ref[...]) * _dot(yn_ref[...], wn_ref[...]) + _sigmoid(gb_ref[...]) * _dot(yh_ref[...], wh_ref[...])
    o_ref[...] = x_ref[...] + _dot(mixed.astype(BF16), wo_ref[...])


def _merge(x, p, y_nsa, y_hgrn, wn, wh, wo, tm):
    t, d = x.shape
    full = lambda a: pl.BlockSpec(a.shape, lambda i: (0, 0))
    return pl.pallas_call(
        _merge_kernel,
        out_shape=jax.ShapeDtypeStruct((t, d), F32),
        grid=(t // tm,),
        in_specs=[
            pl.BlockSpec((tm, d), lambda i: (i, 0)),
            pl.BlockSpec((tm, d), lambda i: (i, COL_GA // d)),
            pl.BlockSpec((tm, d), lambda i: (i, COL_GB // d)),
            pl.BlockSpec((tm, NSA_Q_W), lambda i: (i, 0)),
            pl.BlockSpec((tm, HGRN_W), lambda i: (i, 0)),
            full(wn), full(wh), full(wo),
        ],
        out_specs=pl.BlockSpec((tm, d), lambda i: (i, 0)),
        compiler_params=_params(("arbitrary",)),
        name="merge",
    )(x, p, p, y_nsa, y_hgrn, wn, wh, wo)


def _xattn_kernel(x_ref, g_ref, wq_ref, kv_ref, wo_ref, o_ref):
    x = x_ref[...]
    xq = _dot(_rms(x, g_ref[...]).astype(BF16), wq_ref[...]).astype(BF16)
    heads = [slice(h * XA_HEAD_DIM, (h + 1) * XA_HEAD_DIM) for h in range(XA_HEADS)]
    scores = [_dot_nt(xq[:, ls], kv_ref[0, :, ls]) * (XA_HEAD_DIM ** -0.5) for ls in heads]
    outs = []
    for h, s in enumerate(scores):
        e = jnp.exp(s - jnp.max(s, axis=-1, keepdims=True))
        p = e / jnp.sum(e, axis=-1, keepdims=True)
        outs.append(_dot(p.astype(BF16), kv_ref[0, :, XA_W + h * XA_HEAD_DIM:XA_W + (h + 1) * XA_HEAD_DIM]))
    o_x = jnp.concatenate(outs, axis=-1)
    o_ref[...] = x + _dot(o_x.astype(BF16), wo_ref[...])


def _xattn(x, g, wq, kv, wo, seq, tm):
    t, d = x.shape
    steps = seq // tm
    full = lambda a: pl.BlockSpec(a.shape, lambda i: (0, 0))
    return pl.pallas_call(
        _xattn_kernel,
        out_shape=jax.ShapeDtypeStruct((t, d), F32),
        grid=(t // tm,),
        in_specs=[
            pl.BlockSpec((tm, d), lambda i: (i, 0)),
            full(g), full(wq),
            pl.BlockSpec((1,) + kv.shape[1:], lambda i: (i // steps, 0, 0)),
            full(wo),
        ],
        out_specs=pl.BlockSpec((tm, d), lambda i: (i, 0)),
        compiler_params=_params(("arbitrary",)),
        name="xattn",
    )(x, g, wq, kv, wo)


def _router_kernel(x_ref, g_ref, w_ref, b_ref, hm_ref, rown_ref, rowt_ref, wtt_ref, cnt_ref):
    tm = x_ref.shape[0]
    hm = _rms(x_ref[...], g_ref[...]).astype(BF16)
    hm_ref[...] = hm
    lane = lax.broadcasted_iota(jnp.int32, (tm, LANES), 1)
    lane_f = lane.astype(F32)
    logits = _dot(hm, w_ref[...]) + b_ref[...]
    logits = jnp.where(lane < N_EXPERTS, logits, -jnp.inf)
    picks, vals = [], []
    onehot_all = jnp.zeros((tm, LANES), F32)
    for _ in range(TOP_K):
        mx = jnp.max(logits, axis=-1, keepdims=True)
        first_idx = jnp.min(jnp.where(logits == mx, lane_f, float(LANES)), axis=-1, keepdims=True)
        hit = lane_f == first_idx
        onehot = jnp.where(hit, 1.0, 0.0)
        logits = jnp.where(hit, -jnp.inf, logits)
        picks.append((first_idx, onehot))
        vals.append(mx)
        onehot_all = onehot_all + onehot
    exps = [jnp.exp(v - vals[0]) for v in vals]
    den = exps[0]
    for e in exps[1:]:
        den = den + e
    r_i = lax.broadcasted_iota(jnp.int32, (tm, tm), 0)
    c_i = lax.broadcasted_iota(jnp.int32, (tm, tm), 1)
    lower = jnp.where(c_i < r_i, 1.0, 0.0).astype(BF16)
    before = _dot(lower, onehot_all.astype(BF16))
    counts = jnp.sum(onehot_all, axis=0, keepdims=True)
    cnt_ref[0] = counts
    seg_rows = jnp.floor((counts + (SEG_ALIGN - 1)) * (1.0 / SEG_ALIGN)) * SEG_ALIGN
    e_r = lax.broadcasted_iota(jnp.int32, (LANES, LANES), 0)
    e_c = lax.broadcasted_iota(jnp.int32, (LANES, LANES), 1)
    earlier = jnp.where(e_r < e_c, 1.0, 0.0).astype(BF16)
    seg_start = _dot(jnp.broadcast_to(seg_rows, (SEG_ALIGN, LANES)).astype(BF16), earlier)[0:1]
    row_out = jnp.zeros((tm, LANES), F32)
    wt_out = jnp.zeros((tm, LANES), F32)
    for k in range(TOP_K):
        _, onehot = picks[k]
        row_k = jnp.sum(onehot * (before + seg_start), axis=-1, keepdims=True)
        row_out = jnp.where(lane == k, row_k, row_out)
        wt_out = jnp.where(lane == k, exps[k] / den, wt_out)
    rown_ref[...] = row_out.astype(jnp.int32)
    rowt_ref[0] = row_out.T[:SEG_ALIGN].astype(jnp.int32)
    wtt_ref[0] = wt_out.T[:SEG_ALIGN]


def _router(x, g, w, b, tm):
    t, d = x.shape
    full = lambda a: pl.BlockSpec(a.shape, lambda i: (0, 0))
    lane_out = pl.BlockSpec((tm, LANES), lambda i: (i, 0))
    per_tile = pl.BlockSpec((1, SEG_ALIGN, tm), lambda i: (i, 0, 0))
    return pl.pallas_call(
        _router_kernel,
        out_shape=(
            jax.ShapeDtypeStruct((t, d), BF16),
            jax.ShapeDtypeStruct((t, LANES), jnp.int32),
            jax.ShapeDtypeStruct((t // tm, SEG_ALIGN, tm), jnp.int32),
            jax.ShapeDtypeStruct((t // tm, SEG_ALIGN, tm), F32),
            jax.ShapeDtypeStruct((t // tm, 1, LANES), F32),
        ),
        grid=(t // tm,),
        in_specs=[pl.BlockSpec((tm, d), lambda i: (i, 0)), full(g), full(w), full(b)],
        out_specs=(pl.BlockSpec((tm, d), lambda i: (i, 0)), lane_out, per_tile, per_tile,
                   pl.BlockSpec((1, 1, LANES), lambda i: (i, 0, 0))),
        compiler_params=_params(("arbitrary",)),
        name="router",
    )(x, g, w, b)


def _slots_by_row(row_t, values_t, rows):
    r = lax.broadcasted_iota(jnp.int32, (rows, row_t.shape[1]), 0)
    onehot = jnp.zeros(r.shape, F32)
    weighted = jnp.zeros(r.shape, F32)
    for k in range(TOP_K):
        hit = r == row_t[k:k + 1]
        onehot = jnp.where(hit, 1.0, onehot)
        weighted = jnp.where(hit, values_t[k:k + 1], weighted)
    return onehot, weighted


def _slots_by_token(row_n, rows):
    r = lax.broadcasted_iota(jnp.int32, (row_n.shape[0], rows), 1)
    onehot = jnp.zeros(r.shape, F32)
    for k in range(TOP_K):
        onehot = jnp.where(r == row_n[:, k:k + 1], 1.0, onehot)
    return onehot


def _segment_copies(src_ref, n8_ref, dst_ref, make_copy, tile_tokens, table_row=None):
    base = (pl.program_id(0) if table_row is None else table_row) * N_EXPERTS
    sizes = []
    size = tile_tokens
    while size >= SEG_ALIGN:
        sizes.append(size)
        size //= 2

    def visit(e, start):
        n8 = n8_ref[base + e]
        src = src_ref[base + e]
        dst = dst_ref[base + e]
        for size in sizes:
            done = n8 & (-2 * size)

            @pl.when((n8 & size) != 0)
            def _():
                cp = make_copy(pl.multiple_of(src + done, SEG_ALIGN), pl.multiple_of(dst + done, SEG_ALIGN), size)
                if start:
                    cp.start()
                else:
                    cp.wait()

    def start_all(e, c):
        visit(e, True)
        return c

    def wait_all(e, c):
        visit(e, False)
        return c

    return start_all, wait_all


def _dispatch_kernel(src_ref, n8_ref, dst_ref, nu_ref, hm_ref, rowt_ref, wtt_ref, xs_ref, buf_ref, sem,
                     *, tm, n_steps):
    rows = buf_ref.shape[1]
    d = hm_ref.shape[1]
    tt = hm_ref.shape[0]
    n_tiles = xs_ref.shape[0] // tm
    step = pl.program_id(0)
    slot = step % 2

    def copies(table_row, buf_slot, tile_tokens=tt):
        def make_copy(src, dst, size):
            return pltpu.make_async_copy(buf_ref.at[buf_slot, pl.ds(src, size)], xs_ref.at[pl.ds(dst, size)],
                                         sem.at[buf_slot])
        return _segment_copies(src_ref, n8_ref, dst_ref, make_copy, tile_tokens, table_row) + (make_copy,)

    @pl.when(step >= 2)
    def _():
        lax.fori_loop(0, N_EXPERTS, copies(step - 2, slot)[1], 0)

    onehot, weighted = _slots_by_row(rowt_ref[0], wtt_ref[0], rows)
    buf_ref[slot, :, :d] = _dot(onehot.astype(BF16), hm_ref[...])
    w_hi = weighted.astype(BF16)
    w_lo = (weighted - w_hi.astype(F32)).astype(BF16)
    ones = jnp.ones((tt, LANES), BF16)
    buf_ref[slot, :, d:] = _dot(w_hi, ones) + _dot(w_lo, ones)
    start_cur, wait_cur, make_copy = copies(step, slot)
    lax.fori_loop(0, N_EXPERTS, start_cur, 0)

    @pl.when(step == n_steps - 1)
    def _():
        if n_steps >= 2:
            lax.fori_loop(0, N_EXPERTS, copies(step - 1, 1 - slot)[1], 0)
        lax.fori_loop(0, N_EXPERTS, wait_cur, 0)
        buf_ref[slot, :tm] = jnp.zeros((tm, buf_ref.shape[2]), F32)
        start_tail, wait_tail, _ = copies(n_steps, slot, tm)
        lax.fori_loop(0, N_EXPERTS, start_tail, 0)

        def zero_tile(i):
            return make_copy(0, pl.multiple_of(i * tm, tm), tm)

        lax.fori_loop(nu_ref[0], n_tiles, lambda i, c: (zero_tile(i).start(), c)[1], 0)
        lax.fori_loop(0, N_EXPERTS, wait_tail, 0)
        lax.fori_loop(nu_ref[0], n_tiles, lambda i, c: (zero_tile(i).wait(), c)[1], 0)


def _dispatch(tables, n_used, hm, row_t, wt_t, n_pad, tt, rows, tm):
    t, d = hm.shape
    per_tile = pl.BlockSpec((1,) + row_t.shape[1:], lambda i, *_: (i, 0, 0))
    grid_spec = pltpu.PrefetchScalarGridSpec(
        num_scalar_prefetch=4,
        grid=(t // tt,),
        in_specs=[pl.BlockSpec((tt, d), lambda i, *_: (i, 0)), per_tile, per_tile],
        out_specs=pl.BlockSpec(memory_space=pl.ANY),
        scratch_shapes=[pltpu.VMEM((2, rows, d + LANES), F32), pltpu.SemaphoreType.DMA((2,))],
    )
    return pl.pallas_call(
        functools.partial(_dispatch_kernel, tm=tm, n_steps=t // tt),
        out_shape=jax.ShapeDtypeStruct((n_pad, d + LANES), F32),
        grid_spec=grid_spec,
        compiler_params=_params(("arbitrary",)),
        name="dispatch",
    )(*tables, n_used, hm, row_t, wt_t)


def _w1_prep_kernel(w_ref, o_ref):
    grp = 2 * LANES
    r_i = lax.broadcasted_iota(jnp.int32, (grp, grp), 0)
    c_i = lax.broadcasted_iota(jnp.int32, (grp, grp), 1)
    src_col = jnp.where(c_i < LANES, 2 * c_i, 2 * (c_i - LANES) + 1)
    perm = jnp.where(r_i == src_col, 1.0, 0.0).astype(BF16)
    for c in range(w_ref.shape[1] // grp):
        sl = slice(c * grp, (c + 1) * grp)
        o_ref[:, sl] = _dot(w_ref[:, sl].astype(BF16), perm).astype(BF16)


def _w1_prep(w, tm):
    r, n = w.shape
    return pl.pallas_call(
        _w1_prep_kernel,
        out_shape=jax.ShapeDtypeStruct((r, n), BF16),
        grid=(r // tm,),
        in_specs=[pl.BlockSpec((tm, n), lambda i: (i, 0))],
        out_specs=pl.BlockSpec((tm, n), lambda i: (i, 0)),
        compiler_params=_params(("arbitrary",)),
        name="w1_prep",
    )(w)


def _ffn_kernel(te_ref, nu_ref, x_ref, w1_ref, b1_ref, w2_ref, b2_ref, o_ref):
    del te_ref
    used = pl.program_id(0) < nu_ref[0]

    @pl.when(used)
    def _():
        d = o_ref.shape[1]
        u = _dot(x_ref[:, :d].astype(BF16), w1_ref[0]) + b1_ref[0]
        acts = []
        for c in range(u.shape[1] // (2 * LANES)):
            glu = jnp.minimum(u[:, 2 * c * LANES:(2 * c + 1) * LANES], SWIGLU_LIMIT)
            lin = jnp.clip(u[:, (2 * c + 1) * LANES:(2 * c + 2) * LANES], -SWIGLU_LIMIT, SWIGLU_LIMIT)
            acts.append((glu * _sigmoid(SWIGLU_ALPHA * glu) * (lin + 1.0)).astype(BF16))
        y = _dot(jnp.concatenate(acts, axis=-1), w2_ref[0]) + b2_ref[0]
        weight = x_ref[:, d:]
        o_ref[...] = jnp.concatenate([y[:, c * LANES:(c + 1) * LANES] * weight for c in range(d // LANES)], axis=1)

    @pl.when(jnp.logical_not(used))
    def _():
        o_ref[...] = jnp.zeros_like(o_ref)


def _ffn(tile_expert, n_used, xs, w1, b1, w2, b2, tm):
    n_pad = xs.shape[0]
    d = w1.shape[1]
    f2 = w1.shape[2]
    f = w2.shape[1]
    grid_spec = pltpu.PrefetchScalarGridSpec(
        num_scalar_prefetch=2,
        grid=(n_pad // tm,),
        in_specs=[
            pl.BlockSpec((tm, xs.shape[1]), lambda i, te, nu: (jnp.minimum(i, nu[0] - 1), 0)),
            pl.BlockSpec((1, d, f2), lambda i, te, nu: (te[i], 0, 0)),
            pl.BlockSpec((1, 1, f2), lambda i, te, nu: (te[i], 0, 0)),
            pl.BlockSpec((1, f, d), lambda i, te, nu: (te[i], 0, 0)),
            pl.BlockSpec((1, 1, d), lambda i, te, nu: (te[i], 0, 0)),
        ],
        out_specs=pl.BlockSpec((tm, d), lambda i, te, nu: (i, 0)),
    )
    return pl.pallas_call(
        _ffn_kernel,
        out_shape=jax.ShapeDtypeStruct((n_pad, d), F32),
        grid_spec=grid_spec,
        compiler_params=_params(("arbitrary",)),
        name="expert_ffn",
    )(tile_expert, n_used, xs, w1, b1, w2, b2)


def _combine_kernel(src_ref, n8_ref, dst_ref, ys_ref, rown_ref, x_ref, g_ref, o_ref, buf_ref, sem, *, n_steps):
    rows = buf_ref.shape[1]
    step = pl.program_id(0)
    slot = step % 2

    def copies(table_row, buf_slot):
        def make_copy(src, dst, size):
            return pltpu.make_async_copy(ys_ref.at[pl.ds(dst, size)], buf_ref.at[buf_slot, pl.ds(src, size)],
                                         sem.at[buf_slot])
        return _segment_copies(src_ref, n8_ref, dst_ref, make_copy, x_ref.shape[0], table_row)

    @pl.when(step == 0)
    def _():
        buf_ref[...] = jnp.zeros_like(buf_ref)
        lax.fori_loop(0, N_EXPERTS, copies(0, 0)[0], 0)

    @pl.when(step + 1 < n_steps)
    def _():
        lax.fori_loop(0, N_EXPERTS, copies(step + 1, 1 - slot)[0], 0)

    onehot = _slots_by_token(rown_ref[...], rows).astype(BF16)
    lax.fori_loop(0, N_EXPERTS, copies(step, slot)[1], 0)
    ys = buf_ref[slot]
    y_hi = ys.astype(BF16)
    y_lo = (ys - y_hi.astype(F32)).astype(BF16)
    y = _dot(onehot, y_hi) + _dot(onehot, y_lo)
    o_ref[...] = _rms(x_ref[...] + y, g_ref[...])


def _combine(tables, ys, row_n, x, g, tt, rows):
    t, d = x.shape
    tile = lambda w: pl.BlockSpec((tt, w), lambda i, *_: (i, 0))
    grid_spec = pltpu.PrefetchScalarGridSpec(
        num_scalar_prefetch=3,
        grid=(t // tt,),
        in_specs=[pl.BlockSpec(memory_space=pl.ANY), tile(LANES), tile(d),
                  pl.BlockSpec((1, d), lambda i, *_: (0, 0))],
        out_specs=tile(d),
        scratch_shapes=[pltpu.VMEM((2, rows, d), F32), pltpu.SemaphoreType.DMA((2,))],
    )
    return pl.pallas_call(
        functools.partial(_combine_kernel, n_steps=t // tt),
        out_shape=jax.ShapeDtypeStruct((t, d), F32),
        grid_spec=grid_spec,
        compiler_params=_params(("arbitrary",)),
        name="combine",
    )(*tables, ys, row_n, x, g)


def _tile_sizes(seq):
    return dict(
        tm_proj=1024, tn_proj=P_WIDTH // 4,
        tm_rope=512,
        tq=256, tk=512,
        hgrn_chunks=4,
        tm_merge=512, tm_xattn=512,
        tm_router=512,
        tm_w1_prep=512,
        tm_ffn=512,
    )


def _layer(x, mem, positions, ts, mix_norm_g, w_in, cmp_pe, cmp_w1, cmp_b1, cmp_w2, cmp_b2, lb_logits, hgrn_norm_g,
           w_up_nsa, w_up_hgrn, w_out, xa_norm_g, xa_mem_norm_g, w_xq, w_xkv, w_xo, moe_norm_g, router_w, router_b,
           moe_w1, moe_b1, moe_w2, moe_b2, out_norm_g):
    b, s, d = x.shape
    t = b * s
    g, hg, dh = NSA_KV_GROUPS, NSA_Q_PER_GROUP, NSA_HEAD_DIM
    x2 = x.reshape(t, d)
    row = lambda v: v.reshape(1, -1).astype(F32)

    splits = [0]
    for w in (d, d, NSA_Q_W) + (NSA_KV_W,) * 6 + (3 * NSA_HEADS,) + (HGRN_W,) * 4:
        splits.append(splits[-1] + w)
    seg = lambda i: w_in[:, splits[i]:splits[i + 1]]
    (ga, gb, nq, kc, vc, ks, vs, kw, vw, ng, hq, hf, hi, hgate) = [seg(i) for i in range(14)]
    pad = jnp.zeros((d, P_WIDTH - COL_NG - 3 * NSA_HEADS), w_in.dtype)
    w_p = jnp.concatenate([ga, gb, hq, hf, hi, hgate, nq, ks, kw, kc, vc, vs, vw, ng, pad], axis=1).astype(BF16)

    p = _norm_matmul(x2, row(mix_norm_g), w_p, F32, ts["tm_proj"], ts["tn_proj"], "in_proj")

    half = dh // 2
    inv_freq = ROPE_THETA ** (-jnp.arange(half, dtype=F32) / half)
    invf = jnp.tile(inv_freq, LANES // half).reshape(1, LANES)
    q_r, kk_r = _rope(p, positions.reshape(t, 1), invf, ts["tm_rope"])
    tq = ts["tq"]
    nq = s // tq
    qt = q_r.reshape(b, nq, tq, g, hg, dh).transpose(0, 3, 1, 5, 4, 2).reshape(b, g, nq, dh, hg * tq)
    kk = kk_r.reshape(b, s, 2 * g, dh).transpose(0, 2, 1, 3)
    k_slc, k_win = kk[:, :g], kk[:, g:]
    nb = s // SLC_BLOCK
    block_onehot = (jnp.arange(s)[:, None] // SLC_BLOCK == jnp.arange(nb)[None, :]).astype(BF16)
    ks_aug = jnp.concatenate([k_slc, jnp.broadcast_to(block_onehot, (b, g, s, nb))], axis=-1)

    def values_t(v):
        vt = v.transpose(0, 1, 3, 2).astype(BF16)
        ones = jnp.ones(vt.shape[:2] + (1, vt.shape[3]), BF16)
        zeros = jnp.zeros(vt.shape[:2] + (BF16_SUBLANES - 1, vt.shape[3]), BF16)
        return jnp.concatenate([vt, ones, zeros], axis=2)

    vvt = values_t(p[:, COL_VSVW:COL_VSVW + 2 * NSA_KV_W].reshape(b, s, 2 * g, dh).transpose(0, 2, 1, 3))
    vt_slc, vt_win = vvt[:, :g], vvt[:, g:]
    gates = p[:, COL_NG:COL_NG + 3 * NSA_HEADS].reshape(b, nq, tq, g, hg, 3).transpose(0, 3, 1, 5, 4, 2)
    gates = gates.reshape(b, g, nq, 3, hg * tq)

    nr = s // CMP_STRIDE
    kcvc = p[:, COL_KCVC:COL_KCVC + 2 * NSA_KV_W].reshape(b, s, 2, g, dh).transpose(2, 0, 3, 1, 4)
    r = kcvc.reshape(2, b, g, nr, CMP_STRIDE * dh)
    pe = cmp_pe.reshape(2, 2, 1, CMP_STRIDE * dh)
    zeros_w2 = jnp.zeros_like(cmp_w2)
    w2p = jnp.stack([jnp.concatenate([cmp_w2, zeros_w2], axis=-1),
                     jnp.concatenate([zeros_w2, cmp_w2], axis=-1)], axis=1).astype(BF16)
    b2t = jnp.tile(cmp_b2, (1, g)).reshape(2, 1, LANES)
    pos_cmp = positions[:, CMP_BLOCK - 1::CMP_STRIDE]
    pos_cmp = jnp.pad(pos_cmp, ((0, 0), (0, nr - pos_cmp.shape[1]))).reshape(b, nr, 1)
    cmp = _compress(r, pe, cmp_w1.astype(BF16), cmp_b1.reshape(2, 1, CMP_HIDDEN), w2p, b2t, pos_cmp, invf)
    cmp = cmp.reshape(2, b, nr, g, dh).transpose(0, 1, 3, 2, 4)

    y_nsa = _nsa(qt, cmp[0], values_t(cmp[1]), ks_aug, vt_slc, k_win, vt_win, gates, tq, ts["tk"])
    y_nsa = y_nsa.reshape(b, g, nq, dh, hg, tq).transpose(0, 2, 5, 1, 4, 3).reshape(t, NSA_Q_W)

    y_hgrn = _hgrn(p, lb_logits.astype(F32), row(hgrn_norm_g), b, s, ts["hgrn_chunks"])

    x2 = _merge(x2, p, y_nsa, y_hgrn, w_up_nsa.astype(BF16), w_up_hgrn.astype(BF16), w_out.astype(BF16),
                ts["tm_merge"])

    n_mem = mem.shape[1]
    kv = _norm_matmul(mem.reshape(b * n_mem, d), row(xa_mem_norm_g), w_xkv.astype(BF16), BF16,
                      n_mem, 2 * XA_W, "mem_kv").reshape(b, n_mem, 2 * XA_W)
    x2 = _xattn(x2, row(xa_norm_g), w_xq.astype(BF16), kv, w_xo.astype(BF16), s, ts["tm_xattn"])

    n_exp = router_w.shape[1]
    rw = jnp.pad(router_w, ((0, 0), (0, LANES - n_exp))).astype(BF16)
    rb = jnp.pad(router_b, (0, LANES - n_exp)).reshape(1, LANES).astype(F32)
    assert n_exp == N_EXPERTS
    tt = ts["tm_router"]
    nt = t // tt
    hm, row_n, row_t, wt_t, cnt = _router(x2, row(moe_norm_g), rw, rb, tt)
    tm = ts["tm_ffn"]
    n8 = (cnt[:, 0, :n_exp].astype(jnp.int32) + SEG_ALIGN - 1) // SEG_ALIGN * SEG_ALIGN
    src_off = jnp.cumsum(n8, axis=1) - n8
    region = jnp.sum(n8, axis=0)
    padded = (region + tm - 1) // tm * tm
    ends = jnp.cumsum(padded)
    dst_off = (ends - padded)[None, :] + jnp.cumsum(n8, axis=0) - n8
    tables = tuple(jnp.concatenate([a, tail[None, :]], axis=0).reshape(-1) for a, tail in
                   ((src_off, jnp.zeros_like(region)), (n8, padded - region), (dst_off, ends - padded + region)))
    rows = tt * TOP_K + n_exp * SEG_ALIGN
    n_pad = (t * TOP_K + nt * n_exp * SEG_ALIGN + n_exp * tm + tm - 1) // tm * tm
    n_tiles = n_pad // tm
    tile_ids = jnp.arange(n_tiles, dtype=jnp.int32)
    tile_expert = jnp.sum(((ends // tm)[None, :] <= tile_ids[:, None]).astype(jnp.int32), axis=1)
    tile_expert = jnp.minimum(tile_expert, n_exp - 1)
    n_used = (ends[-1] // tm).reshape(1).astype(jnp.int32)

    xs = _dispatch(tables, n_used, hm, row_t, wt_t, n_pad, tt, rows, tm)
    f = moe_w2.shape[1]
    w1p = _w1_prep(moe_w1.reshape(n_exp * d, 2 * f), ts["tm_w1_prep"]).reshape(n_exp, d, 2 * f)
    b1p = moe_b1.reshape(n_exp, f // LANES, LANES, 2).transpose(0, 1, 3, 2).reshape(n_exp, 1, 2 * f)
    ys = _ffn(tile_expert, n_used, xs, w1p, b1p, moe_w2.astype(BF16), moe_b2.reshape(n_exp, 1, d), tm)
    out = _combine(tables, ys, row_n, x2, row(out_norm_g), tt, rows)
    return out.reshape(b, s, d)


def kernel(x, mem, positions, mix_norm_g, w_in, cmp_pe, cmp_w1, cmp_b1, cmp_w2, cmp_b2, hgrn_lb_logits, hgrn_norm_g, w_up_nsa, w_up_hgrn, w_out, xa_norm_g, xa_mem_norm_g, w_xq, w_xkv, w_xo, moe_norm_g, router_w, router_b, moe_w1, moe_b1, moe_w2, moe_b2, final_norm_g):
    depth = w_in.shape[0]
    assert depth == 1, "single-layer block: the final norm is fused into the last layer's combine"
    ts = _tile_sizes(x.shape[1])
    l = 0
    return _layer(x, mem, positions, ts, mix_norm_g[l], w_in[l], cmp_pe[l], cmp_w1[l], cmp_b1[l], cmp_w2[l], cmp_b2[l],
                  hgrn_lb_logits, hgrn_norm_g[l], w_up_nsa[l], w_up_hgrn[l], w_out[l], xa_norm_g[l], xa_mem_norm_g[l],
                  w_xq[l], w_xkv[l], w_xo[l], moe_norm_g[l], router_w[l], router_b[l], moe_w1[l], moe_b1[l], moe_w2[l],
                  moe_b2[l], final_norm_g)
```

```python
import functools

import jax
import jax.numpy as jnp
from jax import lax
from jax.experimental import pallas as pl
from jax.experimental.pallas import tpu as pltpu

EPS = 1e-6
ROPE_THETA = 10000.0
NEG_INF = -1e30
FORCE_SCORE = 1e9

NSA_HEADS = 8
NSA_KV_GROUPS = 2
NSA_Q_PER_GROUP = NSA_HEADS // NSA_KV_GROUPS
NSA_HEAD_DIM = 64
CMP_BLOCK = 32
CMP_STRIDE = 16
CMP_HIDDEN = 256
SLC_BLOCK = 64
SLC_TOPK = 16
N_LOCAL_BLOCKS = 2
WINDOW = 512
NSA_Q_W = NSA_HEADS * NSA_HEAD_DIM
NSA_KV_W = NSA_KV_GROUPS * NSA_HEAD_DIM

HGRN_HEADS = 4
HGRN_DK = 128
HGRN_DV = 128
HGRN_CHUNK = 64
HGRN_SUB = 16
HGRN_W = HGRN_HEADS * HGRN_DK

XA_HEADS = 4
XA_HEAD_DIM = 128
XA_W = XA_HEADS * XA_HEAD_DIM

N_EXPERTS = 32
TOP_K = 4
SWIGLU_ALPHA = 1.702
SWIGLU_LIMIT = 7.0

LANES = 128
SEG_ALIGN = 8
BF16_SUBLANES = 16
LOG2E = 1.4426950408889634
VMEM_LIMIT = 48 * 1024 * 1024

COL_GA = 0
COL_GB = 1024
COL_HQ = 2048
COL_HF = 2560
COL_HI = 3072
COL_HG = 3584
COL_NQ = 4096
COL_KSKW = 4608
COL_KCVC = 4864
COL_VSVW = 5120
COL_NG = 5376
P_WIDTH = 5632

F32 = jnp.float32
BF16 = jnp.bfloat16


def _params(sem):
    return pltpu.CompilerParams(dimension_semantics=sem, vmem_limit_bytes=VMEM_LIMIT)


def _dot(a, b):
    return jnp.dot(a, b, preferred_element_type=F32)


def _dot_nt(a, b):
    return lax.dot_general(a, b, (((1,), (1,)), ((), ())), preferred_element_type=F32)


def _rms(xf, g):
    return xf * lax.rsqrt(jnp.mean(xf * xf, axis=-1, keepdims=True) + EPS) * g


def _sigmoid(x):
    return 1.0 / (1.0 + jnp.exp(-x))


def _norm_matmul_kernel(x_ref, g_ref, w_ref, o_ref, hn_ref):
    @pl.when(pl.program_id(1) == 0)
    def _():
        hn_ref[...] = _rms(x_ref[...], g_ref[...]).astype(BF16)

    o_ref[...] = _dot(hn_ref[...], w_ref[...]).astype(o_ref.dtype)


def _norm_matmul(x, g, w, out_dtype, tm, tn, name):
    t, d = x.shape
    n = w.shape[1]
    return pl.pallas_call(
        _norm_matmul_kernel,
        out_shape=jax.ShapeDtypeStruct((t, n), out_dtype),
        grid=(t // tm, n // tn),
        in_specs=[
            pl.BlockSpec((tm, d), lambda i, j: (i, 0)),
            pl.BlockSpec((1, d), lambda i, j: (0, 0)),
            pl.BlockSpec((d, tn), lambda i, j: (0, j)),
        ],
        out_specs=pl.BlockSpec((tm, tn), lambda i, j: (i, j)),
        scratch_shapes=[pltpu.VMEM((tm, d), BF16)],
        compiler_params=_params(("arbitrary", "arbitrary")),
        name=name,
    )(x, g, w)


def _rope_coeffs(pos_col, invf):
    ang = pos_col.astype(F32) * invf
    lane = lax.broadcasted_iota(jnp.int32, ang.shape, 1)
    first = (lane & (NSA_HEAD_DIM - 1)) < (NSA_HEAD_DIM // 2)
    c = jnp.cos(ang)
    s = jnp.sin(ang)
    return c, jnp.where(first, -s, s), first


def _rope_tile(x, c, s_signed, first):
    half = NSA_HEAD_DIM // 2
    partner = jnp.where(first, pltpu.roll(x, LANES - half, 1), pltpu.roll(x, half, 1))
    return x * c + partner * s_signed


def _rope_kernel(q_ref, k_ref, pos_ref, invf_ref, qo_ref, ko_ref, *, q_scale):
    c, s_signed, first = _rope_coeffs(pos_ref[...], invf_ref[...])
    for i in range(q_ref.shape[1] // LANES):
        sl = slice(i * LANES, (i + 1) * LANES)
        qo_ref[:, sl] = (_rope_tile(q_ref[:, sl], c, s_signed, first) * q_scale).astype(BF16)
    for i in range(k_ref.shape[1] // LANES):
        sl = slice(i * LANES, (i + 1) * LANES)
        ko_ref[:, sl] = _rope_tile(k_ref[:, sl], c, s_signed, first).astype(BF16)


def _rope(p, pos_col, invf, tm):
    t = p.shape[0]
    kw = 2 * NSA_KV_W
    return pl.pallas_call(
        functools.partial(_rope_kernel, q_scale=NSA_HEAD_DIM ** -0.5 * LOG2E),
        out_shape=(jax.ShapeDtypeStruct((t, NSA_Q_W), BF16), jax.ShapeDtypeStruct((t, kw), BF16)),
        grid=(t // tm,),
        in_specs=[
            pl.BlockSpec((tm, NSA_Q_W), lambda i: (i, COL_NQ // NSA_Q_W)),
            pl.BlockSpec((tm, kw), lambda i: (i, COL_KSKW // kw)),
            pl.BlockSpec((tm, 1), lambda i: (i, 0)),
            pl.BlockSpec((1, LANES), lambda i: (0, 0)),
        ],
        out_specs=(
            pl.BlockSpec((tm, NSA_Q_W), lambda i: (i, 0)),
            pl.BlockSpec((tm, kw), lambda i: (i, 0)),
        ),
        compiler_params=_params(("arbitrary",)),
        name="rope",
    )(p, p, pos_col, invf)


def _gelu_tanh(x):
    return 0.5 * x * (1.0 + jnp.tanh(0.7978845608028654 * (x + 0.044715 * (x * x * x))))


def _compress_kernel(r_ref, pe_ref, w1_ref, b1_ref, w2_ref, b2_ref, pos_ref, invf_ref, o_ref):
    nr = r_ref.shape[3]
    half = r_ref.shape[4]
    acc = None
    for g in range(NSA_KV_GROUPS):
        r = r_ref[0, 0, g]
        top = _dot((r + pe_ref[0, 0]).astype(BF16), w1_ref[0, :half, :])
        bot = _dot((r + pe_ref[0, 1]).astype(BF16), w1_ref[0, half:, :])
        pre = top + pltpu.roll(bot, nr - 1, 0) + b1_ref[0]
        part = _dot(_gelu_tanh(pre).astype(BF16), w2_ref[0, g])
        acc = part if acc is None else acc + part
    out = acc + b2_ref[0]
    c, s_signed, first = _rope_coeffs(pos_ref[0], invf_ref[...])
    roped = _rope_tile(out, c, s_signed, first)
    is_key = pl.program_id(0) == 0
    o_ref[0, 0] = jnp.where(is_key, roped, out).astype(BF16)


def _compress(r, pe, w1, b1, w2p, b2t, pos_cmp, invf):
    _, b, g, nr, half = r.shape
    return pl.pallas_call(
        _compress_kernel,
        out_shape=jax.ShapeDtypeStruct((2, b, nr, LANES), BF16),
        grid=(2, b),
        in_specs=[
            pl.BlockSpec((1, 1, g, nr, half), lambda k, i: (k, i, 0, 0, 0)),
            pl.BlockSpec((1, 2, 1, half), lambda k, i: (k, 0, 0, 0)),
            pl.BlockSpec((1, 2 * half, CMP_HIDDEN), lambda k, i: (k, 0, 0)),
            pl.BlockSpec((1, 1, CMP_HIDDEN), lambda k, i: (k, 0, 0)),
            pl.BlockSpec((1, g, CMP_HIDDEN, LANES), lambda k, i: (k, 0, 0, 0)),
            pl.BlockSpec((1, 1, LANES), lambda k, i: (k, 0, 0)),
            pl.BlockSpec((1, nr, 1), lambda k, i: (i, 0, 0)),
            pl.BlockSpec((1, LANES), lambda k, i: (0, 0)),
        ],
        out_specs=pl.BlockSpec((1, 1, nr, LANES), lambda k, i: (k, i, 0, 0)),
        compiler_params=_params(("arbitrary", "arbitrary")),
        name="compress",
    )(r, pe, w1, b1, w2p, b2t, pos_cmp, invf)


def _nsa_kernel(qt_ref, kc_ref, vct_ref, ksa_ref, vst_ref, kw_ref, vwt_ref, g_ref, o_ref, acc_ref, out_ref,
                sa_ref, sb_ref, qa_ref, *, tq, tk, seq):
    hg = NSA_Q_PER_GROUP
    dh = NSA_HEAD_DIM
    nc = kc_ref.shape[2]
    nb = seq // SLC_BLOCK
    top_k = min(SLC_TOPK, nb)
    s0 = pl.program_id(2) * tq
    t_lane = s0 + lax.broadcasted_iota(jnp.int32, (1, tq), 1)
    gate = _sigmoid(g_ref[0, 0, 0])

    def scores(k_tile, bias):
        s = _dot(k_tile, qt_ref[0, 0, 0])
        return jnp.concatenate([s[:, h * tq:(h + 1) * tq] + bias for h in range(hg)], axis=1)

    def normalised(acc):
        return acc[:dh] / acc[dh:dh + 1]

    n_col = lax.broadcasted_iota(jnp.int32, (nc, 1), 0)
    valid_c = (n_col * CMP_STRIDE + (CMP_BLOCK - 1) <= t_lane) & (n_col < nc - 1)
    win_keys = WINDOW + tq
    w0 = pl.multiple_of(jnp.maximum(s0 - WINDOW, 0), tq)
    wpos = w0 + lax.broadcasted_iota(jnp.int32, (win_keys, 1), 0)
    bias_w = jnp.where((wpos <= t_lane) & (wpos > t_lane - WINDOW), 0.0, NEG_INF)
    s_c = scores(kc_ref[0, 0], jnp.where(valid_c, 0.0, NEG_INF))
    s_w = scores(kw_ref[0, 0, pl.ds(w0, win_keys), :], bias_w)

    e_c = jnp.exp2(s_c - jnp.max(s_c, axis=0, keepdims=True))
    t_all = s0 + (lax.broadcasted_iota(jnp.int32, (1, hg * tq), 1) & (tq - 1))
    row_ok = t_all >= CMP_BLOCK - 1
    pn = e_c * jnp.where(row_ok, 1.0 / jnp.sum(e_c, axis=0, keepdims=True), 0.0)
    out_ref[...] = gate[0:1] * _dot(vct_ref[0, 0], pn.astype(BF16))[:dh]
    p_sum = pn[:, 0:tq]
    for h in range(1, hg):
        p_sum = p_sum + pn[:, h * tq:(h + 1) * tq]

    j_col = lax.broadcasted_iota(jnp.int32, (nb, 1), 0)
    n_row = lax.broadcasted_iota(jnp.int32, (1, nc), 1) * CMP_STRIDE
    overlap = (n_row < j_col * SLC_BLOCK + SLC_BLOCK) & (n_row + CMP_BLOCK > j_col * SLC_BLOCK)
    overlap = jnp.where(overlap, 1.0, 0.0).astype(BF16)
    p_hi = p_sum.astype(BF16)
    p_lo = (p_sum - p_hi.astype(F32)).astype(BF16)
    imp = _dot(overlap, p_hi) + _dot(overlap, p_lo)

    p_w = jnp.exp2(s_w - jnp.max(s_w, axis=0, keepdims=True)).astype(BF16)
    out_ref[...] += gate[2:3] * normalised(_dot(vwt_ref[0, 0, :, pl.ds(w0, win_keys)], p_w))

    cur = t_lane >> 6
    causal_b = j_col <= cur
    forced = (j_col == 0) | (causal_b & (j_col > cur - N_LOCAL_BLOCKS))
    score = jnp.where(forced, FORCE_SCORE, jnp.where(causal_b, imp, -1.0))
    j_f = jnp.broadcast_to(j_col.astype(F32), (nb, tq))
    for _ in range(top_k):
        mx = jnp.max(score, axis=0, keepdims=True)
        first_idx = jnp.min(jnp.where(score == mx, j_f, float(nb)), axis=0, keepdims=True)
        score = jnp.where(j_f == first_idx, -jnp.inf, score)
    sel = jnp.where(causal_b & (score == -jnp.inf), 1.0, 0.0)

    sel_bias = ((sel - 1.0) * (-NEG_INF)).astype(BF16)
    qa_ref[:dh] = qt_ref[0, 0, 0]
    qa_ref[dh:] = jnp.concatenate([sel_bias] * hg, axis=1)
    k_col = lax.broadcasted_iota(jnp.int32, (tk, 1), 0)

    def put_scores(k0, buf_ref):
        buf_ref[...] = _dot(ksa_ref[0, 0, pl.ds(pl.multiple_of(k0, tk), tk), :], qa_ref[...])

    def consume(k0, buf_ref, m_old, causal):
        s = buf_ref[...]
        if causal:
            bias = jnp.where(k0 + k_col <= t_lane, 0.0, NEG_INF)
            s = jnp.concatenate([s[:, h * tq:(h + 1) * tq] + bias for h in range(hg)], axis=1)
        m_new = jnp.maximum(m_old, jnp.max(s, axis=0, keepdims=True))
        pv = _dot(vst_ref[0, 0, :, pl.ds(pl.multiple_of(k0, tk), tk)], jnp.exp2(s - m_new).astype(BF16))
        acc_ref[...] = jnp.exp2(m_old - m_new) * acc_ref[...] + pv
        return m_new

    def slc_pair(it, m):
        k0 = it * (2 * tk)
        put_scores(k0 + tk, sb_ref)
        m = consume(k0, sa_ref, m, False)
        put_scores(k0 + 2 * tk, sa_ref)
        return consume(k0 + tk, sb_ref, m, False)

    acc_ref[...] = jnp.zeros_like(acc_ref)
    put_scores(jnp.int32(0), sa_ref)
    d_tile = s0 // tk
    n_full = d_tile // 2
    m_s = lax.fori_loop(0, n_full, slc_pair, jnp.full((1, hg * tq), NEG_INF, F32))
    e0 = n_full * (2 * tk)

    @pl.when(d_tile % 2 == 1)
    def _():
        put_scores(e0 + tk, sb_ref)
        consume(e0 + tk, sb_ref, consume(e0, sa_ref, m_s, False), True)

    @pl.when(d_tile % 2 == 0)
    def _():
        consume(e0, sa_ref, m_s, True)

    o_ref[0, 0, 0] = (out_ref[...] + gate[1:2] * normalised(acc_ref[...])).astype(o_ref.dtype)


def _nsa(qt, kc, vct, ksa, vst, kw, vwt, gates, tq, tk):
    b, g, nq, dh, lanes = qt.shape
    hg = lanes // tq
    s = nq * tq
    nr = kc.shape[2]
    vr = vst.shape[2]
    k_spec = pl.BlockSpec((1, 1, s, dh), lambda i, j, k: (i, j, 0, 0))
    vt_spec = pl.BlockSpec((1, 1, vr, s), lambda i, j, k: (i, j, 0, 0))
    return pl.pallas_call(
        functools.partial(_nsa_kernel, tq=tq, tk=tk, seq=s),
        out_shape=jax.ShapeDtypeStruct((b, g, nq, dh, hg * tq), BF16),
        grid=(b, g, nq),
        in_specs=[
            pl.BlockSpec((1, 1, 1, dh, hg * tq), lambda i, j, k: (i, j, k, 0, 0)),
            pl.BlockSpec((1, 1, nr, dh), lambda i, j, k: (i, j, 0, 0)),
            pl.BlockSpec((1, 1, vr, nr), lambda i, j, k: (i, j, 0, 0)),
            pl.BlockSpec((1, 1, s, ksa.shape[3]), lambda i, j, k: (i, j, 0, 0)), vt_spec, k_spec, vt_spec,
            pl.BlockSpec((1, 1, 1, 3, hg * tq), lambda i, j, k: (i, j, k, 0, 0)),
        ],
        out_specs=pl.BlockSpec((1, 1, 1, dh, hg * tq), lambda i, j, k: (i, j, k, 0, 0)),
        scratch_shapes=[pltpu.VMEM((vr, hg * tq), F32), pltpu.VMEM((dh, hg * tq), F32),
                        pltpu.VMEM((tk, hg * tq), F32), pltpu.VMEM((tk, hg * tq), F32),
                        pltpu.VMEM((ksa.shape[3], hg * tq), BF16)],
        compiler_params=_params(("arbitrary", "arbitrary", "arbitrary")),
        name="nsa",
    )(qt, kc, vct, ksa, vst, kw, vwt, gates)


def _cumsum_rows(x):
    n = x.shape[0]
    row = lax.broadcasted_iota(jnp.int32, x.shape, 0)
    d = 1
    while d < n:
        x = x + jnp.where(row >= d, pltpu.roll(x, d, 0), 0.0)
        d *= 2
    return x


def _hgrn_kernel(q_ref, f_ref, i_ref, g_ref, lb_ref, gn_ref, o_ref, st_ref, *, n_chunks):
    @pl.when(pl.program_id(1) == 0)
    def _():
        st_ref[...] = jnp.zeros_like(st_ref)

    c_len = HGRN_CHUNK
    sub = HGRN_SUB
    lbl = lb_ref[...]
    lb_e = jnp.exp(lbl - jnp.max(lbl, axis=0, keepdims=True))
    lb_all = lb_e[0:1] / jnp.sum(lb_e, axis=0, keepdims=True)

    items = [(c, h) for c in range(n_chunks) for h in range(HGRN_HEADS)]
    wave1 = {}
    for c, h in items:
        rs = slice(c * c_len, (c + 1) * c_len)
        ls = slice(h * HGRN_DK, (h + 1) * HGRN_DK)
        lb = lb_all[:, ls]
        f = lb + (1.0 - lb) * _sigmoid(f_ref[rs, ls])
        k = 1.0 - f
        b = _cumsum_rows(jnp.log(f))
        q = q_ref[rs, ls]
        v32 = i_ref[rs, ls]
        b_end = b[c_len - 1:c_len]
        attn = []
        for blk in range(c_len // sub):
            lo, hi = blk * sub, (blk + 1) * sub
            mid = lo + sub // 2
            beta = b[mid - 1:mid]
            qd = (q[lo:hi] * jnp.exp(b[lo:hi] - beta)).astype(BF16)
            kd = (k[:hi] * jnp.exp(beta - b[:hi])).astype(BF16)
            attn.append(_dot_nt(qd, kd))
        update = _dot(v32.T.astype(BF16), (k * jnp.exp(b_end - b)).astype(BF16))
        wave1[c, h] = ((q * jnp.exp(b)).astype(BF16), jnp.exp(b_end), update, attn)

    o_inter = {}
    for h in range(HGRN_HEADS):
        st = st_ref[h]
        for c in range(n_chunks):
            q_dec, decay, update, _ = wave1[c, h]
            o_inter[c, h] = _dot_nt(q_dec, st.astype(BF16))
            st = st * decay + update
        st_ref[h] = st

    for c, h in items:
        rs = slice(c * c_len, (c + 1) * c_len)
        ls = slice(h * HGRN_DK, (h + 1) * HGRN_DK)
        v = i_ref[rs, ls].astype(BF16)
        pieces = []
        for blk, a in enumerate(wave1[c, h][3]):
            lo, hi = blk * sub, (blk + 1) * sub
            ti = lax.broadcasted_iota(jnp.int32, (sub, hi), 0)
            si = lax.broadcasted_iota(jnp.int32, (sub, hi), 1)
            pieces.append(_dot(jnp.where(si <= ti + lo, a, 0.0).astype(BF16), v[:hi]))
        o = o_inter[c, h] + jnp.concatenate(pieces, axis=0)
        gate = g_ref[rs, ls]
        o_ref[rs, ls] = (_rms(o, gn_ref[...]) * (gate * _sigmoid(gate))).astype(o_ref.dtype)


def _hgrn(p, lb_logits, gn, batch, seq, n_chunks):
    t = p.shape[0]
    rows = n_chunks * HGRN_CHUNK
    steps = seq // rows

    def col(cb):
        return pl.BlockSpec((rows, HGRN_W), lambda i, j: (i * steps + j, cb))

    return pl.pallas_call(
        functools.partial(_hgrn_kernel, n_chunks=n_chunks),
        out_shape=jax.ShapeDtypeStruct((t, HGRN_W), BF16),
        grid=(batch, steps),
        in_specs=[
            col(COL_HQ // HGRN_W), col(COL_HF // HGRN_W), col(COL_HI // HGRN_W), col(COL_HG // HGRN_W),
            pl.BlockSpec(lb_logits.shape, lambda i, j: (0, 0)),
            pl.BlockSpec((1, HGRN_DV), lambda i, j: (0, 0)),
        ],
        out_specs=pl.BlockSpec((rows, HGRN_W), lambda i, j: (i * steps + j, 0)),
        scratch_shapes=[pltpu.VMEM((HGRN_HEADS, HGRN_DV, HGRN_DK), F32)],
        compiler_params=_params(("arbitrary", "arbitrary")),
        name="hgrn",
    )(p, p, p, p, lb_logits, gn)


def _merge_kernel(x_ref, ga_ref, gb_ref, yn_ref, yh_ref, wn_ref, wh_ref, wo_ref, o_ref):
    tm = x_ref.shape[0]
    halves = [slice(0, tm // 2), slice(tm // 2, tm)]
    ups = [(_dot(yn_ref[r, :], wn_ref[...]), _dot(yh_ref[r, :], wh_ref[...])) for r in halves]
    for r, (up_nsa, up_hgrn) in zip(halves, ups):
        mixed = _sigmoid(ga_ref[r, :]) * up_nsa + _sigmoid(gb_ref[r, :]) * up_hgrn
        o_ref[r, :] = x_ref[r, :] + _dot(mixed.astype(BF16), wo_ref[...])


def _merge(x, p, y_nsa, y_hgrn, wn, wh, wo, tm):
    t, d = x.shape
    full = lambda a: pl.BlockSpec(a.shape, lambda i: (0, 0))
    return pl.pallas_call(
        _merge_kernel,
        out_shape=jax.ShapeDtypeStruct((t, d), F32),
        grid=(t // tm,),
        in_specs=[
            pl.BlockSpec((tm, d), lambda i: (i, 0)),
            pl.BlockSpec((tm, d), lambda i: (i, COL_GA // d)),
            pl.BlockSpec((tm, d), lambda i: (i, COL_GB // d)),
            pl.BlockSpec((tm, NSA_Q_W), lambda i: (i, 0)),
            pl.BlockSpec((tm, HGRN_W), lambda i: (i, 0)),
            full(wn), full(wh), full(wo),
        ],
        out_specs=pl.BlockSpec((tm, d), lambda i: (i, 0)),
        compiler_params=_params(("arbitrary",)),
        name="merge",
    )(x, p, p, y_nsa, y_hgrn, wn, wh, wo)


def _xattn_kernel(x_ref, g_ref, wq_ref, kv_ref, wo_ref, o_ref):
    x = x_ref[...]
    xq = _dot(_rms(x, g_ref[...]).astype(BF16), wq_ref[...]).astype(BF16)
    heads = [slice(h * XA_HEAD_DIM, (h + 1) * XA_HEAD_DIM) for h in range(XA_HEADS)]
    scores = [_dot_nt(xq[:, ls], kv_ref[0, :, ls]) * (XA_HEAD_DIM ** -0.5) for ls in heads]
    outs = []
    for h, s in enumerate(scores):
        e = jnp.exp(s - jnp.max(s, axis=-1, keepdims=True))
        p = e / jnp.sum(e, axis=-1, keepdims=True)
        outs.append(_dot(p.astype(BF16), kv_ref[0, :, XA_W + h * XA_HEAD_DIM:XA_W + (h + 1) * XA_HEAD_DIM]))
    o_x = jnp.concatenate(outs, axis=-1)
    o_ref[...] = x + _dot(o_x.astype(BF16), wo_ref[...])


def _xattn(x, g, wq, kv, wo, seq, tm):
    t, d = x.shape
    steps = seq // tm
    full = lambda a: pl.BlockSpec(a.shape, lambda i: (0, 0))
    return pl.pallas_call(
        _xattn_kernel,
        out_shape=jax.ShapeDtypeStruct((t, d), F32),
        grid=(t // tm,),
        in_specs=[
            pl.BlockSpec((tm, d), lambda i: (i, 0)),
            full(g), full(wq),
            pl.BlockSpec((1,) + kv.shape[1:], lambda i: (i // steps, 0, 0)),
            full(wo),
        ],
        out_specs=pl.BlockSpec((tm, d), lambda i: (i, 0)),
        compiler_params=_params(("arbitrary",)),
        name="xattn",
    )(x, g, wq, kv, wo)


def _router_kernel(x_ref, g_ref, w_ref, b_ref, hm_ref, rown_ref, rowt_ref, wtt_ref, cnt_ref):
    tm = x_ref.shape[0]
    hm = _rms(x_ref[...], g_ref[...]).astype(BF16)
    hm_ref[...] = hm
    lane = lax.broadcasted_iota(jnp.int32, (tm, LANES), 1)
    lane_f = lane.astype(F32)
    logits = _dot(hm, w_ref[...]) + b_ref[...]
    logits = jnp.where(lane < N_EXPERTS, logits, -jnp.inf)
    picks, vals = [], []
    onehot_all = jnp.zeros((tm, LANES), F32)
    for _ in range(TOP_K):
        mx = jnp.max(logits, axis=-1, keepdims=True)
        first_idx = jnp.min(jnp.where(logits == mx, lane_f, float(LANES)), axis=-1, keepdims=True)
        hit = lane_f == first_idx
        onehot = jnp.where(hit, 1.0, 0.0)
        logits = jnp.where(hit, -jnp.inf, logits)
        picks.append(onehot)
        vals.append(mx)
        onehot_all = onehot_all + onehot
    exps = [jnp.exp(v - vals[0]) for v in vals]
    den = exps[0]
    for e in exps[1:]:
        den = den + e
    r_i = lax.broadcasted_iota(jnp.int32, (tm, tm), 0)
    c_i = lax.broadcasted_iota(jnp.int32, (tm, tm), 1)
    lower = jnp.where(c_i < r_i, 1.0, 0.0).astype(BF16)
    before = _dot(lower, onehot_all.astype(BF16))
    counts = jnp.sum(onehot_all, axis=0, keepdims=True)
    cnt_ref[0] = counts
    seg_rows = jnp.floor((counts + (SEG_ALIGN - 1)) * (1.0 / SEG_ALIGN)) * SEG_ALIGN
    e_r = lax.broadcasted_iota(jnp.int32, (LANES, LANES), 0)
    e_c = lax.broadcasted_iota(jnp.int32, (LANES, LANES), 1)
    earlier = jnp.where(e_r < e_c, 1.0, 0.0).astype(BF16)
    seg_start = _dot(jnp.broadcast_to(seg_rows, (SEG_ALIGN, LANES)).astype(BF16), earlier)[0:1]
    row_out = jnp.zeros((tm, LANES), F32)
    wt_out = jnp.zeros((tm, LANES), F32)
    for k in range(TOP_K):
        row_k = jnp.sum(picks[k] * (before + seg_start), axis=-1, keepdims=True)
        row_out = jnp.where(lane == k, row_k, row_out)
        wt_out = jnp.where(lane == k, exps[k] / den, wt_out)
    rown_ref[...] = row_out.astype(jnp.int32)
    rowt_ref[0] = row_out.T[:SEG_ALIGN].astype(jnp.int32)
    wtt_ref[0] = wt_out.T[:SEG_ALIGN]


def _router(x, g, w, b, tm):
    t, d = x.shape
    full = lambda a: pl.BlockSpec(a.shape, lambda i: (0, 0))
    lane_out = pl.BlockSpec((tm, LANES), lambda i: (i, 0))
    per_tile = pl.BlockSpec((1, SEG_ALIGN, tm), lambda i: (i, 0, 0))
    return pl.pallas_call(
        _router_kernel,
        out_shape=(
            jax.ShapeDtypeStruct((t, d), BF16),
            jax.ShapeDtypeStruct((t, LANES), jnp.int32),
            jax.ShapeDtypeStruct((t // tm, SEG_ALIGN, tm), jnp.int32),
            jax.ShapeDtypeStruct((t // tm, SEG_ALIGN, tm), F32),
            jax.ShapeDtypeStruct((t // tm, 1, LANES), F32),
        ),
        grid=(t // tm,),
        in_specs=[pl.BlockSpec((tm, d), lambda i: (i, 0)), full(g), full(w), full(b)],
        out_specs=(pl.BlockSpec((tm, d), lambda i: (i, 0)), lane_out, per_tile, per_tile,
                   pl.BlockSpec((1, 1, LANES), lambda i: (i, 0, 0))),
        compiler_params=_params(("arbitrary",)),
        name="router",
    )(x, g, w, b)


def _slots_by_row(row_t, values_t, rows):
    r = lax.broadcasted_iota(jnp.int32, (rows, row_t.shape[1]), 0)
    onehot = jnp.zeros(r.shape, F32)
    weighted = jnp.zeros(r.shape, F32)
    for k in range(TOP_K):
        hit = r == row_t[k:k + 1]
        onehot = jnp.where(hit, 1.0, onehot)
        weighted = jnp.where(hit, values_t[k:k + 1], weighted)
    return onehot, weighted


def _slots_by_token(row_n, rows):
    r = lax.broadcasted_iota(jnp.int32, (row_n.shape[0], rows), 1)
    onehot = jnp.zeros(r.shape, F32)
    for k in range(TOP_K):
        onehot = jnp.where(r == row_n[:, k:k + 1], 1.0, onehot)
    return onehot


def _segment_copies(src_ref, n8_ref, dst_ref, make_copy, tile_tokens, table_row=None):
    base = (pl.program_id(0) if table_row is None else table_row) * N_EXPERTS
    sizes = []
    size = tile_tokens
    while size >= SEG_ALIGN:
        sizes.append(size)
        size //= 2

    def visit(e, start):
        n8 = n8_ref[base + e]
        src = src_ref[base + e]
        dst = dst_ref[base + e]
        for size in sizes:
            done = n8 & (-2 * size)

            @pl.when((n8 & size) != 0)
            def _():
                cp = make_copy(pl.multiple_of(src + done, SEG_ALIGN), pl.multiple_of(dst + done, SEG_ALIGN), size)
                if start:
                    cp.start()
                else:
                    cp.wait()

    def start_all(e, c):
        visit(e, True)
        return c

    def wait_all(e, c):
        visit(e, False)
        return c

    return start_all, wait_all


def _dispatch_kernel(src_ref, n8_ref, dst_ref, nu_ref, hm_ref, rowt_ref, wtt_ref, xs_ref, buf_ref, sem,
                     *, tm, n_steps):
    rows = buf_ref.shape[1]
    d = hm_ref.shape[1]
    tt = hm_ref.shape[0]
    n_tiles = xs_ref.shape[0] // tm
    step = pl.program_id(0)
    slot = step % 2

    def copies(table_row, buf_slot, tile_tokens=tt):
        def make_copy(src, dst, size):
            return pltpu.make_async_copy(buf_ref.at[buf_slot, pl.ds(src, size)], xs_ref.at[pl.ds(dst, size)],
                                         sem.at[buf_slot])
        return _segment_copies(src_ref, n8_ref, dst_ref, make_copy, tile_tokens, table_row) + (make_copy,)

    @pl.when(step >= 2)
    def _():
        lax.fori_loop(0, N_EXPERTS, copies(step - 2, slot)[1], 0)

    onehot, weighted = _slots_by_row(rowt_ref[0], wtt_ref[0], rows)
    buf_ref[slot, :, :d] = _dot(onehot.astype(BF16), hm_ref[...])
    w_hi = weighted.astype(BF16)
    w_lo = (weighted - w_hi.astype(F32)).astype(BF16)
    ones = jnp.ones((tt, LANES), BF16)
    buf_ref[slot, :, d:] = _dot(w_hi, ones) + _dot(w_lo, ones)
    start_cur, wait_cur, make_copy = copies(step, slot)
    lax.fori_loop(0, N_EXPERTS, start_cur, 0)

    @pl.when(step == n_steps - 1)
    def _():
        if n_steps >= 2:
            lax.fori_loop(0, N_EXPERTS, copies(step - 1, 1 - slot)[1], 0)
        lax.fori_loop(0, N_EXPERTS, wait_cur, 0)
        buf_ref[slot, :tm] = jnp.zeros((tm, buf_ref.shape[2]), F32)
        start_tail, wait_tail, _ = copies(n_steps, slot, tm)
        lax.fori_loop(0, N_EXPERTS, start_tail, 0)

        def zero_tile(i):
            return make_copy(0, pl.multiple_of(i * tm, tm), tm)

        lax.fori_loop(nu_ref[0], n_tiles, lambda i, c: (zero_tile(i).start(), c)[1], 0)
        lax.fori_loop(0, N_EXPERTS, wait_tail, 0)
        lax.fori_loop(nu_ref[0], n_tiles, lambda i, c: (zero_tile(i).wait(), c)[1], 0)


def _dispatch(tables, n_used, hm, row_t, wt_t, n_pad, tt, rows, tm):
    t, d = hm.shape
    per_tile = pl.BlockSpec((1,) + row_t.shape[1:], lambda i, *_: (i, 0, 0))
    grid_spec = pltpu.PrefetchScalarGridSpec(
        num_scalar_prefetch=4,
        grid=(t // tt,),
        in_specs=[pl.BlockSpec((tt, d), lambda i, *_: (i, 0)), per_tile, per_tile],
        out_specs=pl.BlockSpec(memory_space=pl.ANY),
        scratch_shapes=[pltpu.VMEM((2, rows, d + LANES), F32), pltpu.SemaphoreType.DMA((2,))],
    )
    return pl.pallas_call(
        functools.partial(_dispatch_kernel, tm=tm, n_steps=t // tt),
        out_shape=jax.ShapeDtypeStruct((n_pad, d + LANES), F32),
        grid_spec=grid_spec,
        compiler_params=_params(("arbitrary",)),
        name="dispatch",
    )(*tables, n_used, hm, row_t, wt_t)


def _w1_prep_kernel(w_ref, o_ref):
    grp = 2 * LANES
    r_i = lax.broadcasted_iota(jnp.int32, (grp, grp), 0)
    c_i = lax.broadcasted_iota(jnp.int32, (grp, grp), 1)
    src_col = jnp.where(c_i < LANES, 2 * c_i, 2 * (c_i - LANES) + 1)
    perm = jnp.where(r_i == src_col, 1.0, 0.0).astype(BF16)
    for c in range(w_ref.shape[1] // grp):
        sl = slice(c * grp, (c + 1) * grp)
        o_ref[:, sl] = _dot(w_ref[:, sl].astype(BF16), perm).astype(BF16)


def _w1_prep(w, tm):
    r, n = w.shape
    return pl.pallas_call(
        _w1_prep_kernel,
        out_shape=jax.ShapeDtypeStruct((r, n), BF16),
        grid=(r // tm,),
        in_specs=[pl.BlockSpec((tm, n), lambda i: (i, 0))],
        out_specs=pl.BlockSpec((tm, n), lambda i: (i, 0)),
        compiler_params=_params(("arbitrary",)),
        name="w1_prep",
    )(w)


def _ffn_kernel(te_ref, nu_ref, x_ref, w1_ref, b1_ref, w2_ref, b2_ref, o_ref):
    del te_ref
    used = pl.program_id(0) < nu_ref[0]

    @pl.when(used)
    def _():
        d = o_ref.shape[1]
        u = _dot(x_ref[:, :d].astype(BF16), w1_ref[0]) + b1_ref[0]
        acts = []
        for c in range(u.shape[1] // (2 * LANES)):
            glu = jnp.minimum(u[:, 2 * c * LANES:(2 * c + 1) * LANES], SWIGLU_LIMIT)
            lin = jnp.clip(u[:, (2 * c + 1) * LANES:(2 * c + 2) * LANES], -SWIGLU_LIMIT, SWIGLU_LIMIT)
            acts.append((glu * _sigmoid(SWIGLU_ALPHA * glu) * (lin + 1.0)).astype(BF16))
        y = _dot(jnp.concatenate(acts, axis=-1), w2_ref[0]) + b2_ref[0]
        weight = x_ref[:, d:]
        o_ref[...] = jnp.concatenate([y[:, c * LANES:(c + 1) * LANES] * weight for c in range(d // LANES)], axis=1)

    @pl.when(jnp.logical_not(used))
    def _():
        o_ref[...] = jnp.zeros_like(o_ref)


def _ffn(tile_expert, n_used, xs, w1, b1, w2, b2, tm):
    n_pad = xs.shape[0]
    d = w1.shape[1]
    f2 = w1.shape[2]
    f = w2.shape[1]
    grid_spec = pltpu.PrefetchScalarGridSpec(
        num_scalar_prefetch=2,
        grid=(n_pad // tm,),
        in_specs=[
            pl.BlockSpec((tm, xs.shape[1]), lambda i, te, nu: (jnp.minimum(i, nu[0] - 1), 0)),
            pl.BlockSpec((1, d, f2), lambda i, te, nu: (te[i], 0, 0)),
            pl.BlockSpec((1, 1, f2), lambda i, te, nu: (te[i], 0, 0)),
            pl.BlockSpec((1, f, d), lambda i, te, nu: (te[i], 0, 0)),
            pl.BlockSpec((1, 1, d), lambda i, te, nu: (te[i], 0, 0)),
        ],
        out_specs=pl.BlockSpec((tm, d), lambda i, te, nu: (i, 0)),
    )
    return pl.pallas_call(
        _ffn_kernel,
        out_shape=jax.ShapeDtypeStruct((n_pad, d), F32),
        grid_spec=grid_spec,
        compiler_params=_params(("arbitrary",)),
        name="expert_ffn",
    )(tile_expert, n_used, xs, w1, b1, w2, b2)


def _combine_kernel(src_ref, n8_ref, dst_ref, ys_ref, rown_ref, x_ref, g_ref, o_ref, buf_ref, sem, *, n_steps):
    rows = buf_ref.shape[1]
    step = pl.program_id(0)
    slot = step % 2

    def copies(table_row, buf_slot):
        def make_copy(src, dst, size):
            return pltpu.make_async_copy(ys_ref.at[pl.ds(dst, size)], buf_ref.at[buf_slot, pl.ds(src, size)],
                                         sem.at[buf_slot])
        return _segment_copies(src_ref, n8_ref, dst_ref, make_copy, x_ref.shape[0], table_row)

    @pl.when(step == 0)
    def _():
        buf_ref[...] = jnp.zeros_like(buf_ref)
        lax.fori_loop(0, N_EXPERTS, copies(0, 0)[0], 0)

    @pl.when(step + 1 < n_steps)
    def _():
        lax.fori_loop(0, N_EXPERTS, copies(step + 1, 1 - slot)[0], 0)

    onehot = _slots_by_token(rown_ref[...], rows).astype(BF16)
    lax.fori_loop(0, N_EXPERTS, copies(step, slot)[1], 0)
    ys = buf_ref[slot]
    y_hi = ys.astype(BF16)
    y_lo = (ys - y_hi.astype(F32)).astype(BF16)
    y = _dot(onehot, y_hi) + _dot(onehot, y_lo)
    o_ref[...] = _rms(x_ref[...] + y, g_ref[...])


def _combine(tables, ys, row_n, x, g, tt, rows):
    t, d = x.shape
    tile = lambda w: pl.BlockSpec((tt, w), lambda i, *_: (i, 0))
    grid_spec = pltpu.PrefetchScalarGridSpec(
        num_scalar_prefetch=3,
        grid=(t // tt,),
        in_specs=[pl.BlockSpec(memory_space=pl.ANY), tile(LANES), tile(d),
                  pl.BlockSpec((1, d), lambda i, *_: (0, 0))],
        out_specs=tile(d),
        scratch_shapes=[pltpu.VMEM((2, rows, d), F32), pltpu.SemaphoreType.DMA((2,))],
    )
    return pl.pallas_call(
        functools.partial(_combine_kernel, n_steps=t // tt),
        out_shape=jax.ShapeDtypeStruct((t, d), F32),
        grid_spec=grid_spec,
        compiler_params=_params(("arbitrary",)),
        name="combine",
    )(*tables, ys, row_n, x, g)


def _tile_sizes(seq):
    return dict(
        tm_proj=1024, tn_proj=P_WIDTH // 4,
        tm_rope=512,
        tq=256, tk=512,
        hgrn_chunks=4,
        tm_merge=512, tm_xattn=512,
        tm_router=512,
        tm_w1_prep=512,
        tm_ffn=512,
    )


def _layer(x, mem, positions, ts, mix_norm_g, w_in, cmp_pe, cmp_w1, cmp_b1, cmp_w2, cmp_b2, lb_logits, hgrn_norm_g,
           w_up_nsa, w_up_hgrn, w_out, xa_norm_g, xa_mem_norm_g, w_xq, w_xkv, w_xo, moe_norm_g, router_w, router_b,
           moe_w1, moe_b1, moe_w2, moe_b2, out_norm_g):
    b, s, d = x.shape
    t = b * s
    g, hg, dh = NSA_KV_GROUPS, NSA_Q_PER_GROUP, NSA_HEAD_DIM
    x2 = x.reshape(t, d)
    row = lambda v: v.reshape(1, -1).astype(F32)

    splits = [0]
    for w in (d, d, NSA_Q_W) + (NSA_KV_W,) * 6 + (3 * NSA_HEADS,) + (HGRN_W,) * 4:
        splits.append(splits[-1] + w)
    seg = lambda i: w_in[:, splits[i]:splits[i + 1]]
    (ga, gb, nq, kc, vc, ks, vs, kw, vw, ng, hq, hf, hi, hgate) = [seg(i) for i in range(14)]
    pad = jnp.zeros((d, P_WIDTH - COL_NG - 3 * NSA_HEADS), w_in.dtype)
    w_p = jnp.concatenate([ga, gb, hq, hf, hi, hgate, nq, ks, kw, kc, vc, vs, vw, ng, pad], axis=1).astype(BF16)

    p = _norm_matmul(x2, row(mix_norm_g), w_p, F32, ts["tm_proj"], ts["tn_proj"], "in_proj")

    half = dh // 2
    inv_freq = ROPE_THETA ** (-jnp.arange(half, dtype=F32) / half)
    invf = jnp.tile(inv_freq, LANES // half).reshape(1, LANES)
    q_r, kk_r = _rope(p, positions.reshape(t, 1), invf, ts["tm_rope"])
    tq = ts["tq"]
    nq = s // tq
    qt = q_r.reshape(b, nq, tq, g, hg, dh).transpose(0, 3, 1, 5, 4, 2).reshape(b, g, nq, dh, hg * tq)
    kk = kk_r.reshape(b, s, 2 * g, dh).transpose(0, 2, 1, 3)
    k_slc, k_win = kk[:, :g], kk[:, g:]
    nb = s // SLC_BLOCK
    block_onehot = (jnp.arange(s)[:, None] // SLC_BLOCK == jnp.arange(nb)[None, :]).astype(BF16)
    ks_aug = jnp.concatenate([k_slc, jnp.broadcast_to(block_onehot, (b, g, s, nb))], axis=-1)

    def values_t(v):
        vt = v.transpose(0, 1, 3, 2).astype(BF16)
        ones = jnp.ones(vt.shape[:2] + (1, vt.shape[3]), BF16)
        zeros = jnp.zeros(vt.shape[:2] + (BF16_SUBLANES - 1, vt.shape[3]), BF16)
        return jnp.concatenate([vt, ones, zeros], axis=2)

    vvt = values_t(p[:, COL_VSVW:COL_VSVW + 2 * NSA_KV_W].reshape(b, s, 2 * g, dh).transpose(0, 2, 1, 3))
    vt_slc, vt_win = vvt[:, :g], vvt[:, g:]
    gates = p[:, COL_NG:COL_NG + 3 * NSA_HEADS].reshape(b, nq, tq, g, hg, 3).transpose(0, 3, 1, 5, 4, 2)
    gates = gates.reshape(b, g, nq, 3, hg * tq)

    nr = s // CMP_STRIDE
    kcvc = p[:, COL_KCVC:COL_KCVC + 2 * NSA_KV_W].reshape(b, s, 2, g, dh).transpose(2, 0, 3, 1, 4)
    r = kcvc.reshape(2, b, g, nr, CMP_STRIDE * dh)
    pe = cmp_pe.reshape(2, 2, 1, CMP_STRIDE * dh)
    zeros_w2 = jnp.zeros_like(cmp_w2)
    w2p = jnp.stack([jnp.concatenate([cmp_w2, zeros_w2], axis=-1),
                     jnp.concatenate([zeros_w2, cmp_w2], axis=-1)], axis=1).astype(BF16)
    b2t = jnp.tile(cmp_b2, (1, g)).reshape(2, 1, LANES)
    pos_cmp = positions[:, CMP_BLOCK - 1::CMP_STRIDE]
    pos_cmp = jnp.pad(pos_cmp, ((0, 0), (0, nr - pos_cmp.shape[1]))).reshape(b, nr, 1)
    cmp = _compress(r, pe, cmp_w1.astype(BF16), cmp_b1.reshape(2, 1, CMP_HIDDEN), w2p, b2t, pos_cmp, invf)
    cmp = cmp.reshape(2, b, nr, g, dh).transpose(0, 1, 3, 2, 4)

    y_nsa = _nsa(qt, cmp[0], values_t(cmp[1]), ks_aug, vt_slc, k_win, vt_win, gates, tq, ts["tk"])
    y_nsa = y_nsa.reshape(b, g, nq, dh, hg, tq).transpose(0, 2, 5, 1, 4, 3).reshape(t, NSA_Q_W)

    y_hgrn = _hgrn(p, lb_logits.astype(F32), row(hgrn_norm_g), b, s, ts["hgrn_chunks"])

    x2 = _merge(x2, p, y_nsa, y_hgrn, w_up_nsa.astype(BF16), w_up_hgrn.astype(BF16), w_out.astype(BF16),
                ts["tm_merge"])

    n_mem = mem.shape[1]
    kv = _norm_matmul(mem.reshape(b * n_mem, d), row(xa_mem_norm_g), w_xkv.astype(BF16), BF16,
                      n_mem, 2 * XA_W, "mem_kv").reshape(b, n_mem, 2 * XA_W)
    x2 = _xattn(x2, row(xa_norm_g), w_xq.astype(BF16), kv, w_xo.astype(BF16), s, ts["tm_xattn"])

    n_exp = router_w.shape[1]
    rw = jnp.pad(router_w, ((0, 0), (0, LANES - n_exp))).astype(BF16)
    rb = jnp.pad(router_b, (0, LANES - n_exp)).reshape(1, LANES).astype(F32)
    assert n_exp == N_EXPERTS
    tt = ts["tm_router"]
    nt = t // tt
    hm, row_n, row_t, wt_t, cnt = _router(x2, row(moe_norm_g), rw, rb, tt)
    tm = ts["tm_ffn"]
    n8 = (cnt[:, 0, :n_exp].astype(jnp.int32) + SEG_ALIGN - 1) // SEG_ALIGN * SEG_ALIGN
    src_off = jnp.cumsum(n8, axis=1) - n8
    region = jnp.sum(n8, axis=0)
    padded = (region + tm - 1) // tm * tm
    ends = jnp.cumsum(padded)
    dst_off = (ends - padded)[None, :] + jnp.cumsum(n8, axis=0) - n8
    tables = tuple(jnp.concatenate([a, tail[None, :]], axis=0).reshape(-1) for a, tail in
                   ((src_off, jnp.zeros_like(region)), (n8, padded - region), (dst_off, ends - padded + region)))
    rows = tt * TOP_K + n_exp * SEG_ALIGN
    n_pad = (t * TOP_K + nt * n_exp * SEG_ALIGN + n_exp * tm + tm - 1) // tm * tm
    n_tiles = n_pad // tm
    tile_ids = jnp.arange(n_tiles, dtype=jnp.int32)
    tile_expert = jnp.sum(((ends // tm)[None, :] <= tile_ids[:, None]).astype(jnp.int32), axis=1)
    tile_expert = jnp.minimum(tile_expert, n_exp - 1)
    n_used = (ends[-1] // tm).reshape(1).astype(jnp.int32)

    xs = _dispatch(tables, n_used, hm, row_t, wt_t, n_pad, tt, rows, tm)
    f = moe_w2.shape[1]
    w1p = _w1_prep(moe_w1.reshape(n_exp * d, 2 * f), ts["tm_w1_prep"]).reshape(n_exp, d, 2 * f)
    b1p = moe_b1.reshape(n_exp, f // LANES, LANES, 2).transpose(0, 1, 3, 2).reshape(n_exp, 1, 2 * f)
    ys = _ffn(tile_expert, n_used, xs, w1p, b1p, moe_w2.astype(BF16), moe_b2.reshape(n_exp, 1, d), tm)
    out = _combine(tables, ys, row_n, x2, row(out_norm_g), tt, rows)
    return out.reshape(b, s, d)


def kernel(x, mem, positions, mix_norm_g, w_in, cmp_pe, cmp_w1, cmp_b1, cmp_w2, cmp_b2, hgrn_lb_logits, hgrn_norm_g, w_up_nsa, w_up_hgrn, w_out, xa_norm_g, xa_mem_norm_g, w_xq, w_xkv, w_xo, moe_norm_g, router_w, router_b, moe_w1, moe_b1, moe_w2, moe_b2, final_norm_g):
    depth = w_in.shape[0]
    assert depth == 1, "single-layer block: the final norm is fused into the last layer's combine"
    ts = _tile_sizes(x.shape[1])
    l = 0
    return _layer(x, mem, positions, ts, mix_norm_g[l], w_in[l], cmp_pe[l], cmp_w1[l], cmp_b1[l], cmp_w2[l], cmp_b2[l],
                  hgrn_lb_logits, hgrn_norm_g[l], w_up_nsa[l], w_up_hgrn[l], w_out[l], xa_norm_g[l], xa_mem_norm_g[l],
                  w_xq[l], w_xkv[l], w_xo[l], moe_norm_g[l], router_w[l], router_b[l], moe_w1[l], moe_b1[l], moe_w2[l],
                  moe_b2[l], final_norm_g)
```

```python
import functools

import jax
import jax.numpy as jnp
from jax import lax
from jax.experimental import pallas as pl
from jax.experimental.pallas import tpu as pltpu

EPS = 1e-6
ROPE_THETA = 10000.0
NEG_INF = -1e30
FORCE_SCORE = 1e9

NSA_HEADS = 8
NSA_KV_GROUPS = 2
NSA_Q_PER_GROUP = NSA_HEADS // NSA_KV_GROUPS
NSA_HEAD_DIM = 64
CMP_BLOCK = 32
CMP_STRIDE = 16
CMP_HIDDEN = 256
SLC_BLOCK = 64
SLC_TOPK = 16
N_LOCAL_BLOCKS = 2
WINDOW = 512
NSA_Q_W = NSA_HEADS * NSA_HEAD_DIM
NSA_KV_W = NSA_KV_GROUPS * NSA_HEAD_DIM

HGRN_HEADS = 4
HGRN_DK = 128
HGRN_DV = 128
HGRN_CHUNK = 64
HGRN_SUB = 16
HGRN_W = HGRN_HEADS * HGRN_DK

XA_HEADS = 4
XA_HEAD_DIM = 128
XA_W = XA_HEADS * XA_HEAD_DIM

N_EXPERTS = 32
TOP_K = 4
SWIGLU_ALPHA = 1.702
SWIGLU_LIMIT = 7.0

LANES = 128
SEG_ALIGN = 8
BF16_SUBLANES = 16
LOG2E = 1.4426950408889634
VMEM_LIMIT = 48 * 1024 * 1024

COL_GA = 0
COL_GB = 1024
COL_HQ = 2048
COL_HF = 2560
COL_HI = 3072
COL_HG = 3584
COL_NQ = 4096
COL_KSKW = 4608
COL_KCVC = 4864
COL_VSVW = 5120
COL_NG = 5376
P_WIDTH = 5632

F32 = jnp.float32
BF16 = jnp.bfloat16


def _params(sem):
    return pltpu.CompilerParams(dimension_semantics=sem, vmem_limit_bytes=VMEM_LIMIT)


def _dot(a, b):
    return jnp.dot(a, b, preferred_element_type=F32)


def _dot_nt(a, b):
    return lax.dot_general(a, b, (((1,), (1,)), ((), ())), preferred_element_type=F32)


def _rms(xf, g):
    return xf * lax.rsqrt(jnp.mean(xf * xf, axis=-1, keepdims=True) + EPS) * g


def _sigmoid(x):
    return 1.0 / (1.0 + jnp.exp(-x))


def _norm_matmul_kernel(x_ref, g_ref, w_ref, o_ref, hn_ref):
    @pl.when(pl.program_id(1) == 0)
    def _():
        hn_ref[...] = _rms(x_ref[...], g_ref[...]).astype(BF16)

    o_ref[...] = _dot(hn_ref[...], w_ref[...]).astype(o_ref.dtype)


def _norm_matmul(x, g, w, out_dtype, tm, tn, name):
    t, d = x.shape
    n = w.shape[1]
    return pl.pallas_call(
        _norm_matmul_kernel,
        out_shape=jax.ShapeDtypeStruct((t, n), out_dtype),
        grid=(t // tm, n // tn),
        in_specs=[
            pl.BlockSpec((tm, d), lambda i, j: (i, 0)),
            pl.BlockSpec((1, d), lambda i, j: (0, 0)),
            pl.BlockSpec((d, tn), lambda i, j: (0, j)),
        ],
        out_specs=pl.BlockSpec((tm, tn), lambda i, j: (i, j)),
        scratch_shapes=[pltpu.VMEM((tm, d), BF16)],
        compiler_params=_params(("arbitrary", "arbitrary")),
        name=name,
    )(x, g, w)


def _rope_coeffs(pos_col, invf):
    ang = pos_col.astype(F32) * invf
    lane = lax.broadcasted_iota(jnp.int32, ang.shape, 1)
    first = (lane & (NSA_HEAD_DIM - 1)) < (NSA_HEAD_DIM // 2)
    c = jnp.cos(ang)
    s = jnp.sin(ang)
    return c, jnp.where(first, -s, s), first


def _rope_tile(x, c, s_signed, first):
    half = NSA_HEAD_DIM // 2
    partner = jnp.where(first, pltpu.roll(x, LANES - half, 1), pltpu.roll(x, half, 1))
    return x * c + partner * s_signed


def _rope_kernel(q_ref, k_ref, pos_ref, invf_ref, qo_ref, ko_ref, *, q_scale):
    c, s_signed, first = _rope_coeffs(pos_ref[...], invf_ref[...])
    for i in range(q_ref.shape[1] // LANES):
        sl = slice(i * LANES, (i + 1) * LANES)
        qo_ref[:, sl] = (_rope_tile(q_ref[:, sl], c, s_signed, first) * q_scale).astype(BF16)
    for i in range(k_ref.shape[1] // LANES):
        sl = slice(i * LANES, (i + 1) * LANES)
        ko_ref[:, sl] = _rope_tile(k_ref[:, sl], c, s_signed, first).astype(BF16)


def _rope(p, pos_col, invf, tm):
    t = p.shape[0]
    kw = 2 * NSA_KV_W
    return pl.pallas_call(
        functools.partial(_rope_kernel, q_scale=NSA_HEAD_DIM ** -0.5 * LOG2E),
        out_shape=(jax.ShapeDtypeStruct((t, NSA_Q_W), BF16), jax.ShapeDtypeStruct((t, kw), BF16)),
        grid=(t // tm,),
        in_specs=[
            pl.BlockSpec((tm, NSA_Q_W), lambda i: (i, COL_NQ // NSA_Q_W)),
            pl.BlockSpec((tm, kw), lambda i: (i, COL_KSKW // kw)),
            pl.BlockSpec((tm, 1), lambda i: (i, 0)),
            pl.BlockSpec((1, LANES), lambda i: (0, 0)),
        ],
        out_specs=(
            pl.BlockSpec((tm, NSA_Q_W), lambda i: (i, 0)),
            pl.BlockSpec((tm, kw), lambda i: (i, 0)),
        ),
        compiler_params=_params(("arbitrary",)),
        name="rope",
    )(p, p, pos_col, invf)


def _gelu_tanh(x):
    return 0.5 * x * (1.0 + jnp.tanh(0.7978845608028654 * (x + 0.044715 * (x * x * x))))


def _compress_kernel(r_ref, pe_ref, w1_ref, b1_ref, w2_ref, b2_ref, pos_ref, invf_ref, o_ref):
    nr = r_ref.shape[3]
    half = r_ref.shape[4]
    acc = None
    for g in range(NSA_KV_GROUPS):
        r = r_ref[0, 0, g]
        top = _dot((r + pe_ref[0, 0]).astype(BF16), w1_ref[0, :half, :])
        bot = _dot((r + pe_ref[0, 1]).astype(BF16), w1_ref[0, half:, :])
        pre = top + pltpu.roll(bot, nr - 1, 0) + b1_ref[0]
        part = _dot(_gelu_tanh(pre).astype(BF16), w2_ref[0, g])
        acc = part if acc is None else acc + part
    out = acc + b2_ref[0]
    c, s_signed, first = _rope_coeffs(pos_ref[0], invf_ref[...])
    roped = _rope_tile(out, c, s_signed, first)
    is_key = pl.program_id(0) == 0
    o_ref[0, 0] = jnp.where(is_key, roped, out).astype(BF16)


def _compress(r, pe, w1, b1, w2p, b2t, pos_cmp, invf):
    _, b, g, nr, half = r.shape
    return pl.pallas_call(
        _compress_kernel,
        out_shape=jax.ShapeDtypeStruct((2, b, nr, LANES), BF16),
        grid=(2, b),
        in_specs=[
            pl.BlockSpec((1, 1, g, nr, half), lambda k, i: (k, i, 0, 0, 0)),
            pl.BlockSpec((1, 2, 1, half), lambda k, i: (k, 0, 0, 0)),
            pl.BlockSpec((1, 2 * half, CMP_HIDDEN), lambda k, i: (k, 0, 0)),
            pl.BlockSpec((1, 1, CMP_HIDDEN), lambda k, i: (k, 0, 0)),
            pl.BlockSpec((1, g, CMP_HIDDEN, LANES), lambda k, i: (k, 0, 0, 0)),
            pl.BlockSpec((1, 1, LANES), lambda k, i: (k, 0, 0)),
            pl.BlockSpec((1, nr, 1), lambda k, i: (i, 0, 0)),
            pl.BlockSpec((1, LANES), lambda k, i: (0, 0)),
        ],
        out_specs=pl.BlockSpec((1, 1, nr, LANES), lambda k, i: (k, i, 0, 0)),
        compiler_params=_params(("arbitrary", "arbitrary")),
        name="compress",
    )(r, pe, w1, b1, w2p, b2t, pos_cmp, invf)


def _nsa_kernel(qt_ref, kc_ref, vct_ref, ksa_ref, vst_ref, kw_ref, vwt_ref, g_ref, o_ref, acc_ref, out_ref,
                sa_ref, sb_ref, qa_ref, *, tq, tk, seq):
    hg = NSA_Q_PER_GROUP
    dh = NSA_HEAD_DIM
    nc = kc_ref.shape[2]
    nb = seq // SLC_BLOCK
    top_k = min(SLC_TOPK, nb)
    s0 = pl.program_id(2) * tq
    t_lane = s0 + lax.broadcasted_iota(jnp.int32, (1, tq), 1)
    gate = _sigmoid(g_ref[0, 0, 0])

    def scores(k_tile, bias):
        s = _dot(k_tile, qt_ref[0, 0, 0])
        return jnp.concatenate([s[:, h * tq:(h + 1) * tq] + bias for h in range(hg)], axis=1)

    def normalised(acc):
        return acc[:dh] / acc[dh:dh + 1]

    n_col = lax.broadcasted_iota(jnp.int32, (nc, 1), 0)
    valid_c = (n_col * CMP_STRIDE + (CMP_BLOCK - 1) <= t_lane) & (n_col < nc - 1)
    win_keys = WINDOW + tq
    w0 = pl.multiple_of(jnp.maximum(s0 - WINDOW, 0), tq)
    wpos = w0 + lax.broadcasted_iota(jnp.int32, (win_keys, 1), 0)
    bias_w = jnp.where((wpos <= t_lane) & (wpos > t_lane - WINDOW), 0.0, NEG_INF)
    s_c = scores(kc_ref[0, 0], jnp.where(valid_c, 0.0, NEG_INF))
    s_w = scores(kw_ref[0, 0, pl.ds(w0, win_keys), :], bias_w)

    e_c = jnp.exp2(s_c - jnp.max(s_c, axis=0, keepdims=True))
    t_all = s0 + (lax.broadcasted_iota(jnp.int32, (1, hg * tq), 1) & (tq - 1))
    row_ok = t_all >= CMP_BLOCK - 1
    pn = e_c * jnp.where(row_ok, 1.0 / jnp.sum(e_c, axis=0, keepdims=True), 0.0)
    out_ref[...] = gate[0:1] * _dot(vct_ref[0, 0], pn.astype(BF16))[:dh]
    p_sum = pn[:, 0:tq]
    for h in range(1, hg):
        p_sum = p_sum + pn[:, h * tq:(h + 1) * tq]

    j_col = lax.broadcasted_iota(jnp.int32, (nb, 1), 0)
    n_row = lax.broadcasted_iota(jnp.int32, (1, nc), 1) * CMP_STRIDE
    overlap = (n_row < j_col * SLC_BLOCK + SLC_BLOCK) & (n_row + CMP_BLOCK > j_col * SLC_BLOCK)
    overlap = jnp.where(overlap, 1.0, 0.0).astype(BF16)
    p_hi = p_sum.astype(BF16)
    p_lo = (p_sum - p_hi.astype(F32)).astype(BF16)
    imp = _dot(overlap, p_hi) + _dot(overlap, p_lo)

    p_w = jnp.exp2(s_w - jnp.max(s_w, axis=0, keepdims=True)).astype(BF16)
    out_ref[...] += gate[2:3] * normalised(_dot(vwt_ref[0, 0, :, pl.ds(w0, win_keys)], p_w))

    cur = t_lane >> 6
    causal_b = j_col <= cur
    forced = (j_col == 0) | (causal_b & (j_col > cur - N_LOCAL_BLOCKS))
    score = jnp.where(forced, FORCE_SCORE, jnp.where(causal_b, imp, -1.0))
    j_f = jnp.broadcast_to(j_col.astype(F32), (nb, tq))
    for _ in range(top_k):
        mx = jnp.max(score, axis=0, keepdims=True)
        first_idx = jnp.min(jnp.where(score == mx, j_f, float(nb)), axis=0, keepdims=True)
        score = jnp.where(j_f == first_idx, -jnp.inf, score)
    sel = jnp.where(causal_b & (score == -jnp.inf), 1.0, 0.0)

    sel_bias = ((sel - 1.0) * (-NEG_INF)).astype(BF16)
    qa_ref[:dh] = qt_ref[0, 0, 0]
    qa_ref[dh:] = jnp.concatenate([sel_bias] * hg, axis=1)
    k_col = lax.broadcasted_iota(jnp.int32, (tk, 1), 0)

    def put_scores(k0, buf_ref):
        buf_ref[...] = _dot(ksa_ref[0, 0, pl.ds(pl.multiple_of(k0, tk), tk), :], qa_ref[...])

    def consume(k0, buf_ref, m_old, causal):
        s = buf_ref[...]
        if causal:
            bias = jnp.where(k0 + k_col <= t_lane, 0.0, NEG_INF)
            s = jnp.concatenate([s[:, h * tq:(h + 1) * tq] + bias for h in range(hg)], axis=1)
        m_new = jnp.maximum(m_old, jnp.max(s, axis=0, keepdims=True))
        pv = _dot(vst_ref[0, 0, :, pl.ds(pl.multiple_of(k0, tk), tk)], jnp.exp2(s - m_new).astype(BF16))
        acc_ref[...] = jnp.exp2(m_old - m_new) * acc_ref[...] + pv
        return m_new

    def slc_pair(it, m):
        k0 = it * (2 * tk)
        put_scores(k0 + tk, sb_ref)
        m = consume(k0, sa_ref, m, False)
        put_scores(k0 + 2 * tk, sa_ref)
        return consume(k0 + tk, sb_ref, m, False)

    acc_ref[...] = jnp.zeros_like(acc_ref)
    put_scores(jnp.int32(0), sa_ref)
    d_tile = s0 // tk
    n_full = d_tile // 2
    m_s = lax.fori_loop(0, n_full, slc_pair, jnp.full((1, hg * tq), NEG_INF, F32))
    e0 = n_full * (2 * tk)

    @pl.when(d_tile % 2 == 1)
    def _():
        put_scores(e0 + tk, sb_ref)
        consume(e0 + tk, sb_ref, consume(e0, sa_ref, m_s, False), True)

    @pl.when(d_tile % 2 == 0)
    def _():
        consume(e0, sa_ref, m_s, True)

    o_ref[0, 0, 0] = (out_ref[...] + gate[1:2] * normalised(acc_ref[...])).astype(o_ref.dtype)


def _nsa(qt, kc, vct, ksa, vst, kw, vwt, gates, tq, tk):
    b, g, nq, dh, lanes = qt.shape
    hg = lanes // tq
    s = nq * tq
    nr = kc.shape[2]
    vr = vst.shape[2]
    k_spec = pl.BlockSpec((1, 1, s, dh), lambda i, j, k: (i, j, 0, 0))
    vt_spec = pl.BlockSpec((1, 1, vr, s), lambda i, j, k: (i, j, 0, 0))
    return pl.pallas_call(
        functools.partial(_nsa_kernel, tq=tq, tk=tk, seq=s),
        out_shape=jax.ShapeDtypeStruct((b, g, nq, dh, hg * tq), BF16),
        grid=(b, g, nq),
        in_specs=[
            pl.BlockSpec((1, 1, 1, dh, hg * tq), lambda i, j, k: (i, j, k, 0, 0)),
            pl.BlockSpec((1, 1, nr, dh), lambda i, j, k: (i, j, 0, 0)),
            pl.BlockSpec((1, 1, vr, nr), lambda i, j, k: (i, j, 0, 0)),
            pl.BlockSpec((1, 1, s, ksa.shape[3]), lambda i, j, k: (i, j, 0, 0)), vt_spec, k_spec, vt_spec,
            pl.BlockSpec((1, 1, 1, 3, hg * tq), lambda i, j, k: (i, j, k, 0, 0)),
        ],
        out_specs=pl.BlockSpec((1, 1, 1, dh, hg * tq), lambda i, j, k: (i, j, k, 0, 0)),
        scratch_shapes=[pltpu.VMEM((vr, hg * tq), F32), pltpu.VMEM((dh, hg * tq), F32),
                        pltpu.VMEM((tk, hg * tq), F32), pltpu.VMEM((tk, hg * tq), F32),
                        pltpu.VMEM((ksa.shape[3], hg * tq), BF16)],
        compiler_params=_params(("arbitrary", "arbitrary", "arbitrary")),
        name="nsa",
    )(qt, kc, vct, ksa, vst, kw, vwt, gates)


def _cumsum_rows(x):
    n = x.shape[0]
    row = lax.broadcasted_iota(jnp.int32, x.shape, 0)
    d = 1
    while d < n:
        x = x + jnp.where(row >= d, pltpu.roll(x, d, 0), 0.0)
        d *= 2
    return x


def _hgrn_kernel(q_ref, f_ref, i_ref, g_ref, lb_ref, gn_ref, o_ref, st_ref, *, n_chunks):
    @pl.when(pl.program_id(1) == 0)
    def _():
        st_ref[...] = jnp.zeros_like(st_ref)

    c_len = HGRN_CHUNK
    sub = HGRN_SUB
    lbl = lb_ref[...]
    lb_e = jnp.exp(lbl - jnp.max(lbl, axis=0, keepdims=True))
    lb_all = lb_e[0:1] / jnp.sum(lb_e, axis=0, keepdims=True)

    items = [(c, h) for c in range(n_chunks) for h in range(HGRN_HEADS)]
    wave1 = {}
    for c, h in items:
        rs = slice(c * c_len, (c + 1) * c_len)
        ls = slice(h * HGRN_DK, (h + 1) * HGRN_DK)
        lb = lb_all[:, ls]
        f = lb + (1.0 - lb) * _sigmoid(f_ref[rs, ls])
        k = 1.0 - f
        b = _cumsum_rows(jnp.log(f))
        q = q_ref[rs, ls]
        v32 = i_ref[rs, ls]
        b_end = b[c_len - 1:c_len]
        attn = []
        for blk in range(c_len // sub):
            lo, hi = blk * sub, (blk + 1) * sub
            mid = lo + sub // 2
            beta = b[mid - 1:mid]
            qd = (q[lo:hi] * jnp.exp(b[lo:hi] - beta)).astype(BF16)
            kd = (k[:hi] * jnp.exp(beta - b[:hi])).astype(BF16)
            attn.append(_dot_nt(qd, kd))
        update = _dot(v32.T.astype(BF16), (k * jnp.exp(b_end - b)).astype(BF16))
        wave1[c, h] = ((q * jnp.exp(b)).astype(BF16), jnp.exp(b_end), update, attn)

    o_inter = {}
    for h in range(HGRN_HEADS):
        st = st_ref[h]
        for c in range(n_chunks):
            q_dec, decay, update, _ = wave1[c, h]
            o_inter[c, h] = _dot_nt(q_dec, st.astype(BF16))
            st = st * decay + update
        st_ref[h] = st

    for c, h in items:
        rs = slice(c * c_len, (c + 1) * c_len)
        ls = slice(h * HGRN_DK, (h + 1) * HGRN_DK)
        v = i_ref[rs, ls].astype(BF16)
        pieces = []
        for blk, a in enumerate(wave1[c, h][3]):
            lo, hi = blk * sub, (blk + 1) * sub
            ti = lax.broadcasted_iota(jnp.int32, (sub, hi), 0)
            si = lax.broadcasted_iota(jnp.int32, (sub, hi), 1)
            pieces.append(_dot(jnp.where(si <= ti + lo, a, 0.0).astype(BF16), v[:hi]))
        o = o_inter[c, h] + jnp.concatenate(pieces, axis=0)
        gate = g_ref[rs, ls]
        o_ref[rs, ls] = (_rms(o, gn_ref[...]) * (gate * _sigmoid(gate))).astype(o_ref.dtype)


def _hgrn(p, lb_logits, gn, batch, seq, n_chunks):
    t = p.shape[0]
    rows = n_chunks * HGRN_CHUNK
    steps = seq // rows

    def col(cb):
        return pl.BlockSpec((rows, HGRN_W), lambda i, j: (i * steps + j, cb))

    return pl.pallas_call(
        functools.partial(_hgrn_kernel, n_chunks=n_chunks),
        out_shape=jax.ShapeDtypeStruct((t, HGRN_W), BF16),
        grid=(batch, steps),
        in_specs=[
            col(COL_HQ // HGRN_W), col(COL_HF // HGRN_W), col(COL_HI // HGRN_W), col(COL_HG // HGRN_W),
            pl.BlockSpec(lb_logits.shape, lambda i, j: (0, 0)),
            pl.BlockSpec((1, HGRN_DV), lambda i, j: (0, 0)),
        ],
        out_specs=pl.BlockSpec((rows, HGRN_W), lambda i, j: (i * steps + j, 0)),
        scratch_shapes=[pltpu.VMEM((HGRN_HEADS, HGRN_DV, HGRN_DK), F32)],
        compiler_params=_params(("arbitrary", "arbitrary")),
        name="hgrn",
    )(p, p, p, p, lb_logits, gn)


def _merge_kernel(x_ref, ga_ref, gb_ref, yn_ref, yh_ref, wn_ref, wh_ref, wo_ref, o_ref):
    mixed = _sigmoid(ga_ref[...]) * _dot(yn_ref[...], wn_ref[...]) + _sigmoid(gb_ref[...]) * _dot(yh_ref[...], wh_ref[...])
    o_ref[...] = x_ref[...] + _dot(mixed.astype(BF16), wo_ref[...])


def _merge(x, p, y_nsa, y_hgrn, wn, wh, wo, tm):
    t, d = x.shape
    full = lambda a: pl.BlockSpec(a.shape, lambda i: (0, 0))
    return pl.pallas_call(
        _merge_kernel,
        out_shape=jax.ShapeDtypeStruct((t, d), F32),
        grid=(t // tm,),
        in_specs=[
            pl.BlockSpec((tm, d), lambda i: (i, 0)),
            pl.BlockSpec((tm, d), lambda i: (i, COL_GA // d)),
            pl.BlockSpec((tm, d), lambda i: (i, COL_GB // d)),
            pl.BlockSpec((tm, NSA_Q_W), lambda i: (i, 0)),
            pl.BlockSpec((tm, HGRN_W), lambda i: (i, 0)),
            full(wn), full(wh), full(wo),
        ],
        out_specs=pl.BlockSpec((tm, d), lambda i: (i, 0)),
        compiler_params=_params(("arbitrary",)),
        name="merge",
    )(x, p, p, y_nsa, y_hgrn, wn, wh, wo)


def _xattn_kernel(x_ref, g_ref, wq_ref, kv_ref, wo_ref, o_ref):
    x = x_ref[...]
    xq = _dot(_rms(x, g_ref[...]).astype(BF16), wq_ref[...]).astype(BF16)
    heads = [slice(h * XA_HEAD_DIM, (h + 1) * XA_HEAD_DIM) for h in range(XA_HEADS)]
    scores = [_dot_nt(xq[:, ls], kv_ref[0, :, ls]) * (XA_HEAD_DIM ** -0.5) for ls in heads]
    outs = []
    for h, s in enumerate(scores):
        e = jnp.exp(s - jnp.max(s, axis=-1, keepdims=True))
        p = e / jnp.sum(e, axis=-1, keepdims=True)
        outs.append(_dot(p.astype(BF16), kv_ref[0, :, XA_W + h * XA_HEAD_DIM:XA_W + (h + 1) * XA_HEAD_DIM]))
    o_x = jnp.concatenate(outs, axis=-1)
    o_ref[...] = x + _dot(o_x.astype(BF16), wo_ref[...])


def _xattn(x, g, wq, kv, wo, seq, tm):
    t, d = x.shape
    steps = seq // tm
    full = lambda a: pl.BlockSpec(a.shape, lambda i: (0, 0))
    return pl.pallas_call(
        _xattn_kernel,
        out_shape=jax.ShapeDtypeStruct((t, d), F32),
        grid=(t // tm,),
        in_specs=[
            pl.BlockSpec((tm, d), lambda i: (i, 0)),
            full(g), full(wq),
            pl.BlockSpec((1,) + kv.shape[1:], lambda i: (i // steps, 0, 0)),
            full(wo),
        ],
        out_specs=pl.BlockSpec((tm, d), lambda i: (i, 0)),
        compiler_params=_params(("arbitrary",)),
        name="xattn",
    )(x, g, wq, kv, wo)


def _router_kernel(x_ref, g_ref, w_ref, b_ref, hm_ref, rown_ref, rowt_ref, wtt_ref, cnt_ref):
    tm = x_ref.shape[0]
    hm = _rms(x_ref[...], g_ref[...]).astype(BF16)
    hm_ref[...] = hm
    lane = lax.broadcasted_iota(jnp.int32, (tm, LANES), 1)
    lane_f = lane.astype(F32)
    logits = _dot(hm, w_ref[...]) + b_ref[...]
    logits = jnp.where(lane < N_EXPERTS, logits, -jnp.inf)
    picks, vals = [], []
    onehot_all = jnp.zeros((tm, LANES), F32)
    for _ in range(TOP_K):
        mx = jnp.max(logits, axis=-1, keepdims=True)
        first_idx = jnp.min(jnp.where(logits == mx, lane_f, float(LANES)), axis=-1, keepdims=True)
        hit = lane_f == first_idx
        onehot = jnp.where(hit, 1.0, 0.0)
        logits = jnp.where(hit, -jnp.inf, logits)
        picks.append(onehot)
        vals.append(mx)
        onehot_all = onehot_all + onehot
    exps = [jnp.exp(v - vals[0]) for v in vals]
    den = exps[0]
    for e in exps[1:]:
        den = den + e
    r_i = lax.broadcasted_iota(jnp.int32, (tm, tm), 0)
    c_i = lax.broadcasted_iota(jnp.int32, (tm, tm), 1)
    lower = jnp.where(c_i < r_i, 1.0, 0.0).astype(BF16)
    before = _dot(lower, onehot_all.astype(BF16))
    counts = jnp.sum(onehot_all, axis=0, keepdims=True)
    cnt_ref[0] = counts
    seg_rows = jnp.floor((counts + (SEG_ALIGN - 1)) * (1.0 / SEG_ALIGN)) * SEG_ALIGN
    e_r = lax.broadcasted_iota(jnp.int32, (LANES, LANES), 0)
    e_c = lax.broadcasted_iota(jnp.int32, (LANES, LANES), 1)
    earlier = jnp.where(e_r < e_c, 1.0, 0.0).astype(BF16)
    seg_start = _dot(jnp.broadcast_to(seg_rows, (SEG_ALIGN, LANES)).astype(BF16), earlier)[0:1]
    row_out = jnp.zeros((tm, LANES), F32)
    wt_out = jnp.zeros((tm, LANES), F32)
    for k in range(TOP_K):
        row_k = jnp.sum(picks[k] * (before + seg_start), axis=-1, keepdims=True)
        row_out = jnp.where(lane == k, row_k, row_out)
        wt_out = jnp.where(lane == k, exps[k] / den, wt_out)
    rown_ref[...] = row_out.astype(jnp.int32)
    rowt_ref[0] = row_out.T[:SEG_ALIGN].astype(jnp.int32)
    wtt_ref[0] = wt_out.T[:SEG_ALIGN]


def _router(x, g, w, b, tm):
    t, d = x.shape
    full = lambda a: pl.BlockSpec(a.shape, lambda i: (0, 0))
    lane_out = pl.BlockSpec((tm, LANES), lambda i: (i, 0))
    per_tile = pl.BlockSpec((1, SEG_ALIGN, tm), lambda i: (i, 0, 0))
    return pl.pallas_call(
        _router_kernel,
        out_shape=(
            jax.ShapeDtypeStruct((t, d), BF16),
            jax.ShapeDtypeStruct((t, LANES), jnp.int32),
            jax.ShapeDtypeStruct((t // tm, SEG_ALIGN, tm), jnp.int32),
            jax.ShapeDtypeStruct((t // tm, SEG_ALIGN, tm), F32),
            jax.ShapeDtypeStruct((t // tm, 1, LANES), F32),
        ),
        grid=(t // tm,),
        in_specs=[pl.BlockSpec((tm, d), lambda i: (i, 0)), full(g), full(w), full(b)],
        out_specs=(pl.BlockSpec((tm, d), lambda i: (i, 0)), lane_out, per_tile, per_tile,
                   pl.BlockSpec((1, 1, LANES), lambda i: (i, 0, 0))),
        compiler_params=_params(("arbitrary",)),
        name="router",
    )(x, g, w, b)


def _slots_by_row(row_t, values_t, rows):
    r = lax.broadcasted_iota(jnp.int32, (rows, row_t.shape[1]), 0)
    onehot = jnp.zeros(r.shape, F32)
    weighted = jnp.zeros(r.shape, F32)
    for k in range(TOP_K):
        hit = r == row_t[k:k + 1]
        onehot = jnp.where(hit, 1.0, onehot)
        weighted = jnp.where(hit, values_t[k:k + 1], weighted)
    return onehot, weighted


def _slots_by_token(row_n, rows):
    r = lax.broadcasted_iota(jnp.int32, (row_n.shape[0], rows), 1)
    onehot = jnp.zeros(r.shape, F32)
    for k in range(TOP_K):
        onehot = jnp.where(r == row_n[:, k:k + 1], 1.0, onehot)
    return onehot


def _segment_copies(src_ref, n8_ref, dst_ref, make_copy, tile_tokens, table_row=None):
    base = (pl.program_id(0) if table_row is None else table_row) * N_EXPERTS
    sizes = []
    size = tile_tokens
    while size >= SEG_ALIGN:
        sizes.append(size)
        size //= 2

    def visit(e, start):
        n8 = n8_ref[base + e]
        src = src_ref[base + e]
        dst = dst_ref[base + e]
        for size in sizes:
            done = n8 & (-2 * size)

            @pl.when((n8 & size) != 0)
            def _():
                cp = make_copy(pl.multiple_of(src + done, SEG_ALIGN), pl.multiple_of(dst + done, SEG_ALIGN), size)
                if start:
                    cp.start()
                else:
                    cp.wait()

    def start_all(e, c):
        visit(e, True)
        return c

    def wait_all(e, c):
        visit(e, False)
        return c

    return start_all, wait_all


def _dispatch_kernel(src_ref, n8_ref, dst_ref, nu_ref, hm_ref, rowt_ref, wtt_ref, xs_ref, buf_ref, sem,
                     *, tm, n_steps):
    rows = buf_ref.shape[1]
    d = hm_ref.shape[1]
    tt = hm_ref.shape[0]
    n_tiles = xs_ref.shape[0] // tm
    step = pl.program_id(0)
    slot = step % 2

    def copies(table_row, buf_slot, tile_tokens=tt):
        def make_copy(src, dst, size):
            return pltpu.make_async_copy(buf_ref.at[buf_slot, pl.ds(src, size)], xs_ref.at[pl.ds(dst, size)],
                                         sem.at[buf_slot])
        return _segment_copies(src_ref, n8_ref, dst_ref, make_copy, tile_tokens, table_row) + (make_copy,)

    @pl.when(step >= 2)
    def _():
        lax.fori_loop(0, N_EXPERTS, copies(step - 2, slot)[1], 0)

    onehot, weighted = _slots_by_row(rowt_ref[0], wtt_ref[0], rows)
    buf_ref[slot, :, :d] = _dot(onehot.astype(BF16), hm_ref[...])
    w_hi = weighted.astype(BF16)
    w_lo = (weighted - w_hi.astype(F32)).astype(BF16)
    ones = jnp.ones((tt, LANES), BF16)
    buf_ref[slot, :, d:] = _dot(w_hi, ones) + _dot(w_lo, ones)
    start_cur, wait_cur, make_copy = copies(step, slot)
    lax.fori_loop(0, N_EXPERTS, start_cur, 0)

    @pl.when(step == n_steps - 1)
    def _():
        if n_steps >= 2:
            lax.fori_loop(0, N_EXPERTS, copies(step - 1, 1 - slot)[1], 0)
        lax.fori_loop(0, N_EXPERTS, wait_cur, 0)
        buf_ref[slot, :tm] = jnp.zeros((tm, buf_ref.shape[2]), F32)
        start_tail, wait_tail, _ = copies(n_steps, slot, tm)
        lax.fori_loop(0, N_EXPERTS, start_tail, 0)

        def zero_tile(i):
            return make_copy(0, pl.multiple_of(i * tm, tm), tm)

        lax.fori_loop(nu_ref[0], n_tiles, lambda i, c: (zero_tile(i).start(), c)[1], 0)
        lax.fori_loop(0, N_EXPERTS, wait_tail, 0)
        lax.fori_loop(nu_ref[0], n_tiles, lambda i, c: (zero_tile(i).wait(), c)[1], 0)


def _dispatch(tables, n_used, hm, row_t, wt_t, n_pad, tt, rows, tm):
    t, d = hm.shape
    per_tile = pl.BlockSpec((1,) + row_t.shape[1:], lambda i, *_: (i, 0, 0))
    grid_spec = pltpu.PrefetchScalarGridSpec(
        num_scalar_prefetch=4,
        grid=(t // tt,),
        in_specs=[pl.BlockSpec((tt, d), lambda i, *_: (i, 0)), per_tile, per_tile],
        out_specs=pl.BlockSpec(memory_space=pl.ANY),
        scratch_shapes=[pltpu.VMEM((2, rows, d + LANES), F32), pltpu.SemaphoreType.DMA((2,))],
    )
    return pl.pallas_call(
        functools.partial(_dispatch_kernel, tm=tm, n_steps=t // tt),
        out_shape=jax.ShapeDtypeStruct((n_pad, d + LANES), F32),
        grid_spec=grid_spec,
        compiler_params=_params(("arbitrary",)),
        name="dispatch",
    )(*tables, n_used, hm, row_t, wt_t)


def _w1_prep_kernel(w_ref, o_ref):
    grp = 2 * LANES
    r_i = lax.broadcasted_iota(jnp.int32, (grp, grp), 0)
    c_i = lax.broadcasted_iota(jnp.int32, (grp, grp), 1)
    src_col = jnp.where(c_i < LANES, 2 * c_i, 2 * (c_i - LANES) + 1)
    perm = jnp.where(r_i == src_col, 1.0, 0.0).astype(BF16)
    for c in range(w_ref.shape[1] // grp):
        sl = slice(c * grp, (c + 1) * grp)
        o_ref[:, sl] = _dot(w_ref[:, sl].astype(BF16), perm).astype(BF16)


def _w1_prep(w, tm):
    r, n = w.shape
    return pl.pallas_call(
        _w1_prep_kernel,
        out_shape=jax.ShapeDtypeStruct((r, n), BF16),
        grid=(r // tm,),
        in_specs=[pl.BlockSpec((tm, n), lambda i: (i, 0))],
        out_specs=pl.BlockSpec((tm, n), lambda i: (i, 0)),
        compiler_params=_params(("arbitrary",)),
        name="w1_prep",
    )(w)


def _ffn_kernel(te_ref, nu_ref, x_ref, w1_ref, b1_ref, w2_ref, b2_ref, o_ref):
    del te_ref
    used = pl.program_id(0) < nu_ref[0]

    @pl.when(used)
    def _():
        d = o_ref.shape[1]
        u = _dot(x_ref[:, :d].astype(BF16), w1_ref[0]) + b1_ref[0]
        acts = []
        for c in range(u.shape[1] // (2 * LANES)):
            glu = jnp.minimum(u[:, 2 * c * LANES:(2 * c + 1) * LANES], SWIGLU_LIMIT)
            lin = jnp.clip(u[:, (2 * c + 1) * LANES:(2 * c + 2) * LANES], -SWIGLU_LIMIT, SWIGLU_LIMIT)
            acts.append((glu * _sigmoid(SWIGLU_ALPHA * glu) * (lin + 1.0)).astype(BF16))
        y = _dot(jnp.concatenate(acts, axis=-1), w2_ref[0]) + b2_ref[0]
        weight = x_ref[:, d:]
        o_ref[...] = jnp.concatenate([y[:, c * LANES:(c + 1) * LANES] * weight for c in range(d // LANES)], axis=1)

    @pl.when(jnp.logical_not(used))
    def _():
        o_ref[...] = jnp.zeros_like(o_ref)


def _ffn(tile_expert, n_used, xs, w1, b1, w2, b2, tm):
    n_pad = xs.shape[0]
    d = w1.shape[1]
    f2 = w1.shape[2]
    f = w2.shape[1]
    grid_spec = pltpu.PrefetchScalarGridSpec(
        num_scalar_prefetch=2,
        grid=(n_pad // tm,),
        in_specs=[
            pl.BlockSpec((tm, xs.shape[1]), lambda i, te, nu: (jnp.minimum(i, nu[0] - 1), 0)),
            pl.BlockSpec((1, d, f2), lambda i, te, nu: (te[i], 0, 0)),
            pl.BlockSpec((1, 1, f2), lambda i, te, nu: (te[i], 0, 0)),
            pl.BlockSpec((1, f, d), lambda i, te, nu: (te[i], 0, 0)),
            pl.BlockSpec((1, 1, d), lambda i, te, nu: (te[i], 0, 0)),
        ],
        out_specs=pl.BlockSpec((tm, d), lambda i, te, nu: (i, 0)),
    )
    return pl.pallas_call(
        _ffn_kernel,
        out_shape=jax.ShapeDtypeStruct((n_pad, d), F32),
        grid_spec=grid_spec,
        compiler_params=_params(("arbitrary",)),
        name="expert_ffn",
    )(tile_expert, n_used, xs, w1, b1, w2, b2)


def _combine_kernel(src_ref, n8_ref, dst_ref, ys_ref, rown_ref, x_ref, g_ref, o_ref, buf_ref, sem, *, n_steps):
    rows = buf_ref.shape[1]
    step = pl.program_id(0)
    slot = step % 2

    def copies(table_row, buf_slot):
        def make_copy(src, dst, size):
            return pltpu.make_async_copy(ys_ref.at[pl.ds(dst, size)], buf_ref.at[buf_slot, pl.ds(src, size)],
                                         sem.at[buf_slot])
        return _segment_copies(src_ref, n8_ref, dst_ref, make_copy, x_ref.shape[0], table_row)

    @pl.when(step == 0)
    def _():
        buf_ref[...] = jnp.zeros_like(buf_ref)
        lax.fori_loop(0, N_EXPERTS, copies(0, 0)[0], 0)

    @pl.when(step + 1 < n_steps)
    def _():
        lax.fori_loop(0, N_EXPERTS, copies(step + 1, 1 - slot)[0], 0)

    onehot = _slots_by_token(rown_ref[...], rows).astype(BF16)
    lax.fori_loop(0, N_EXPERTS, copies(step, slot)[1], 0)
    ys = buf_ref[slot]
    y_hi = ys.astype(BF16)
    y_lo = (ys - y_hi.astype(F32)).astype(BF16)
    y = _dot(onehot, y_hi) + _dot(onehot, y_lo)
    o_ref[...] = _rms(x_ref[...] + y, g_ref[...])


def _combine(tables, ys, row_n, x, g, tt, rows):
    t, d = x.shape
    tile = lambda w: pl.BlockSpec((tt, w), lambda i, *_: (i, 0))
    grid_spec = pltpu.PrefetchScalarGridSpec(
        num_scalar_prefetch=3,
        grid=(t // tt,),
        in_specs=[pl.BlockSpec(memory_space=pl.ANY), tile(LANES), tile(d),
                  pl.BlockSpec((1, d), lambda i, *_: (0, 0))],
        out_specs=tile(d),
        scratch_shapes=[pltpu.VMEM((2, rows, d), F32), pltpu.SemaphoreType.DMA((2,))],
    )
    return pl.pallas_call(
        functools.partial(_combine_kernel, n_steps=t // tt),
        out_shape=jax.ShapeDtypeStruct((t, d), F32),
        grid_spec=grid_spec,
        compiler_params=_params(("arbitrary",)),
        name="combine",
    )(*tables, ys, row_n, x, g)


def _tile_sizes(seq):
    return dict(
        tm_proj=1024, tn_proj=P_WIDTH // 4,
        tm_rope=512,
        tq=256, tk=512,
        hgrn_chunks=4,
        tm_merge=512, tm_xattn=512,
        tm_router=512,
        tm_w1_prep=512,
        tm_ffn=512,
    )


def _layer(x, mem, positions, ts, mix_norm_g, w_in, cmp_pe, cmp_w1, cmp_b1, cmp_w2, cmp_b2, lb_logits, hgrn_norm_g,
           w_up_nsa, w_up_hgrn, w_out, xa_norm_g, xa_mem_norm_g, w_xq, w_xkv, w_xo, moe_norm_g, router_w, router_b,
           moe_w1, moe_b1, moe_w2, moe_b2, out_norm_g):
    b, s, d = x.shape
    t = b * s
    g, hg, dh = NSA_KV_GROUPS, NSA_Q_PER_GROUP, NSA_HEAD_DIM
    x2 = x.reshape(t, d)
    row = lambda v: v.reshape(1, -1).astype(F32)

    splits = [0]
    for w in (d, d, NSA_Q_W) + (NSA_KV_W,) * 6 + (3 * NSA_HEADS,) + (HGRN_W,) * 4:
        splits.append(splits[-1] + w)
    seg = lambda i: w_in[:, splits[i]:splits[i + 1]]
    (ga, gb, nq, kc, vc, ks, vs, kw, vw, ng, hq, hf, hi, hgate) = [seg(i) for i in range(14)]
    pad = jnp.zeros((d, P_WIDTH - COL_NG - 3 * NSA_HEADS), w_in.dtype)
    w_p = jnp.concatenate([ga, gb, hq, hf, hi, hgate, nq, ks, kw, kc, vc, vs, vw, ng, pad], axis=1).astype(BF16)

    p = _norm_matmul(x2, row(mix_norm_g), w_p, F32, ts["tm_proj"], ts["tn_proj"], "in_proj")

    half = dh // 2
    inv_freq = ROPE_THETA ** (-jnp.arange(half, dtype=F32) / half)
    invf = jnp.tile(inv_freq, LANES // half).reshape(1, LANES)
    q_r, kk_r = _rope(p, positions.reshape(t, 1), invf, ts["tm_rope"])
    tq = ts["tq"]
    nq = s // tq
    qt = q_r.reshape(b, nq, tq, g, hg, dh).transpose(0, 3, 1, 5, 4, 2).reshape(b, g, nq, dh, hg * tq)
    kk = kk_r.reshape(b, s, 2 * g, dh).transpose(0, 2, 1, 3)
    k_slc, k_win = kk[:, :g], kk[:, g:]
    nb = s // SLC_BLOCK
    block_onehot = (jnp.arange(s)[:, None] // SLC_BLOCK == jnp.arange(nb)[None, :]).astype(BF16)
    ks_aug = jnp.concatenate([k_slc, jnp.broadcast_to(block_onehot, (b, g, s, nb))], axis=-1)

    def values_t(v):
        vt = v.transpose(0, 1, 3, 2).astype(BF16)
        ones = jnp.ones(vt.shape[:2] + (1, vt.shape[3]), BF16)
        zeros = jnp.zeros(vt.shape[:2] + (BF16_SUBLANES - 1, vt.shape[3]), BF16)
        return jnp.concatenate([vt, ones, zeros], axis=2)

    vvt = values_t(p[:, COL_VSVW:COL_VSVW + 2 * NSA_KV_W].reshape(b, s, 2 * g, dh).transpose(0, 2, 1, 3))
    vt_slc, vt_win = vvt[:, :g], vvt[:, g:]
    gates = p[:, COL_NG:COL_NG + 3 * NSA_HEADS].reshape(b, nq, tq, g, hg, 3).transpose(0, 3, 1, 5, 4, 2)
    gates = gates.reshape(b, g, nq, 3, hg * tq)

    nr = s // CMP_STRIDE
    kcvc = p[:, COL_KCVC:COL_KCVC + 2 * NSA_KV_W].reshape(b, s, 2, g, dh).transpose(2, 0, 3, 1, 4)
    r = kcvc.reshape(2, b, g, nr, CMP_STRIDE * dh)
    pe = cmp_pe.reshape(2, 2, 1, CMP_STRIDE * dh)
    zeros_w2 = jnp.zeros_like(cmp_w2)
    w2p = jnp.stack([jnp.concatenate([cmp_w2, zeros_w2], axis=-1),
                     jnp.concatenate([zeros_w2, cmp_w2], axis=-1)], axis=1).astype(BF16)
    b2t = jnp.tile(cmp_b2, (1, g)).reshape(2, 1, LANES)
    pos_cmp = positions[:, CMP_BLOCK - 1::CMP_STRIDE]
    pos_cmp = jnp.pad(pos_cmp, ((0, 0), (0, nr - pos_cmp.shape[1]))).reshape(b, nr, 1)
    cmp = _compress(r, pe, cmp_w1.astype(BF16), cmp_b1.reshape(2, 1, CMP_HIDDEN), w2p, b2t, pos_cmp, invf)
    cmp = cmp.reshape(2, b, nr, g, dh).transpose(0, 1, 3, 2, 4)

    y_nsa = _nsa(qt, cmp[0], values_t(cmp[1]), ks_aug, vt_slc, k_win, vt_win, gates, tq, ts["tk"])
    y_nsa = y_nsa.reshape(b, g, nq, dh, hg, tq).transpose(0, 2, 5, 1, 4, 3).reshape(t, NSA_Q_W)

    y_hgrn = _hgrn(p, lb_logits.astype(F32), row(hgrn_norm_g), b, s, ts["hgrn_chunks"])

    x2 = _merge(x2, p, y_nsa, y_hgrn, w_up_nsa.astype(BF16), w_up_hgrn.astype(BF16), w_out.astype(BF16),
                ts["tm_merge"])

    n_mem = mem.shape[1]
    kv = _norm_matmul(mem.reshape(b * n_mem, d), row(xa_mem_norm_g), w_xkv.astype(BF16), BF16,
                      n_mem, 2 * XA_W, "mem_kv").reshape(b, n_mem, 2 * XA_W)
    x2 = _xattn(x2, row(xa_norm_g), w_xq.astype(BF16), kv, w_xo.astype(BF16), s, ts["tm_xattn"])

    n_exp = router_w.shape[1]
    rw = jnp.pad(router_w, ((0, 0), (0, LANES - n_exp))).astype(BF16)
    rb = jnp.pad(router_b, (0, LANES - n_exp)).reshape(1, LANES).astype(F32)
    assert n_exp == N_EXPERTS
    tt = ts["tm_router"]
    nt = t // tt
    hm, row_n, row_t, wt_t, cnt = _router(x2, row(moe_norm_g), rw, rb, tt)
    tm = ts["tm_ffn"]
    n8 = (cnt[:, 0, :n_exp].astype(jnp.int32) + SEG_ALIGN - 1) // SEG_ALIGN * SEG_ALIGN
    src_off = jnp.cumsum(n8, axis=1) - n8
    region = jnp.sum(n8, axis=0)
    padded = (region + tm - 1) // tm * tm
    ends = jnp.cumsum(padded)
    dst_off = (ends - padded)[None, :] + jnp.cumsum(n8, axis=0) - n8
    tables = tuple(jnp.concatenate([a, tail[None, :]], axis=0).reshape(-1) for a, tail in
                   ((src_off, jnp.zeros_like(region)), (n8, padded - region), (dst_off, ends - padded + region)))
    rows = tt * TOP_K + n_exp * SEG_ALIGN
    n_pad = (t * TOP_K + nt * n_exp * SEG_ALIGN + n_exp * tm + tm - 1) // tm * tm
    n_tiles = n_pad // tm
    tile_ids = jnp.arange(n_tiles, dtype=jnp.int32)
    tile_expert = jnp.sum(((ends // tm)[None, :] <= tile_ids[:, None]).astype(jnp.int32), axis=1)
    tile_expert = jnp.minimum(tile_expert, n_exp - 1)
    n_used = (ends[-1] // tm).reshape(1).astype(jnp.int32)

    xs = _dispatch(tables, n_used, hm, row_t, wt_t, n_pad, tt, rows, tm)
    f = moe_w2.shape[1]
    w1p = _w1_prep(moe_w1.reshape(n_exp * d, 2 * f), ts["tm_w1_prep"]).reshape(n_exp, d, 2 * f)
    b1p = moe_b1.reshape(n_exp, f // LANES, LANES, 2).transpose(0, 1, 3, 2).reshape(n_exp, 1, 2 * f)
    ys = _ffn(tile_expert, n_used, xs, w1p, b1p, moe_w2.astype(BF16), moe_b2.reshape(n_exp, 1, d), tm)
    out = _combine(tables, ys, row_n, x2, row(out_norm_g), tt, rows)
    return out.reshape(b, s, d)


def kernel(x, mem, positions, mix_norm_g, w_in, cmp_pe, cmp_w1, cmp_b1, cmp_w2, cmp_b2, hgrn_lb_logits, hgrn_norm_g, w_up_nsa, w_up_hgrn, w_out, xa_norm_g, xa_mem_norm_g, w_xq, w_xkv, w_xo, moe_norm_g, router_w, router_b, moe_w1, moe_b1, moe_w2, moe_b2, final_norm_g):
    depth = w_in.shape[0]
    assert depth == 1, "single-layer block: the final norm is fused into the last layer's combine"
    ts = _tile_sizes(x.shape[1])
    l = 0
    return _layer(x, mem, positions, ts, mix_norm_g[l], w_in[l], cmp_pe[l], cmp_w1[l], cmp_b1[l], cmp_w2[l], cmp_b2[l],
                  hgrn_lb_logits, hgrn_norm_g[l], w_up_nsa[l], w_up_hgrn[l], w_out[l], xa_norm_g[l], xa_mem_norm_g[l],
                  w_xq[l], w_xkv[l], w_xo[l], moe_norm_g[l], router_w[l], router_b[l], moe_w1[l], moe_b1[l], moe_w2[l],
                  moe_b2[l], final_norm_g)
```
